```python
import jax, jax.numpy as jnp
from jax import lax
import numpy as np

D_MODEL = 1024
BATCH = 4
SEQ = 4096
DEPTH = 2
DEC_BATCH = 8
DEC_SEQ = 32
PAST_LEN = 1024

CHUNK = 64
D_A = D_MODEL
CONV_A_W = 3
D_B = D_MODEL
CONV_B_W = 4
LRU_BLOCKS = 8
LRU_BW = D_B // LRU_BLOCKS
LRU_C = 8.0
D_C = D_MODEL
C_HEADS = 8
C_HD = D_C // C_HEADS
PEER_HEADS = 8
PEER_DQ = 256
PEER_DH = PEER_DQ // 2
N_KEYS = 128
N_EXPERTS = N_KEYS * N_KEYS
PEER_TOPK = 16
PEER_BLOCK = 128
ALPHA = (2.0 * DEPTH) ** 0.25
BETA = (8.0 * DEPTH) ** -0.25
LN_EPS = 1e-5
RMS_EPS = 1e-6
IN_SPLITS = (D_A, D_A, D_A, D_B, D_B, D_C, D_C, D_C, D_C, D_MODEL, D_MODEL, D_MODEL)
IN_COLS = sum(IN_SPLITS)

kernel_name = 'hybrid_conv_rglru_hgrn2_peer_stream_step'


def layer_norm(x, g, b):
    xf = x.astype(jnp.float32)
    mu = jnp.mean(xf, axis=-1, keepdims=True)
    var = jnp.mean(jnp.square(xf - mu), axis=-1, keepdims=True)
    return ((xf - mu) * lax.rsqrt(var + LN_EPS) * g + b).astype(x.dtype)


def causal_dwconv(u, past, w):
    width = w.shape[0]
    t_len = u.shape[1]
    full = jnp.concatenate([past.astype(u.dtype), u], axis=1)
    y = sum(full[:, k:k + t_len] * w[k] for k in range(width))
    return y, full[:, full.shape[1] - (width - 1):]


def rg_lru(x, w_a, b_a, w_x, b_x, lam, h0):
    bsz, t_len, _ = x.shape
    xf = x.astype(jnp.float32)
    xb = xf.reshape(bsz, t_len, LRU_BLOCKS, LRU_BW)
    r = jax.nn.sigmoid(jnp.einsum('btni,nij->btnj', xb, w_a).reshape(bsz, t_len, D_B) + b_a)
    gi = jax.nn.sigmoid(jnp.einsum('btni,nij->btnj', xb, w_x).reshape(bsz, t_len, D_B) + b_x)
    log_a = -LRU_C * r * jax.nn.softplus(-lam.astype(jnp.float32))
    a = jnp.exp(log_a)
    u = jnp.sqrt(-jnp.expm1(2.0 * log_a)) * (gi * xf)
    u = u.at[:, 0].add(a[:, 0] * h0.astype(jnp.float32))

    def combine(left, right):
        a1, b1 = left
        a2, b2 = right
        return a1 * a2, a2 * b1 + b2

    _, h = lax.associative_scan(combine, (a, u), axis=1)
    return h.astype(x.dtype), h[:, -1].astype(x.dtype)


def hgrn2_chunk(s0, blk):
    q, k, v, logf = blk
    l_len = q.shape[1]
    b = jnp.cumsum(logf, axis=1)
    mask = jnp.tril(jnp.ones((l_len, l_len), dtype=bool))
    diff = b[:, :, None] - b[:, None, :]
    decay = jnp.exp(jnp.where(mask[None, :, :, None, None], diff, -jnp.inf))
    att = jnp.einsum('bthd,bshd,btshd->bhts', q, k, decay)
    o = (jnp.einsum('bthd,bhde->bthe', q * jnp.exp(b), s0)
         + jnp.einsum('bhts,bshe->bthe', att, v))
    b_last = b[:, -1]
    s_new = (jnp.exp(b_last)[..., None] * s0
             + jnp.einsum('bshd,bshe->bhde', k * jnp.exp(b_last[:, None] - b), v))
    return s_new, o


def hgrn2(q, k, v, logf, s0):
    bsz, t_len, nh, _ = q.shape
    l_len = min(CHUNK, t_len)
    n_chunks = t_len // l_len

    def to_chunks(t):
        return t.reshape(bsz, n_chunks, l_len, nh, t.shape[-1]).swapaxes(0, 1)

    s_t, o = lax.scan(hgrn2_chunk, s0, (to_chunks(q), to_chunks(k), to_chunks(v), to_chunks(logf)))
    o = o.swapaxes(0, 1).reshape(bsz, t_len, nh, o.shape[-1])
    return o, s_t


def peer_block(xt, wq, keys, u_tab, v_tab):
    n = xt.shape[0]
    q = (xt @ wq).reshape(n, PEER_HEADS, 2, PEER_DH)
    s = jnp.einsum('nhpd,hpkd->nhpk', q, keys).astype(jnp.float32)
    sv, si = lax.top_k(s, PEER_TOPK)
    cand = (sv[:, :, 0, :, None] + sv[:, :, 1, None, :]).reshape(n, PEER_HEADS, PEER_TOPK * PEER_TOPK)
    cid = (si[:, :, 0, :, None] * N_KEYS + si[:, :, 1, None, :]).reshape(n, PEER_HEADS, PEER_TOPK * PEER_TOPK)
    top_s, top_p = lax.top_k(cand, PEER_TOPK)
    expert = jnp.take_along_axis(cid, top_p, axis=-1)
    g = jax.nn.softmax(top_s, axis=-1)
    h = jax.nn.gelu(jnp.einsum('nd,nhkd->nhk', xt, u_tab[expert]), approximate=False)
    return jnp.einsum('nhk,nhkd->nd', (g * h).astype(xt.dtype), v_tab[expert])


def peer(x, wq, keys, u_tab, v_tab):
    bsz, t_len, d = x.shape
    n = bsz * t_len
    blk = PEER_BLOCK if n % PEER_BLOCK == 0 else n
    xt = x.reshape(n // blk, blk, d)
    out = lax.map(lambda t: peer_block(t, wq, keys, u_tab, v_tab), xt)
    return out.reshape(bsz, t_len, d)


def trunk_layer(x, past_a, past_b, h0, s0, lb, w_in, b_in, conv_a_w, conv_b_w, conv_b_b,
                lru_wa, lru_ba, lru_wx, lru_bx, lru_lambda, hgrn_norm_g, w_out_a, w_out_b,
                w_out_c, w_o, ln1_g, ln1_b, peer_wq, peer_keys, peer_u, peer_v, ln2_g, ln2_b):
    bsz, t_len, _ = x.shape
    z = x @ w_in + b_in
    split_points = np.cumsum(IN_SPLITS)[:-1].tolist()
    a_b, a_c, a_x, b_x, b_g, c_q, c_f, c_i, c_g, g_a, g_b, g_c = jnp.split(z, split_points, axis=-1)

    conv_a, new_a = causal_dwconv(a_c * a_x, past_a, conv_a_w)
    y_a = (a_b * conv_a) @ w_out_a

    conv_b, new_b = causal_dwconv(b_x, past_b, conv_b_w)
    lru_out, h_t = rg_lru(conv_b + conv_b_b, lru_wa, lru_ba, lru_wx, lru_bx, lru_lambda, h0)
    y_b = (jax.nn.gelu(b_g, approximate=False) * lru_out) @ w_out_b

    def heads(t):
        return t.astype(jnp.float32).reshape(bsz, t_len, C_HEADS, C_HD)

    lb_h = lb.reshape(C_HEADS, C_HD)
    f = lb_h + (1.0 - lb_h) * jax.nn.sigmoid(heads(c_f))
    o, s_t = hgrn2(jax.nn.silu(heads(c_q)), 1.0 - f, heads(c_i), jnp.log(f), s0.astype(jnp.float32))
    o = o * lax.rsqrt(jnp.mean(jnp.square(o), axis=-1, keepdims=True) + RMS_EPS) * hgrn_norm_g
    o = o * jax.nn.silu(heads(c_g))
    y_c = o.reshape(bsz, t_len, D_C).astype(x.dtype) @ w_out_c

    merged = jax.nn.sigmoid(g_a) * y_a + jax.nn.sigmoid(g_b) * y_b + jax.nn.sigmoid(g_c) * y_c
    x = layer_norm(ALPHA * x + merged @ w_o, ln1_g, ln1_b)
    x = layer_norm(ALPHA * x + peer(x, peer_wq, peer_keys, peer_u, peer_v), ln2_g, ln2_b)
    return x, new_a, new_b, h_t, s_t.astype(x.dtype)


def setup_inputs(seed: int = 0) -> dict:
    key = jax.random.key(seed)
    ks = jax.random.split(key, 32)
    nrm = jax.random.normal
    f32 = jnp.float32
    a0 = jax.random.uniform(ks[15], (DEPTH, D_B), f32, 0.9, 0.999)
    a_root = a0 ** (1.0 / LRU_C)
    lru_lambda = jnp.log(a_root) - jnp.log1p(-a_root)
    return {
        'x_prompt': nrm(ks[0], (BATCH, SEQ, D_MODEL), f32),
        'x_sample': nrm(ks[1], (DEC_BATCH, DEC_SEQ, D_MODEL), f32),
        'state_conv_a': nrm(ks[2], (DEPTH, DEC_BATCH, CONV_A_W - 1, D_A), f32),
        'state_conv_b': nrm(ks[3], (DEPTH, DEC_BATCH, CONV_B_W - 1, D_B), f32),
        'state_lru': 0.5 * nrm(ks[4], (DEPTH, DEC_BATCH, D_B), f32),
        'state_hgrn': 0.5 * nrm(ks[5], (DEPTH, DEC_BATCH, C_HEADS, C_HD, C_HD), f32),
        'w_in': nrm(ks[6], (DEPTH, D_MODEL, IN_COLS), f32) * D_MODEL ** -0.5,
        'b_in': 0.02 * nrm(ks[7], (DEPTH, IN_COLS), f32),
        'conv_a_w': nrm(ks[8], (DEPTH, CONV_A_W, D_A), f32) * CONV_A_W ** -0.5,
        'conv_b_w': nrm(ks[9], (DEPTH, CONV_B_W, D_B), f32) * CONV_B_W ** -0.5,
        'conv_b_b': 0.02 * nrm(ks[10], (DEPTH, D_B), f32),
        'lru_wa': nrm(ks[11], (DEPTH, LRU_BLOCKS, LRU_BW, LRU_BW), f32) * LRU_BW ** -0.5,
        'lru_ba': 0.02 * nrm(ks[12], (DEPTH, D_B), f32),
        'lru_wx': nrm(ks[13], (DEPTH, LRU_BLOCKS, LRU_BW, LRU_BW), f32) * LRU_BW ** -0.5,
        'lru_bx': 0.02 * nrm(ks[14], (DEPTH, D_B), f32),
        'lru_lambda': lru_lambda,
        'hgrn_lb_logits': 0.5 * nrm(ks[16], (DEPTH, D_C), f32),
        'hgrn_norm_g': 1.0 + 0.05 * nrm(ks[17], (DEPTH, C_HD), f32),
        'w_out_a': nrm(ks[18], (DEPTH, D_A, D_MODEL), f32) * D_A ** -0.5,
        'w_out_b': nrm(ks[19], (DEPTH, D_B, D_MODEL), f32) * D_B ** -0.5,
        'w_out_c': nrm(ks[20], (DEPTH, D_C, D_MODEL), f32) * D_C ** -0.5,
        'w_o': nrm(ks[21], (DEPTH, D_MODEL, D_MODEL), f32) * (BETA * D_MODEL ** -0.5),
        'ln1_g': 1.0 + 0.05 * nrm(ks[22], (DEPTH, D_MODEL), f32),
        'ln1_b': 0.02 * nrm(ks[23], (DEPTH, D_MODEL), f32),
        'peer_wq': nrm(ks[24], (DEPTH, D_MODEL, PEER_HEADS * PEER_DQ), f32) * D_MODEL ** -0.5,
        'peer_keys': nrm(ks[25], (DEPTH, PEER_HEADS, 2, N_KEYS, PEER_DH), f32) * PEER_DH ** -0.5,
        'peer_u': nrm(ks[26], (DEPTH, N_EXPERTS, D_MODEL), f32) * D_MODEL ** -0.5,
        'peer_v': nrm(ks[27], (DEPTH, N_EXPERTS, D_MODEL), f32) * (BETA * PEER_HEADS ** -0.5),
        'ln2_g': 1.0 + 0.05 * nrm(ks[28], (DEPTH, D_MODEL), f32),
        'ln2_b': 0.02 * nrm(ks[29], (DEPTH, D_MODEL), f32),
    }


def reference(x_prompt, x_sample, state_conv_a, state_conv_b, state_lru, state_hgrn,
              w_in, b_in, conv_a_w, conv_b_w, conv_b_b, lru_wa, lru_ba, lru_wx, lru_bx,
              lru_lambda, hgrn_lb_logits, hgrn_norm_g, w_out_a, w_out_b, w_out_c, w_o,
              ln1_g, ln1_b, peer_wq, peer_keys, peer_u, peer_v, ln2_g, ln2_b):
    p = jax.nn.softmax(hgrn_lb_logits.astype(jnp.float32), axis=0)
    lbs = jnp.cumsum(p, axis=0) - p[0]
    xp, xs = x_prompt, x_sample
    bp = xp.shape[0]
    pa_l, pb_l, ph_l, ps_l = [], [], [], []
    sa_l, sb_l, sh_l, ss_l = [], [], [], []
    for l in range(DEPTH):
        lp = (w_in[l], b_in[l], conv_a_w[l], conv_b_w[l], conv_b_b[l], lru_wa[l], lru_ba[l],
              lru_wx[l], lru_bx[l], lru_lambda[l], hgrn_norm_g[l], w_out_a[l], w_out_b[l],
              w_out_c[l], w_o[l], ln1_g[l], ln1_b[l], peer_wq[l], peer_keys[l], peer_u[l],
              peer_v[l], ln2_g[l], ln2_b[l])
        xp, pa, pb, ph, ps = trunk_layer(
            xp,
            jnp.zeros((bp, CONV_A_W - 1, D_A), xp.dtype),
            jnp.zeros((bp, CONV_B_W - 1, D_B), xp.dtype),
            jnp.zeros((bp, D_B), xp.dtype),
            jnp.zeros((bp, C_HEADS, C_HD, C_HD), xp.dtype),
            lbs[l], *lp)
        xs, sa, sb, sh, ss = trunk_layer(
            xs, state_conv_a[l], state_conv_b[l], state_lru[l], state_hgrn[l], lbs[l], *lp)
        pa_l.append(pa); pb_l.append(pb); ph_l.append(ph); ps_l.append(ps)
        sa_l.append(sa); sb_l.append(sb); sh_l.append(sh); ss_l.append(ss)
    return (xp, xs,
            jnp.stack(pa_l), jnp.stack(pb_l), jnp.stack(ph_l), jnp.stack(ps_l),
            jnp.stack(sa_l), jnp.stack(sb_l), jnp.stack(sh_l), jnp.stack(ss_l))
```

```python
import functools

import jax
import jax.numpy as jnp
from jax import lax
from jax.experimental import pallas as pl
from jax.experimental.pallas import tpu as pltpu

D = 1024
DEPTH = 2
CONV_A_W = 3
CONV_B_W = 4
LRU_BLOCKS = 8
LRU_BW = D // LRU_BLOCKS
LRU_C = 8.0
HEADS = 8
HD = D // HEADS
PEER_HEADS = 8
PEER_DH = 128
N_KEYS = 128
N_EXPERTS = N_KEYS * N_KEYS
TOPK = 16
ALPHA = (2.0 * DEPTH) ** 0.25
LN_EPS = 1e-5
RMS_EPS = 1e-6
IN_COLS = 12 * D
SUB = 16
LANES = 128
SUBLANES = 8
VMEM_LIMIT = 56 * 1024 * 1024

OFF_AB, OFF_AC, OFF_AX, OFF_BX, OFF_BG, OFF_CQ, OFF_CF, OFF_CI, OFF_CG, OFF_GA, OFF_GB, OFF_GC = (
    i * D for i in range(12))

BF = jnp.bfloat16
F32 = jnp.float32


def _sigmoid(x):
    return 1.0 / (1.0 + jnp.exp(-x))


def _gelu(x):
    return 0.5 * x * (1.0 + lax.erf(x * (2.0 ** -0.5)))


def _layer_norm(x, g, b):
    mu = jnp.mean(x, axis=-1, keepdims=True)
    xc = x - mu
    var = jnp.mean(xc * xc, axis=-1, keepdims=True)
    return xc * lax.rsqrt(var + LN_EPS) * g + b


def _nt_dot(a, b):
    return lax.dot_general(a, b, (((1,), (1,)), ((), ())), preferred_element_type=F32)


def _tn_dot(a, b):
    return lax.dot_general(a, b, (((0,), (0,)), ((), ())), preferred_element_type=F32)


def _shift_rows(x, d, fill, rows):
    return jnp.where(rows >= d, pltpu.roll(x, d, 0), fill)


def _inproj_body(x_ref, w_ref, b_ref, o_ref):
    x = x_ref[...].astype(BF)
    o_ref[...] = jnp.dot(x, w_ref[...], preferred_element_type=F32) + b_ref[...]


def _in_proj(x2d, w_bf, b_row):
    n = x2d.shape[0]
    tm = min(n, 1024)
    tn = 2048
    return pl.pallas_call(
        _inproj_body,
        grid=(n // tm, IN_COLS // tn),
        in_specs=[pl.BlockSpec((tm, D), lambda i, j: (i, 0)),
                  pl.BlockSpec((D, tn), lambda i, j: (0, j)),
                  pl.BlockSpec((1, tn), lambda i, j: (0, j))],
        out_specs=pl.BlockSpec((tm, tn), lambda i, j: (i, j)),
        out_shape=jax.ShapeDtypeStruct((n, IN_COLS), F32),
        compiler_params=pltpu.CompilerParams(
            dimension_semantics=("arbitrary", "arbitrary"), vmem_limit_bytes=VMEM_LIMIT),
        name="in_proj",
    )(x2d, w_bf, b_row)


def _mixer_body(x_ref, z_ref, pa_ref, pb_ref, h0_ref, s0_ref, lb_ref, cwa_ref, cwb_ref, cbb_ref,
                wa_ref, ba_ref, wx_ref, bx_ref, lam_ref, ng_ref, woa_ref, wob_ref, woc_ref, wo_ref,
                g1_ref, b1_ref,
                x1_ref, na_ref, nb_ref, nh_ref, ns_ref,
                ua, ub, hc, st, qs, ks, vs, bs, att, oc, *, chunk):
    c = pl.program_id(1)
    last = pl.num_programs(1) - 1
    nsub = chunk // SUB
    rows = lax.broadcasted_iota(jnp.int32, (chunk, D), 0)

    @pl.when(c == 0)
    def _():
        ua[SUBLANES - 2:SUBLANES, :] = pa_ref[...]
        ub[SUBLANES - 3:SUBLANES, :] = pb_ref[...]
        hc[0:1, :] = h0_ref[...]
        for h in range(HEADS):
            st[h] = s0_ref[h].T

    u = z_ref[:, OFF_AC:OFF_AC + D] * z_ref[:, OFF_AX:OFF_AX + D]
    ua[SUBLANES:SUBLANES + chunk, :] = u
    conv_a = (cwa_ref[0:1, :] * ua[SUBLANES - 2:SUBLANES - 2 + chunk, :]
              + cwa_ref[1:2, :] * ua[SUBLANES - 1:SUBLANES - 1 + chunk, :]
              + cwa_ref[2:3, :] * u)
    tail_a = ua[SUBLANES + chunk - 2:SUBLANES + chunk, :]
    na_ref[...] = tail_a
    ua[SUBLANES - 2:SUBLANES, :] = tail_a
    y_a = jnp.dot((z_ref[:, OFF_AB:OFF_AB + D] * conv_a).astype(BF), woa_ref[...],
                  preferred_element_type=F32)
    merged = _sigmoid(z_ref[:, OFF_GA:OFF_GA + D]) * y_a

    xb = z_ref[:, OFF_BX:OFF_BX + D]
    ub[SUBLANES:SUBLANES + chunk, :] = xb
    xl = (cwb_ref[0:1, :] * ub[SUBLANES - 3:SUBLANES - 3 + chunk, :]
          + cwb_ref[1:2, :] * ub[SUBLANES - 2:SUBLANES - 2 + chunk, :]
          + cwb_ref[2:3, :] * ub[SUBLANES - 1:SUBLANES - 1 + chunk, :]
          + cwb_ref[3:4, :] * xb) + cbb_ref[...]
    tail_b = ub[SUBLANES + chunk - 3:SUBLANES + chunk, :]
    nb_ref[...] = tail_b
    ub[SUBLANES - 3:SUBLANES, :] = tail_b

    xl_bf = xl.astype(BF)
    ra = jnp.concatenate(
        [jnp.dot(xl_bf[:, n * LRU_BW:(n + 1) * LRU_BW], wa_ref[n], preferred_element_type=F32)
         for n in range(LRU_BLOCKS)], axis=1)
    rx = jnp.concatenate(
        [jnp.dot(xl_bf[:, n * LRU_BW:(n + 1) * LRU_BW], wx_ref[n], preferred_element_type=F32)
         for n in range(LRU_BLOCKS)], axis=1)
    r = _sigmoid(ra + ba_ref[...])
    gi = _sigmoid(rx + bx_ref[...])
    nlam = -lam_ref[...]
    softplus = jnp.maximum(nlam, 0.0) + jnp.log1p(jnp.exp(-jnp.abs(nlam)))
    log_a = (-LRU_C) * r * softplus
    a = jnp.exp(log_a)
    uu = jnp.sqrt(jnp.tanh(-log_a) * (a * a + 1.0)) * (gi * xl)
    d = 1
    while d < chunk:
        uu = a * _shift_rows(uu, d, 0.0, rows) + uu
        a = a * _shift_rows(a, d, 1.0, rows)
        d *= 2
    hh = uu + a * hc[0:1, :]
    h_last = hh[chunk - 1:chunk, :]
    hc[0:1, :] = h_last
    nh_ref[...] = h_last
    y_b = jnp.dot((_gelu(z_ref[:, OFF_BG:OFF_BG + D]) * hh).astype(BF), wob_ref[...],
                  preferred_element_type=F32)
    merged = merged + _sigmoid(z_ref[:, OFF_GB:OFF_GB + D]) * y_b

    rows_h = lax.broadcasted_iota(jnp.int32, (chunk, HD), 0)
    cols_a = lax.broadcasted_iota(jnp.int32, (SUB, chunk), 1)
    sub_rows = lax.broadcasted_iota(jnp.int32, (SUB, 1), 0)

    def head_step(h, carry):
        off = pl.multiple_of(h * HD, HD)
        lbh = lb_ref[:, pl.ds(off, HD)]
        f = lbh + (1.0 - lbh) * _sigmoid(z_ref[:, pl.ds(OFF_CF + off, HD)])
        k = 1.0 - f
        cq = z_ref[:, pl.ds(OFF_CQ + off, HD)]
        q = cq * _sigmoid(cq)
        v = z_ref[:, pl.ds(OFF_CI + off, HD)]
        b = jnp.log(f)
        dd = 1
        while dd < chunk:
            b = b + _shift_rows(b, dd, 0.0, rows_h)
            dd *= 2
        qs[...] = q
        ks[...] = k
        vs[...] = v
        bs[...] = b
        st_h = st[h]
        b_last = b[chunk - 1:chunk, :]
        o = _nt_dot((q * jnp.exp(b)).astype(BF), st_h.astype(BF))
        kdec = k * jnp.exp(b_last - b)
        st[h] = st_h * jnp.exp(b_last) + _tn_dot(v.astype(BF), kdec.astype(BF))

        att[0:SUB, :] = jnp.zeros((SUB, chunk), F32)
        for i in range(1, nsub):
            b_ref_i = bs[i * SUB - 1:i * SUB, :]
            q_i = qs[i * SUB:(i + 1) * SUB, :] * jnp.exp(bs[i * SUB:(i + 1) * SUB, :] - b_ref_i)
            k_i = k * jnp.exp(jnp.minimum(b_ref_i - b, 0.0))
            a_i = _nt_dot(q_i.astype(BF), k_i.astype(BF))
            att[i * SUB:(i + 1) * SUB, :] = jnp.where(cols_a < i * SUB, a_i, 0.0)
        o = o + jnp.dot(att[...].astype(BF), v.astype(BF), preferred_element_type=F32)

        o_parts = []
        for i in range(nsub):
            q_i = qs[i * SUB:(i + 1) * SUB, :]
            b_i = bs[i * SUB:(i + 1) * SUB, :]
            o_i = jnp.zeros((SUB, HD), F32)
            for s in range(SUB):
                r0 = i * SUB + s
                p = q_i * ks[r0:r0 + 1, :] * jnp.exp(jnp.minimum(b_i - bs[r0:r0 + 1, :], 0.0))
                a_col = jnp.sum(p, axis=-1, keepdims=True)
                a_col = jnp.where(sub_rows >= s, a_col, 0.0)
                o_i = o_i + a_col * vs[r0:r0 + 1, :]
            o_parts.append(o_i)
        o = o + jnp.concatenate(o_parts, axis=0)

        o = o * lax.rsqrt(jnp.mean(o * o, axis=-1, keepdims=True) + RMS_EPS) * ng_ref[...]
        cg = z_ref[:, pl.ds(OFF_CG + off, HD)]
        oc[:, pl.ds(off, HD)] = o * (cg * _sigmoid(cg))
        return carry

    lax.fori_loop(0, HEADS, head_step, 0)

    @pl.when(c == last)
    def _():
        for h in range(HEADS):
            ns_ref[h] = st[h].T

    y_c = jnp.dot(oc[...].astype(BF), woc_ref[...], preferred_element_type=F32)
    merged = merged + _sigmoid(z_ref[:, OFF_GC:OFF_GC + D]) * y_c
    res = ALPHA * x_ref[...] + jnp.dot(merged.astype(BF), wo_ref[...], preferred_element_type=F32)
    x1_ref[...] = _layer_norm(res, g1_ref[...], b1_ref[...])


def _mixer(x, z, past_a, past_b, h0, s0, lb, lw):
    bsz, t_len, _ = x.shape
    chunk = min(t_len, 128)
    assert t_len % chunk == 0 and chunk % SUB == 0
    nc = t_len // chunk
    full2 = lambda shape: pl.BlockSpec(shape, lambda b, c: (0, 0))
    full3 = lambda shape: pl.BlockSpec(shape, lambda b, c: (0, 0, 0))
    in_specs = [
        pl.BlockSpec((None, chunk, D), lambda b, c: (b, c, 0)),
        pl.BlockSpec((None, chunk, IN_COLS), lambda b, c: (b, c, 0)),
        pl.BlockSpec((None, CONV_A_W - 1, D), lambda b, c: (b, 0, 0)),
        pl.BlockSpec((None, CONV_B_W - 1, D), lambda b, c: (b, 0, 0)),
        pl.BlockSpec((None, 1, D), lambda b, c: (b, 0, 0)),
        pl.BlockSpec((None, HEADS, HD, HD), lambda b, c: (b, 0, 0, 0)),
        full2((1, D)),
        full2((CONV_A_W, D)), full2((CONV_B_W, D)), full2((1, D)),
        full3((LRU_BLOCKS, LRU_BW, LRU_BW)), full2((1, D)),
        full3((LRU_BLOCKS, LRU_BW, LRU_BW)), full2((1, D)),
        full2((1, D)), full2((1, HD)),
        full2((D, D)), full2((D, D)), full2((D, D)), full2((D, D)),
        full2((1, D)), full2((1, D)),
    ]
    out_specs = [
        pl.BlockSpec((None, chunk, D), lambda b, c: (b, c, 0)),
        pl.BlockSpec((None, CONV_A_W - 1, D), lambda b, c: (b, 0, 0)),
        pl.BlockSpec((None, CONV_B_W - 1, D), lambda b, c: (b, 0, 0)),
        pl.BlockSpec((None, 1, D), lambda b, c: (b, 0, 0)),
        pl.BlockSpec((None, HEADS, HD, HD), lambda b, c: (b, 0, 0, 0)),
    ]
    out_shape = [
        jax.ShapeDtypeStruct((bsz, t_len, D), F32),
        jax.ShapeDtypeStruct((bsz, CONV_A_W - 1, D), F32),
        jax.ShapeDtypeStruct((bsz, CONV_B_W - 1, D), F32),
        jax.ShapeDtypeStruct((bsz, 1, D), F32),
        jax.ShapeDtypeStruct((bsz, HEADS, HD, HD), F32),
    ]
    scratch = [
        pltpu.VMEM((SUBLANES + chunk, D), F32),
        pltpu.VMEM((SUBLANES + chunk, D), F32),
        pltpu.VMEM((SUBLANES, D), F32),
        pltpu.VMEM((HEADS, HD, HD), F32),
        pltpu.VMEM((chunk, HD), F32),
        pltpu.VMEM((chunk, HD), F32),
        pltpu.VMEM((chunk, HD), F32),
        pltpu.VMEM((chunk, HD), F32),
        pltpu.VMEM((chunk, chunk), F32),
        pltpu.VMEM((chunk, D), F32),
    ]
    return pl.pallas_call(
        functools.partial(_mixer_body, chunk=chunk),
        grid=(bsz, nc),
        in_specs=in_specs, out_specs=out_specs, out_shape=out_shape, scratch_shapes=scratch,
        compiler_params=pltpu.CompilerParams(
            dimension_semantics=("arbitrary", "arbitrary"), vmem_limit_bytes=VMEM_LIMIT),
        name="mixer",
    )(x, z, past_a, past_b, h0, s0, lb, lw["conv_a_w"], lw["conv_b_w"], lw["conv_b_b"],
      lw["lru_wa"], lw["lru_ba"], lw["lru_wx"], lw["lru_bx"], lw["lru_lambda"], lw["hgrn_norm_g"],
      lw["w_out_a"], lw["w_out_b"], lw["w_out_c"], lw["w_o"], lw["ln1_g"], lw["ln1_b"])


def _top_rows(s, n_top, rows):
    n_rows = s.shape[0]
    out = []
    for _ in range(n_top):
        m = jnp.max(s, axis=0, keepdims=True)
        first = jnp.min(jnp.where(s == m, rows, n_rows), axis=0, keepdims=True)
        s = jnp.where(rows == first, -jnp.inf, s)
        out.append(m)
    return jnp.concatenate(out, axis=0)


def _scores_body(x_ref, wq_ref, keys_ref, xt_ref, s1_ref, s2_ref, p1_ref, p2_ref, tau_ref, qbuf, *, tn):
    xt_ref[...] = x_ref[...].T.astype(BF)
    qbuf[...] = jnp.dot(x_ref[...].astype(BF), wq_ref[...], preferred_element_type=F32).astype(BF)
    rows = lax.broadcasted_iota(jnp.int32, (N_KEYS, LANES), 0)
    n_cand = 2 * TOPK + 6 * SUBLANES
    rows_c = lax.broadcasted_iota(jnp.int32, (n_cand, LANES), 0)

    def step(idx, carry):
        h = idx // (tn // LANES)
        t0 = pl.multiple_of((idx % (tn // LANES)) * LANES, LANES)
        c0 = pl.multiple_of(h * (2 * PEER_DH), 2 * PEER_DH)
        lanes = pl.ds(t0, LANES)
        s1 = _nt_dot(keys_ref[h, 0], qbuf[pl.ds(t0, LANES), pl.ds(c0, PEER_DH)])
        s2 = _nt_dot(keys_ref[h, 1], qbuf[pl.ds(t0, LANES), pl.ds(c0 + PEER_DH, PEER_DH)])
        a = _top_rows(s1, TOPK, rows)
        b = _top_rows(s2, TOPK, rows)
        cand = jnp.concatenate(
            [a[0:1] + b]
            + [a[i:i + 1] + b[0:SUBLANES] for i in range(1, SUBLANES)]
            + [a[SUBLANES:TOPK] + b[0:1]], axis=0)
        top = _top_rows(cand, TOPK, rows_c)
        z_sum = jnp.sum(jnp.exp(top - top[0:1]), axis=0, keepdims=True)
        s1_ref[h, :, lanes] = s1
        s2_ref[h, :, lanes] = s2
        p1_ref[h, :, lanes] = jnp.exp(s1 - a[0:1]) / z_sum
        p2_ref[h, :, lanes] = jnp.exp(s2 - b[0:1])
        tau_ref[h, :, lanes] = top[TOPK - 1:TOPK]
        return carry

    lax.fori_loop(0, PEER_HEADS * (tn // LANES), step, 0)


def _peer_scores(x2d, wq_bf, keys_bf, tn):
    n = x2d.shape[0]
    row_spec = pl.BlockSpec((PEER_HEADS, N_KEYS, tn), lambda i: (0, 0, i))
    row_shape = jax.ShapeDtypeStruct((PEER_HEADS, N_KEYS, n), F32)
    return pl.pallas_call(
        functools.partial(_scores_body, tn=tn),
        grid=(n // tn,),
        in_specs=[pl.BlockSpec((tn, D), lambda i: (i, 0)),
                  pl.BlockSpec((D, 2 * PEER_HEADS * PEER_DH), lambda i: (0, 0)),
                  pl.BlockSpec((PEER_HEADS, 2, N_KEYS, PEER_DH), lambda i: (0, 0, 0, 0))],
        out_specs=[pl.BlockSpec((D, tn), lambda i: (0, i)), row_spec, row_spec, row_spec, row_spec,
                   pl.BlockSpec((PEER_HEADS, 1, tn), lambda i: (0, 0, i))],
        out_shape=[jax.ShapeDtypeStruct((D, n), BF), row_shape, row_shape, row_shape, row_shape,
                   jax.ShapeDtypeStruct((PEER_HEADS, 1, n), F32)],
        scratch_shapes=[pltpu.VMEM((tn, 2 * PEER_HEADS * PEER_DH), BF)],
        compiler_params=pltpu.CompilerParams(
            dimension_semantics=("arbitrary",), vmem_limit_bytes=VMEM_LIMIT),
        name="peer_scores",
    )(x2d, wq_bf, keys_bf)


def _dense_body(xt_ref, x_ref, u_ref, vt_ref, s1_ref, s2_ref, p1_ref, p2_ref, tau_ref, g2_ref, b2_ref,
                y_ref, acc, hbuf, gbuf, *, tn, eb):
    j = pl.program_id(1)
    keys_per_block = eb // N_KEYS
    n_tc = tn // LANES

    @pl.when(j == 0)
    def _():
        acc[...] = jnp.zeros((D, tn), F32)

    hbuf[...] = jnp.dot(u_ref[...], xt_ref[...], preferred_element_type=F32)

    def key_step(il, carry):
        r0 = pl.multiple_of(il * N_KEYS, N_KEYS)
        for tc in range(n_tc):
            lanes = slice(tc * LANES, (tc + 1) * LANES)
            w = jnp.zeros((N_KEYS, LANES), F32)
            for h in range(PEER_HEADS):
                ssum = s1_ref[h, il, :, lanes] + s2_ref[h, :, lanes]
                keep = ssum >= tau_ref[h, :, lanes]
                w = w + jnp.where(keep, p1_ref[h, il, :, lanes] * p2_ref[h, :, lanes], 0.0)
            gbuf[pl.ds(r0, N_KEYS), lanes] = (w * _gelu(hbuf[pl.ds(r0, N_KEYS), lanes])).astype(BF)
        return carry

    lax.fori_loop(0, keys_per_block, key_step, 0)
    acc[...] += jnp.dot(vt_ref[...], gbuf[...], preferred_element_type=F32)

    @pl.when(j == pl.num_programs(1) - 1)
    def _():
        y_ref[...] = _layer_norm(ALPHA * x_ref[...] + acc[...].T, g2_ref[...], b2_ref[...])


def _peer_dense(xt_bf, x2d, u_bf, vt_bf, s1, s2, p1, p2, tau, g2, b2, tn):
    n = x2d.shape[0]
    eb = 1024
    row_spec = pl.BlockSpec((PEER_HEADS, N_KEYS, tn), lambda i, j: (0, 0, i))
    key_spec = pl.BlockSpec((PEER_HEADS, eb // N_KEYS, 1, tn), lambda i, j: (0, j, 0, i))
    s1 = s1.reshape(PEER_HEADS, N_KEYS, 1, n)
    p1 = p1.reshape(PEER_HEADS, N_KEYS, 1, n)
    return pl.pallas_call(
        functools.partial(_dense_body, tn=tn, eb=eb),
        grid=(n // tn, N_EXPERTS // eb),
        in_specs=[pl.BlockSpec((D, tn), lambda i, j: (0, i)),
                  pl.BlockSpec((tn, D), lambda i, j: (i, 0)),
                  pl.BlockSpec((eb, D), lambda i, j: (j, 0)),
                  pl.BlockSpec((D, eb), lambda i, j: (0, j)),
                  key_spec, row_spec, key_spec, row_spec,
                  pl.BlockSpec((PEER_HEADS, 1, tn), lambda i, j: (0, 0, i)),
                  pl.BlockSpec((1, D), lambda i, j: (0, 0)),
                  pl.BlockSpec((1, D), lambda i, j: (0, 0))],
        out_specs=pl.BlockSpec((tn, D), lambda i, j: (i, 0)),
        out_shape=jax.ShapeDtypeStruct((n, D), F32),
        scratch_shapes=[pltpu.VMEM((D, tn), F32), pltpu.VMEM((eb, tn), F32), pltpu.VMEM((eb, tn), BF)],
        compiler_params=pltpu.CompilerParams(
            dimension_semantics=("arbitrary", "arbitrary"), vmem_limit_bytes=VMEM_LIMIT),
        name="peer_dense",
    )(xt_bf, x2d, u_bf, vt_bf, s1, s2, p1, p2, tau, g2, b2)


def _trunk_layer(x, past_a, past_b, h0, s0, lb, lw):
    bsz, t_len, _ = x.shape
    n = bsz * t_len
    z = _in_proj(x.reshape(n, D), lw["w_in"], lw["b_in"]).reshape(bsz, t_len, IN_COLS)
    x1, new_a, new_b, h_t, s_t = _mixer(x, z, past_a, past_b, h0, s0, lb, lw)
    x1 = x1.reshape(n, D)
    tn = min(n, 512)
    xt, s1, s2, p1, p2, tau = _peer_scores(x1, lw["peer_wq"], lw["peer_keys"], tn)
    x2 = _peer_dense(xt, x1, lw["peer_u"], lw["peer_vt"], s1, s2, p1, p2, tau,
                     lw["ln2_g"], lw["ln2_b"], tn)
    return x2.reshape(bsz, t_len, D), new_a, new_b, h_t.reshape(bsz, D), s_t


def kernel(x_prompt, x_sample, state_conv_a, state_conv_b, state_lru, state_hgrn, w_in, b_in, conv_a_w,
           conv_b_w, conv_b_b, lru_wa, lru_ba, lru_wx, lru_bx, lru_lambda, hgrn_lb_logits, hgrn_norm_g,
           w_out_a, w_out_b, w_out_c, w_o, ln1_g, ln1_b, peer_wq, peer_keys, peer_u, peer_v, ln2_g,
           ln2_b):
    p = jax.nn.softmax(hgrn_lb_logits.astype(F32), axis=0)
    lbs = jnp.cumsum(p, axis=0) - p[0]
    bp = x_prompt.shape[0]
    xp, xs = x_prompt, x_sample
    outs_p, outs_s = [], []
    for l in range(DEPTH):
        row = lambda a: a[l].reshape(1, -1)
        lw = dict(
            w_in=w_in[l].astype(BF), b_in=row(b_in),
            conv_a_w=conv_a_w[l], conv_b_w=conv_b_w[l], conv_b_b=row(conv_b_b),
            lru_wa=lru_wa[l].astype(BF), lru_ba=row(lru_ba), lru_wx=lru_wx[l].astype(BF),
            lru_bx=row(lru_bx), lru_lambda=row(lru_lambda), hgrn_norm_g=row(hgrn_norm_g),
            w_out_a=w_out_a[l].astype(BF), w_out_b=w_out_b[l].astype(BF),
            w_out_c=w_out_c[l].astype(BF), w_o=w_o[l].astype(BF),
            ln1_g=row(ln1_g), ln1_b=row(ln1_b),
            peer_wq=peer_wq[l].astype(BF), peer_keys=peer_keys[l].astype(BF),
            peer_u=peer_u[l].astype(BF), peer_vt=peer_v[l].T.astype(BF),
            ln2_g=row(ln2_g), ln2_b=row(ln2_b))
        lb = lbs[l].reshape(1, D)
        xp, pa, pb, ph, ps = _trunk_layer(
            xp, jnp.zeros((bp, CONV_A_W - 1, D), F32), jnp.zeros((bp, CONV_B_W - 1, D), F32),
            jnp.zeros((bp, 1, D), F32), jnp.zeros((bp, HEADS, HD, HD), F32), lb, lw)
        xs, sa, sb, sh, ss = _trunk_layer(
            xs, state_conv_a[l], state_conv_b[l], state_lru[l][:, None, :], state_hgrn[l], lb, lw)
        outs_p.append((pa, pb, ph, ps))
        outs_s.append((sa, sb, sh, ss))
    stack = lambda outs, k: jnp.stack([o[k] for o in outs])
    return (xp, xs,
            stack(outs_p, 0), stack(outs_p, 1), stack(outs_p, 2), stack(outs_p, 3),
            stack(outs_s, 0), stack(outs_s, 1), stack(outs_s, 2), stack(outs_s, 3))
```

```python
import functools

import jax
import jax.numpy as jnp
from jax import lax
from jax.experimental import pallas as pl
from jax.experimental.pallas import tpu as pltpu

D = 1024
DEPTH = 2
CONV_A_W = 3
CONV_B_W = 4
LRU_BLOCKS = 8
LRU_BW = D // LRU_BLOCKS
LRU_C = 8.0
HEADS = 8
HD = D // HEADS
PEER_HEADS = 8
PEER_DH = 128
N_KEYS = 128
N_EXPERTS = N_KEYS * N_KEYS
TOPK = 16
ALPHA = (2.0 * DEPTH) ** 0.25
LN_EPS = 1e-5
RMS_EPS = 1e-6
IN_COLS = 12 * D
SUB = 16
LANES = 128
SUBLANES = 8
VMEM_LIMIT = 56 * 1024 * 1024
GATE_KEYS = 4
GATE_ROWS = 64

OFF_AB, OFF_AC, OFF_AX, OFF_BX, OFF_BG, OFF_CQ, OFF_CF, OFF_CI, OFF_CG, OFF_GA, OFF_GB, OFF_GC = (
    i * D for i in range(12))

BF = jnp.bfloat16
F32 = jnp.float32


def _sigmoid(x):
    return 1.0 / (1.0 + jnp.exp(-x))


def _gelu(x):
    return 0.5 * x * (1.0 + lax.erf(x * (2.0 ** -0.5)))


def _layer_norm(x, g, b):
    mu = jnp.mean(x, axis=-1, keepdims=True)
    xc = x - mu
    var = jnp.mean(xc * xc, axis=-1, keepdims=True)
    return xc * lax.rsqrt(var + LN_EPS) * g + b


def _nt_dot(a, b):
    return lax.dot_general(a, b, (((1,), (1,)), ((), ())), preferred_element_type=F32)


def _tn_dot(a, b):
    return lax.dot_general(a, b, (((0,), (0,)), ((), ())), preferred_element_type=F32)


def _shift_rows(x, d, fill, rows):
    return jnp.where(rows >= d, pltpu.roll(x, d, 0), fill)


def _inproj_body(x_ref, w_ref, b_ref, o_ref):
    x = x_ref[...].astype(BF)
    o_ref[...] = jnp.dot(x, w_ref[...], preferred_element_type=F32) + b_ref[...]


def _in_proj(x2d, w_bf, b_row):
    n = x2d.shape[0]
    tm = min(n, 1024)
    tn = 2048
    return pl.pallas_call(
        _inproj_body,
        grid=(n // tm, IN_COLS // tn),
        in_specs=[pl.BlockSpec((tm, D), lambda i, j: (i, 0)),
                  pl.BlockSpec((D, tn), lambda i, j: (0, j)),
                  pl.BlockSpec((1, tn), lambda i, j: (0, j))],
        out_specs=pl.BlockSpec((tm, tn), lambda i, j: (i, j)),
        out_shape=jax.ShapeDtypeStruct((n, IN_COLS), F32),
        compiler_params=pltpu.CompilerParams(
            dimension_semantics=("arbitrary", "arbitrary"), vmem_limit_bytes=VMEM_LIMIT),
        name="in_proj",
    )(x2d, w_bf, b_row)


def _mixer_body(x_ref, z_ref, pa_ref, pb_ref, h0_ref, s0_ref, lb_ref, cwa_ref, cwb_ref, cbb_ref,
                wa_ref, ba_ref, wx_ref, bx_ref, lam_ref, ng_ref, woa_ref, wob_ref, woc_ref, wo_ref,
                g1_ref, b1_ref,
                x1_ref, na_ref, nb_ref, nh_ref, ns_ref,
                ua, ub, hc, st, qs, ks, vs, bs, att, oc, *, chunk):
    c = pl.program_id(1)
    last = pl.num_programs(1) - 1
    nsub = chunk // SUB
    rows = lax.broadcasted_iota(jnp.int32, (chunk, D), 0)

    @pl.when(c == 0)
    def _():
        ua[SUBLANES - 2:SUBLANES, :] = pa_ref[...]
        ub[SUBLANES - 3:SUBLANES, :] = pb_ref[...]
        hc[0:1, :] = h0_ref[...]
        for h in range(HEADS):
            st[h] = s0_ref[h].T

    u = z_ref[:, OFF_AC:OFF_AC + D] * z_ref[:, OFF_AX:OFF_AX + D]
    ua[SUBLANES:SUBLANES + chunk, :] = u
    conv_a = (cwa_ref[0:1, :] * ua[SUBLANES - 2:SUBLANES - 2 + chunk, :]
              + cwa_ref[1:2, :] * ua[SUBLANES - 1:SUBLANES - 1 + chunk, :]
              + cwa_ref[2:3, :] * u)
    tail_a = ua[SUBLANES + chunk - 2:SUBLANES + chunk, :]
    na_ref[...] = tail_a
    ua[SUBLANES - 2:SUBLANES, :] = tail_a
    y_a = jnp.dot((z_ref[:, OFF_AB:OFF_AB + D] * conv_a).astype(BF), woa_ref[...],
                  preferred_element_type=F32)
    merged = _sigmoid(z_ref[:, OFF_GA:OFF_GA + D]) * y_a

    xb = z_ref[:, OFF_BX:OFF_BX + D]
    ub[SUBLANES:SUBLANES + chunk, :] = xb
    xl = (cwb_ref[0:1, :] * ub[SUBLANES - 3:SUBLANES - 3 + chunk, :]
          + cwb_ref[1:2, :] * ub[SUBLANES - 2:SUBLANES - 2 + chunk, :]
          + cwb_ref[2:3, :] * ub[SUBLANES - 1:SUBLANES - 1 + chunk, :]
          + cwb_ref[3:4, :] * xb) + cbb_ref[...]
    tail_b = ub[SUBLANES + chunk - 3:SUBLANES + chunk, :]
    nb_ref[...] = tail_b
    ub[SUBLANES - 3:SUBLANES, :] = tail_b

    xl_bf = xl.astype(BF)
    ra = jnp.concatenate(
        [jnp.dot(xl_bf[:, n * LRU_BW:(n + 1) * LRU_BW], wa_ref[n], preferred_element_type=F32)
         for n in range(LRU_BLOCKS)], axis=1)
    rx = jnp.concatenate(
        [jnp.dot(xl_bf[:, n * LRU_BW:(n + 1) * LRU_BW], wx_ref[n], preferred_element_type=F32)
         for n in range(LRU_BLOCKS)], axis=1)
    r = _sigmoid(ra + ba_ref[...])
    gi = _sigmoid(rx + bx_ref[...])
    nlam = -lam_ref[...]
    softplus = jnp.maximum(nlam, 0.0) + jnp.log1p(jnp.exp(-jnp.abs(nlam)))
    log_a = (-LRU_C) * r * softplus
    a = jnp.exp(log_a)
    uu = jnp.sqrt(jnp.tanh(-log_a) * (a * a + 1.0)) * (gi * xl)
    d = 1
    while d < chunk:
        uu = a * _shift_rows(uu, d, 0.0, rows) + uu
        a = a * _shift_rows(a, d, 1.0, rows)
        d *= 2
    hh = uu + a * hc[0:1, :]
    h_last = hh[chunk - 1:chunk, :]
    hc[0:1, :] = h_last
    nh_ref[...] = h_last
    y_b = jnp.dot((_gelu(z_ref[:, OFF_BG:OFF_BG + D]) * hh).astype(BF), wob_ref[...],
                  preferred_element_type=F32)
    merged = merged + _sigmoid(z_ref[:, OFF_GB:OFF_GB + D]) * y_b

    rows_h = lax.broadcasted_iota(jnp.int32, (chunk, HD), 0)
    cols_a = lax.broadcasted_iota(jnp.int32, (SUB, chunk), 1)
    sub_rows = lax.broadcasted_iota(jnp.int32, (SUB, 1), 0)

    def head_step(h, carry):
        off = pl.multiple_of(h * HD, HD)
        lbh = lb_ref[:, pl.ds(off, HD)]
        f = lbh + (1.0 - lbh) * _sigmoid(z_ref[:, pl.ds(OFF_CF + off, HD)])
        k = 1.0 - f
        cq = z_ref[:, pl.ds(OFF_CQ + off, HD)]
        q = cq * _sigmoid(cq)
        v = z_ref[:, pl.ds(OFF_CI + off, HD)]
        b = jnp.log(f)
        dd = 1
        while dd < chunk:
            b = b + _shift_rows(b, dd, 0.0, rows_h)
            dd *= 2
        qs[...] = q
        ks[...] = k
        vs[...] = v
        bs[...] = b
        st_h = st[h]
        b_last = b[chunk - 1:chunk, :]
        o = _nt_dot((q * jnp.exp(b)).astype(BF), st_h.astype(BF))
        kdec = k * jnp.exp(b_last - b)
        st[h] = st_h * jnp.exp(b_last) + _tn_dot(v.astype(BF), kdec.astype(BF))

        att[0:SUB, :] = jnp.zeros((SUB, chunk), F32)
        for i in range(1, nsub):
            b_ref_i = bs[i * SUB - 1:i * SUB, :]
            q_i = qs[i * SUB:(i + 1) * SUB, :] * jnp.exp(bs[i * SUB:(i + 1) * SUB, :] - b_ref_i)
            k_i = k * jnp.exp(jnp.minimum(b_ref_i - b, 0.0))
            a_i = _nt_dot(q_i.astype(BF), k_i.astype(BF))
            att[i * SUB:(i + 1) * SUB, :] = jnp.where(cols_a < i * SUB, a_i, 0.0)
        o = o + jnp.dot(att[...].astype(BF), v.astype(BF), preferred_element_type=F32)

        o_parts = []
        for i in range(nsub):
            q_i = qs[i * SUB:(i + 1) * SUB, :]
            b_i = bs[i * SUB:(i + 1) * SUB, :]
            o_i = jnp.zeros((SUB, HD), F32)
            for s in range(SUB):
                r0 = i * SUB + s
                p = q_i * ks[r0:r0 + 1, :] * jnp.exp(jnp.minimum(b_i - bs[r0:r0 + 1, :], 0.0))
                a_col = jnp.sum(p, axis=-1, keepdims=True)
                a_col = jnp.where(sub_rows >= s, a_col, 0.0)
                o_i = o_i + a_col * vs[r0:r0 + 1, :]
            o_parts.append(o_i)
        o = o + jnp.concatenate(o_parts, axis=0)

        o = o * lax.rsqrt(jnp.mean(o * o, axis=-1, keepdims=True) + RMS_EPS) * ng_ref[...]
        cg = z_ref[:, pl.ds(OFF_CG + off, HD)]
        oc[:, pl.ds(off, HD)] = o * (cg * _sigmoid(cg))
        return carry

    lax.fori_loop(0, HEADS, head_step, 0)

    @pl.when(c == last)
    def _():
        for h in range(HEADS):
            ns_ref[h] = st[h].T

    y_c = jnp.dot(oc[...].astype(BF), woc_ref[...], preferred_element_type=F32)
    merged = merged + _sigmoid(z_ref[:, OFF_GC:OFF_GC + D]) * y_c
    res = ALPHA * x_ref[...] + jnp.dot(merged.astype(BF), wo_ref[...], preferred_element_type=F32)
    x1_ref[...] = _layer_norm(res, g1_ref[...], b1_ref[...])


def _mixer(x, z, past_a, past_b, h0, s0, lb, lw):
    bsz, t_len, _ = x.shape
    chunk = min(t_len, 128)
    assert t_len % chunk == 0 and chunk % SUB == 0
    nc = t_len // chunk
    full2 = lambda shape: pl.BlockSpec(shape, lambda b, c: (0, 0))
    full3 = lambda shape: pl.BlockSpec(shape, lambda b, c: (0, 0, 0))
    in_specs = [
        pl.BlockSpec((None, chunk, D), lambda b, c: (b, c, 0)),
        pl.BlockSpec((None, chunk, IN_COLS), lambda b, c: (b, c, 0)),
        pl.BlockSpec((None, CONV_A_W - 1, D), lambda b, c: (b, 0, 0)),
        pl.BlockSpec((None, CONV_B_W - 1, D), lambda b, c: (b, 0, 0)),
        pl.BlockSpec((None, 1, D), lambda b, c: (b, 0, 0)),
        pl.BlockSpec((None, HEADS, HD, HD), lambda b, c: (b, 0, 0, 0)),
        full2((1, D)),
        full2((CONV_A_W, D)), full2((CONV_B_W, D)), full2((1, D)),
        full3((LRU_BLOCKS, LRU_BW, LRU_BW)), full2((1, D)),
        full3((LRU_BLOCKS, LRU_BW, LRU_BW)), full2((1, D)),
        full2((1, D)), full2((1, HD)),
        full2((D, D)), full2((D, D)), full2((D, D)), full2((D, D)),
        full2((1, D)), full2((1, D)),
    ]
    out_specs = [
        pl.BlockSpec((None, chunk, D), lambda b, c: (b, c, 0)),
        pl.BlockSpec((None, CONV_A_W - 1, D), lambda b, c: (b, 0, 0)),
        pl.BlockSpec((None, CONV_B_W - 1, D), lambda b, c: (b, 0, 0)),
        pl.BlockSpec((None, 1, D), lambda b, c: (b, 0, 0)),
        pl.BlockSpec((None, HEADS, HD, HD), lambda b, c: (b, 0, 0, 0)),
    ]
    out_shape = [
        jax.ShapeDtypeStruct((bsz, t_len, D), F32),
        jax.ShapeDtypeStruct((bsz, CONV_A_W - 1, D), F32),
        jax.ShapeDtypeStruct((bsz, CONV_B_W - 1, D), F32),
        jax.ShapeDtypeStruct((bsz, 1, D), F32),
        jax.ShapeDtypeStruct((bsz, HEADS, HD, HD), F32),
    ]
    scratch = [
        pltpu.VMEM((SUBLANES + chunk, D), F32),
        pltpu.VMEM((SUBLANES + chunk, D), F32),
        pltpu.VMEM((SUBLANES, D), F32),
        pltpu.VMEM((HEADS, HD, HD), F32),
        pltpu.VMEM((chunk, HD), F32),
        pltpu.VMEM((chunk, HD), F32),
        pltpu.VMEM((chunk, HD), F32),
        pltpu.VMEM((chunk, HD), F32),
        pltpu.VMEM((chunk, chunk), F32),
        pltpu.VMEM((chunk, D), F32),
    ]
    return pl.pallas_call(
        functools.partial(_mixer_body, chunk=chunk),
        grid=(bsz, nc),
        in_specs=in_specs, out_specs=out_specs, out_shape=out_shape, scratch_shapes=scratch,
        compiler_params=pltpu.CompilerParams(
            dimension_semantics=("arbitrary", "arbitrary"), vmem_limit_bytes=VMEM_LIMIT),
        name="mixer",
    )(x, z, past_a, past_b, h0, s0, lb, lw["conv_a_w"], lw["conv_b_w"], lw["conv_b_b"],
      lw["lru_wa"], lw["lru_ba"], lw["lru_wx"], lw["lru_bx"], lw["lru_lambda"], lw["hgrn_norm_g"],
      lw["w_out_a"], lw["w_out_b"], lw["w_out_c"], lw["w_o"], lw["ln1_g"], lw["ln1_b"])


def _sort_network(n):
    pairs = []

    def merge(lo, m, r):
        step = 2 * r
        if step < m:
            merge(lo, m, step)
            merge(lo + r, m, step)
            pairs.extend((i, i + r) for i in range(lo + r, lo + m - r, step))
        else:
            pairs.append((lo, lo + r))

    def sort(lo, m):
        if m > 1:
            sort(lo, m // 2)
            sort(lo + m // 2, m // 2)
            merge(lo, m, 1)

    sort(0, n)
    return pairs


_SORT16 = _sort_network(TOPK)


def _cmp_exchange(v, i, j):
    if v[j] is None:
        return
    if v[i] is None:
        v[i], v[j] = v[j], None
        return
    v[i], v[j] = jnp.maximum(v[i], v[j]), jnp.minimum(v[i], v[j])


def _sort_desc(v):
    v = list(v) + [None] * (TOPK - len(v))
    for i, j in _SORT16:
        _cmp_exchange(v, i, j)
    return v


def _bitonic_to_desc(v):
    v = list(v)
    d = TOPK // 2
    while d >= 1:
        for i in range(TOPK):
            if i & d == 0:
                _cmp_exchange(v, i, i + d)
        d //= 2
    return v


def _merge_rolled(v, shift):
    w = [None if x is None else pltpu.roll(x, shift, 0) for x in v]
    out = []
    for i in range(TOPK):
        a, b = v[i], w[TOPK - 1 - i]
        out.append(b if a is None else a if b is None else jnp.maximum(a, b))
    return out


def _top16_of_rows(vregs):
    v = _sort_desc(vregs)
    for shift in (4, 2, 1):
        v = _bitonic_to_desc(_merge_rolled(v, shift))
    return v


def _kth16_of_rows(vregs):
    v = _sort_desc(vregs)
    for shift in (4, 2):
        v = _bitonic_to_desc(_merge_rolled(v, shift))
    return functools.reduce(jnp.minimum, _merge_rolled(v, 1))


def _pack_rows(vs, sub):
    out = vs[0]
    for j in range(1, SUBLANES):
        out = jnp.where(sub == j, vs[j], out)
    return out


def _scores_body(x_ref, wq_ref, keys_ref, xt_ref, p1_ref, p2_ref, th_ref, qbuf, *, tn):
    xt_ref[...] = x_ref[...].T.astype(BF)
    qbuf[...] = jnp.dot(x_ref[...].astype(BF), wq_ref[...], preferred_element_type=F32).astype(BF)
    sub = lax.broadcasted_iota(jnp.int32, (SUBLANES, LANES), 0)
    n_tc = tn // LANES

    def step(idx, carry):
        h = idx // n_tc
        tc = idx % n_tc
        t0 = pl.multiple_of(tc * LANES, LANES)
        c0 = pl.multiple_of(h * (2 * PEER_DH), 2 * PEER_DH)
        s1 = _nt_dot(keys_ref[h, 0], qbuf[pl.ds(t0, LANES), pl.ds(c0, PEER_DH)])
        s2 = _nt_dot(keys_ref[h, 1], qbuf[pl.ds(t0, LANES), pl.ds(c0 + PEER_DH, PEER_DH)])
        split = lambda s: [s[r * SUBLANES:(r + 1) * SUBLANES, :] for r in range(N_KEYS // SUBLANES)]
        a = _top16_of_rows(split(s1))
        b = _top16_of_rows(split(s2))
        pa = [jnp.exp(x - a[0]) for x in a]
        pb = [jnp.exp(x - b[0]) for x in b]
        pb_lo, pb_hi = _pack_rows(pb[:SUBLANES], sub), _pack_rows(pb[SUBLANES:], sub)
        pa_hi = _pack_rows(pa[SUBLANES:], sub)
        lhs = [pa[0], pa[0]] + pa[1:SUBLANES] + [pa_hi]
        rhs = [pb_lo, pb_hi] + [pb_lo] * (SUBLANES - 1) + [pb[0]]
        t = [x * y for x, y in zip(lhs, rhs)]
        t16 = _kth16_of_rows(t)
        keep = [x >= t16 for x in t]
        z_sum = jnp.sum(sum(jnp.where(m, x, 0.0) for m, x in zip(keep, t)), axis=0, keepdims=True)
        rz = 1.0 / z_sum
        gates = [(x * rz) * y for x, y in zip(lhs, rhs)]
        th = jnp.min(functools.reduce(jnp.minimum, [jnp.where(m, g, jnp.inf) for m, g in zip(keep, gates)]),
                     axis=0, keepdims=True)
        p1_ref[h, :, pl.ds(t0, LANES)] = jnp.exp(s1 - a[0][0:1]) * rz
        p2_ref[h, tc] = jnp.exp(s2 - b[0][0:1])
        th_ref[h, :, pl.ds(t0, LANES)] = th
        return carry

    lax.fori_loop(0, PEER_HEADS * n_tc, step, 0)


def _peer_scores(x2d, wq_bf, keys_bf, tn):
    n = x2d.shape[0]
    n_tc = tn // LANES
    return pl.pallas_call(
        functools.partial(_scores_body, tn=tn),
        grid=(n // tn,),
        in_specs=[pl.BlockSpec((tn, D), lambda i: (i, 0)),
                  pl.BlockSpec((D, 2 * PEER_HEADS * PEER_DH), lambda i: (0, 0)),
                  pl.BlockSpec((PEER_HEADS, 2, N_KEYS, PEER_DH), lambda i: (0, 0, 0, 0))],
        out_specs=[pl.BlockSpec((D, tn), lambda i: (0, i)),
                   pl.BlockSpec((PEER_HEADS, N_KEYS, tn), lambda i: (0, 0, i)),
                   pl.BlockSpec((PEER_HEADS, n_tc, N_KEYS, LANES), lambda i: (0, i, 0, 0)),
                   pl.BlockSpec((PEER_HEADS, 1, tn), lambda i: (0, 0, i))],
        out_shape=[jax.ShapeDtypeStruct((D, n), BF),
                   jax.ShapeDtypeStruct((PEER_HEADS, N_KEYS, n), F32),
                   jax.ShapeDtypeStruct((PEER_HEADS, n // LANES, N_KEYS, LANES), F32),
                   jax.ShapeDtypeStruct((PEER_HEADS, 1, n), F32)],
        scratch_shapes=[pltpu.VMEM((tn, 2 * PEER_HEADS * PEER_DH), BF)],
        compiler_params=pltpu.CompilerParams(
            dimension_semantics=("arbitrary",), vmem_limit_bytes=VMEM_LIMIT),
        name="peer_scores",
    )(x2d, wq_bf, keys_bf)


def _dense_body(xt_ref, x_ref, u0_ref, un_ref, vp_ref, p1_ref, p2_ref, th_ref, g2_ref, b2_ref,
                y_ref, acc, h_even, h_odd, g_even, g_odd, *, tn, eb, nb):
    j = pl.program_id(1)
    kb = eb // N_KEYS
    n_tc = tn // LANES

    @pl.when(j == 0)
    def _():
        acc[...] = jnp.zeros((D, tn), F32)
        g_odd[:, 0:tn] = jnp.zeros((eb, tn), BF)
        h_even[:, 0:tn] = jnp.dot(u0_ref[...], xt_ref[...], preferred_element_type=F32)

    mxu_w = min(tn, 2 * LANES)
    n_mc = tn // mxu_w

    def h_piece(il, mc, h_next):
        rows = slice(il * N_KEYS, (il + 1) * N_KEYS)
        cols = slice(mc * mxu_w, (mc + 1) * mxu_w)
        h_next[rows, cols] = jnp.dot(un_ref[rows, :], xt_ref[:, cols], preferred_element_type=F32)

    def acc_piece(il, mc, g_prev):
        rows = slice(il * N_KEYS, (il + 1) * N_KEYS)
        cols = slice(mc * mxu_w, (mc + 1) * mxu_w)
        acc[rows, cols] += jnp.dot(vp_ref[rows, :], g_prev[:, cols], preferred_element_type=F32)

    def gate_piece(k0, tc, half, h_cur, g_cur):
        lanes = slice(tc * LANES, (tc + 1) * LANES)
        sub_rows = slice(half * GATE_ROWS, (half + 1) * GATE_ROWS)
        w = [jnp.zeros((GATE_ROWS, LANES), F32) for _ in range(GATE_KEYS)]
        for h in range(PEER_HEADS):
            p2 = p2_ref[h, tc, sub_rows, :]
            th = th_ref[h, :, lanes]
            for a in range(GATE_KEYS):
                prod = p1_ref[h, k0 + a:k0 + a + 1, lanes] * p2
                w[a] = w[a] + jnp.where(prod >= th, prod, 0.0)
        for a in range(GATE_KEYS):
            r0 = (k0 + a) * N_KEYS + half * GATE_ROWS
            g_cur[r0:r0 + GATE_ROWS, lanes] = (w[a] * _gelu(h_cur[r0:r0 + GATE_ROWS, lanes])).astype(BF)

    def block_step(h_cur, h_next, g_cur, g_prev):
        for k0 in range(0, kb, GATE_KEYS):
            mxu = [functools.partial(piece, il, mc, buf)
                   for il in range(k0, k0 + GATE_KEYS)
                   for piece, buf in ((h_piece, h_next), (acc_piece, g_prev))
                   for mc in range(n_mc)]
            vpu = [functools.partial(gate_piece, k0, tc, half, h_cur, g_cur)
                   for tc in range(n_tc) for half in range(N_KEYS // GATE_ROWS)]
            per_vpu = -(-len(mxu) // len(vpu))
            while mxu or vpu:
                for _ in range(per_vpu):
                    if mxu:
                        mxu.pop(0)()
                if vpu:
                    vpu.pop(0)()

    @pl.when((j < nb) & (j % 2 == 0))
    def _():
        block_step(h_even, h_odd, g_even, g_odd)

    @pl.when((j < nb) & (j % 2 == 1))
    def _():
        block_step(h_odd, h_even, g_odd, g_even)

    @pl.when(j == nb)
    def _():
        g_last = g_even if (nb - 1) % 2 == 0 else g_odd

        for il in range(kb):
            for mc in range(n_mc):
                acc_piece(il, mc, g_last)
        y_ref[...] = _layer_norm(ALPHA * x_ref[...] + acc[...].T, g2_ref[...], b2_ref[...])


def _peer_dense(xt_bf, x2d, u_bf, vt_bf, p1, p2, th, g2, b2, tn):
    n = x2d.shape[0]
    eb = D
    nb = N_EXPERTS // eb
    n_tc = tn // LANES
    assert eb // N_KEYS == SUBLANES and (eb // N_KEYS) % GATE_KEYS == 0
    return pl.pallas_call(
        functools.partial(_dense_body, tn=tn, eb=eb, nb=nb),
        grid=(n // tn, nb + 1),
        in_specs=[pl.BlockSpec((D, tn), lambda i, j: (0, i)),
                  pl.BlockSpec((tn, D), lambda i, j: (i, 0)),
                  pl.BlockSpec((eb, D), lambda i, j: (0, 0)),
                  pl.BlockSpec((eb, D), lambda i, j: (jnp.minimum(j + 1, nb - 1), 0)),
                  pl.BlockSpec((D, eb), lambda i, j: (0, jnp.maximum(j - 1, 0))),
                  pl.BlockSpec((PEER_HEADS, eb // N_KEYS, tn),
                               lambda i, j: (0, jnp.minimum(j, nb - 1), i)),
                  pl.BlockSpec((PEER_HEADS, n_tc, N_KEYS, LANES), lambda i, j: (0, i, 0, 0)),
                  pl.BlockSpec((PEER_HEADS, 1, tn), lambda i, j: (0, 0, i)),
                  pl.BlockSpec((1, D), lambda i, j: (0, 0)),
                  pl.BlockSpec((1, D), lambda i, j: (0, 0))],
        out_specs=pl.BlockSpec((tn, D), lambda i, j: (i, 0)),
        out_shape=jax.ShapeDtypeStruct((n, D), F32),
        scratch_shapes=[pltpu.VMEM((D, tn), F32),
                        pltpu.VMEM((eb, tn + LANES), F32), pltpu.VMEM((eb, tn + LANES), F32),
                        pltpu.VMEM((eb, tn + LANES), BF), pltpu.VMEM((eb, tn + LANES), BF)],
        compiler_params=pltpu.CompilerParams(
            dimension_semantics=("arbitrary", "arbitrary"), vmem_limit_bytes=VMEM_LIMIT),
        name="peer_dense",
    )(xt_bf, x2d, u_bf, u_bf, vt_bf, p1, p2, th, g2, b2)


def _trunk_layer(x, past_a, past_b, h0, s0, lb, lw):
    bsz, t_len, _ = x.shape
    n = bsz * t_len
    z = _in_proj(x.reshape(n, D), lw["w_in"], lw["b_in"]).reshape(bsz, t_len, IN_COLS)
    x1, new_a, new_b, h_t, s_t = _mixer(x, z, past_a, past_b, h0, s0, lb, lw)
    x1 = x1.reshape(n, D)
    tn = min(n, 512)
    xt, p1, p2, th = _peer_scores(x1, lw["peer_wq"], lw["peer_keys"], tn)
    x2 = _peer_dense(xt, x1, lw["peer_u"], lw["peer_vt"], p1, p2, th, lw["ln2_g"], lw["ln2_b"], tn)
    return x2.reshape(bsz, t_len, D), new_a, new_b, h_t.reshape(bsz, D), s_t


def kernel(x_prompt, x_sample, state_conv_a, state_conv_b, state_lru, state_hgrn, w_in, b_in, conv_a_w,
           conv_b_w, conv_b_b, lru_wa, lru_ba, lru_wx, lru_bx, lru_lambda, hgrn_lb_logits, hgrn_norm_g,
           w_out_a, w_out_b, w_out_c, w_o, ln1_g, ln1_b, peer_wq, peer_keys, peer_u, peer_v, ln2_g,
           ln2_b):
    p = jax.nn.softmax(hgrn_lb_logits.astype(F32), axis=0)
    lbs = jnp.cumsum(p, axis=0) - p[0]
    bp = x_prompt.shape[0]
    xp, xs = x_prompt, x_sample
    outs_p, outs_s = [], []
    for l in range(DEPTH):
        row = lambda a: a[l].reshape(1, -1)
        lw = dict(
            w_in=w_in[l].astype(BF), b_in=row(b_in),
            conv_a_w=conv_a_w[l], conv_b_w=conv_b_w[l], conv_b_b=row(conv_b_b),
            lru_wa=lru_wa[l].astype(BF), lru_ba=row(lru_ba), lru_wx=lru_wx[l].astype(BF),
            lru_bx=row(lru_bx), lru_lambda=row(lru_lambda), hgrn_norm_g=row(hgrn_norm_g),
            w_out_a=w_out_a[l].astype(BF), w_out_b=w_out_b[l].astype(BF),
            w_out_c=w_out_c[l].astype(BF), w_o=w_o[l].astype(BF),
            ln1_g=row(ln1_g), ln1_b=row(ln1_b),
            peer_wq=peer_wq[l].astype(BF), peer_keys=peer_keys[l].astype(BF),
            peer_u=peer_u[l].astype(BF), peer_vt=peer_v[l].T.astype(BF),
            ln2_g=row(ln2_g), ln2_b=row(ln2_b))
        lb = lbs[l].reshape(1, D)
        xp, pa, pb, ph, ps = _trunk_layer(
            xp, jnp.zeros((bp, CONV_A_W - 1, D), F32), jnp.zeros((bp, CONV_B_W - 1, D), F32),
            jnp.zeros((bp, 1, D), F32), jnp.zeros((bp, HEADS, HD, HD), F32), lb, lw)
        xs, sa, sb, sh, ss = _trunk_layer(
            xs, state_conv_a[l], state_conv_b[l], state_lru[l][:, None, :], state_hgrn[l], lb, lw)
        outs_p.append((pa, pb, ph, ps))
        outs_s.append((sa, sb, sh, ss))
    stack = lambda outs, k: jnp.stack([o[k] for o in outs])
    return (xp, xs,
            stack(outs_p, 0), stack(outs_p, 1), stack(outs_p, 2), stack(outs_p, 3),
            stack(outs_s, 0), stack(outs_s, 1), stack(outs_s, 2), stack(outs_s, 3))
```

```python
import functools

import jax
import jax.numpy as jnp
from jax import lax
from jax.experimental import pallas as pl
from jax.experimental.pallas import tpu as pltpu

D = 1024
DEPTH = 2
CONV_A_W = 3
CONV_B_W = 4
LRU_BLOCKS = 8
LRU_BW = D // LRU_BLOCKS
LRU_C = 8.0
HEADS = 8
HD = D // HEADS
PEER_HEADS = 8
PEER_DH = 128
N_KEYS = 128
N_EXPERTS = N_KEYS * N_KEYS
TOPK = 16
ALPHA = (2.0 * DEPTH) ** 0.25
LN_EPS = 1e-5
RMS_EPS = 1e-6
IN_COLS = 12 * D
SUB = 16
INTRA_CAP = 60.0
LANES = 128
SUBLANES = 8
VMEM_LIMIT = 56 * 1024 * 1024
GATE_KEYS = 4
GATE_ROWS = 64
MXU_ROWS = 512

OFF_AB, OFF_AC, OFF_AX, OFF_BX, OFF_BG, OFF_CQ, OFF_CF, OFF_CI, OFF_CG, OFF_GA, OFF_GB, OFF_GC = (
    i * D for i in range(12))

BF = jnp.bfloat16
F32 = jnp.float32


def _sigmoid(x):
    return 1.0 / (1.0 + jnp.exp(-x))


def _gelu(x):
    return 0.5 * x * (1.0 + lax.erf(x * (2.0 ** -0.5)))


def _layer_norm(x, g, b):
    mu = jnp.mean(x, axis=-1, keepdims=True)
    xc = x - mu
    var = jnp.mean(xc * xc, axis=-1, keepdims=True)
    return xc * lax.rsqrt(var + LN_EPS) * g + b


def _nt_dot(a, b):
    return lax.dot_general(a, b, (((1,), (1,)), ((), ())), preferred_element_type=F32)


def _tn_dot(a, b):
    return lax.dot_general(a, b, (((0,), (0,)), ((), ())), preferred_element_type=F32)


def _shift_rows(x, d, fill, rows):
    return jnp.where(rows >= d, pltpu.roll(x, d, 0), fill)


def _inproj_body(x_ref, w_ref, b_ref, o_ref):
    x = x_ref[...].astype(BF)
    o_ref[...] = jnp.dot(x, w_ref[...], preferred_element_type=F32) + b_ref[...]


def _in_proj(x2d, w_bf, b_row):
    n = x2d.shape[0]
    tm = min(n, 1024)
    tn = 2048
    return pl.pallas_call(
        _inproj_body,
        grid=(n // tm, IN_COLS // tn),
        in_specs=[pl.BlockSpec((tm, D), lambda i, j: (i, 0)),
                  pl.BlockSpec((D, tn), lambda i, j: (0, j)),
                  pl.BlockSpec((1, tn), lambda i, j: (0, j))],
        out_specs=pl.BlockSpec((tm, tn), lambda i, j: (i, j)),
        out_shape=jax.ShapeDtypeStruct((n, IN_COLS), F32),
        compiler_params=pltpu.CompilerParams(
            dimension_semantics=("arbitrary", "arbitrary"), vmem_limit_bytes=VMEM_LIMIT),
        name="in_proj",
    )(x2d, w_bf, b_row)


def _mixer_body(x_ref, z_ref, pa_ref, pb_ref, h0_ref, s0_ref, lb_ref, cwa_ref, cwb_ref, cbb_ref,
                wa_ref, ba_ref, wx_ref, bx_ref, lam_ref, ng_ref, woa_ref, wob_ref, woc_ref, wo_ref,
                g1_ref, b1_ref,
                x1_ref, na_ref, nb_ref, nh_ref, ns_ref,
                ua, ub, hc, st, qs, ks, vs, bs, att, oc, kf, bf, qsil, qe, kd, att_all, *, chunk):
    c = pl.program_id(1)
    last = pl.num_programs(1) - 1
    nsub = chunk // SUB
    rows = lax.broadcasted_iota(jnp.int32, (chunk, D), 0)

    @pl.when(c == 0)
    def _():
        ua[SUBLANES - 2:SUBLANES, :] = pa_ref[...]
        ub[SUBLANES - 3:SUBLANES, :] = pb_ref[...]
        hc[0:1, :] = h0_ref[...]
        for h in range(HEADS):
            st[h] = s0_ref[h].T
        att_all[...] = jnp.zeros((HEADS, chunk, chunk), F32)

    u = z_ref[:, OFF_AC:OFF_AC + D] * z_ref[:, OFF_AX:OFF_AX + D]
    ua[SUBLANES:SUBLANES + chunk, :] = u
    conv_a = (cwa_ref[0:1, :] * ua[SUBLANES - 2:SUBLANES - 2 + chunk, :]
              + cwa_ref[1:2, :] * ua[SUBLANES - 1:SUBLANES - 1 + chunk, :]
              + cwa_ref[2:3, :] * u)
    tail_a = ua[SUBLANES + chunk - 2:SUBLANES + chunk, :]
    na_ref[...] = tail_a
    ua[SUBLANES - 2:SUBLANES, :] = tail_a
    y_a = jnp.dot((z_ref[:, OFF_AB:OFF_AB + D] * conv_a).astype(BF), woa_ref[...],
                  preferred_element_type=F32)
    merged = _sigmoid(z_ref[:, OFF_GA:OFF_GA + D]) * y_a

    xb = z_ref[:, OFF_BX:OFF_BX + D]
    ub[SUBLANES:SUBLANES + chunk, :] = xb
    xl = (cwb_ref[0:1, :] * ub[SUBLANES - 3:SUBLANES - 3 + chunk, :]
          + cwb_ref[1:2, :] * ub[SUBLANES - 2:SUBLANES - 2 + chunk, :]
          + cwb_ref[2:3, :] * ub[SUBLANES - 1:SUBLANES - 1 + chunk, :]
          + cwb_ref[3:4, :] * xb) + cbb_ref[...]
    tail_b = ub[SUBLANES + chunk - 3:SUBLANES + chunk, :]
    nb_ref[...] = tail_b
    ub[SUBLANES - 3:SUBLANES, :] = tail_b

    xl_bf = xl.astype(BF)
    ra = jnp.concatenate(
        [jnp.dot(xl_bf[:, n * LRU_BW:(n + 1) * LRU_BW], wa_ref[n], preferred_element_type=F32)
         for n in range(LRU_BLOCKS)], axis=1)
    rx = jnp.concatenate(
        [jnp.dot(xl_bf[:, n * LRU_BW:(n + 1) * LRU_BW], wx_ref[n], preferred_element_type=F32)
         for n in range(LRU_BLOCKS)], axis=1)
    r = _sigmoid(ra + ba_ref[...])
    gi = _sigmoid(rx + bx_ref[...])
    nlam = -lam_ref[...]
    softplus = jnp.maximum(nlam, 0.0) + jnp.log1p(jnp.exp(-jnp.abs(nlam)))
    log_a = (-LRU_C) * r * softplus
    a = jnp.exp(log_a)
    uu = jnp.sqrt(jnp.tanh(-log_a) * (a * a + 1.0)) * (gi * xl)
    d = 1
    while d < chunk:
        uu = a * _shift_rows(uu, d, 0.0, rows) + uu
        a = a * _shift_rows(a, d, 1.0, rows)
        d *= 2
    hh = uu + a * hc[0:1, :]
    h_last = hh[chunk - 1:chunk, :]
    hc[0:1, :] = h_last
    nh_ref[...] = h_last
    y_b = jnp.dot((_gelu(z_ref[:, OFF_BG:OFF_BG + D]) * hh).astype(BF), wob_ref[...],
                  preferred_element_type=F32)
    merged = merged + _sigmoid(z_ref[:, OFF_GB:OFF_GB + D]) * y_b

    cols_a = lax.broadcasted_iota(jnp.int32, (SUB, chunk), 1)
    sub_rows = lax.broadcasted_iota(jnp.int32, (SUB, 1), 0)
    f_all = lb_ref[...] + (1.0 - lb_ref[...]) * _sigmoid(z_ref[:, OFF_CF:OFF_CF + D])
    kf[...] = 1.0 - f_all
    b_all = jnp.log(f_all)
    d = 1
    while d < chunk:
        b_all = b_all + _shift_rows(b_all, d, 0.0, rows)
        d *= 2
    bf[...] = b_all
    intra_bound = jnp.max(functools.reduce(jnp.maximum, [
        (bf[i * SUB - 1:i * SUB, :] if i else 0.0) - bf[(i + 1) * SUB - 1:(i + 1) * SUB, :]
        for i in range(nsub)]))

    def head_step(h, carry):
        off = pl.multiple_of(h * HD, HD)
        k = kf[:, pl.ds(off, HD)]
        b = bf[:, pl.ds(off, HD)]
        cq = z_ref[:, pl.ds(OFF_CQ + off, HD)]
        q = cq * _sigmoid(cq)
        v = z_ref[:, pl.ds(OFF_CI + off, HD)]
        qs[...] = q
        ks[...] = k
        vs[...] = v
        bs[...] = b
        st_h = st[h]
        b_last = b[chunk - 1:chunk, :]
        o = _nt_dot((q * jnp.exp(b)).astype(BF), st_h.astype(BF))
        kdec = k * jnp.exp(b_last - b)
        st[h] = st_h * jnp.exp(b_last) + _tn_dot(v.astype(BF), kdec.astype(BF))

        def sub_start(i):
            return bs[i * SUB - 1:i * SUB, :] if i else jnp.zeros((1, HD), F32)

        def intra_direct():
            att[0:SUB, :] = jnp.zeros((SUB, chunk), F32)
            for i in range(1, nsub):
                lo, hi = i * SUB, (i + 1) * SUB
                q_i = qs[lo:hi, :] * jnp.exp(bs[lo:hi, :] - sub_start(i))
                k_i = k * jnp.exp(jnp.minimum(sub_start(i) - b, 0.0))
                a_i = _nt_dot(q_i.astype(BF), k_i.astype(BF))
                att[lo:hi, :] = jnp.where(cols_a < lo, a_i, 0.0)
            o_off = jnp.dot(att[...].astype(BF), v.astype(BF), preferred_element_type=F32)
            o_parts = []
            for i in range(nsub):
                q_i = qs[i * SUB:(i + 1) * SUB, :]
                b_i = bs[i * SUB:(i + 1) * SUB, :]
                o_i = jnp.zeros((SUB, HD), F32)
                for s in range(SUB):
                    r0 = i * SUB + s
                    p = q_i * ks[r0:r0 + 1, :] * jnp.exp(jnp.minimum(b_i - bs[r0:r0 + 1, :], 0.0))
                    a_col = jnp.sum(p, axis=-1, keepdims=True)
                    a_col = jnp.where(sub_rows >= s, a_col, 0.0)
                    o_i = o_i + a_col * vs[r0:r0 + 1, :]
                o_parts.append(o_i)
            return o_off + jnp.concatenate(o_parts, axis=0)

        o = o + intra_direct()
        oc[:, pl.ds(off, HD)] = o * lax.rsqrt(jnp.mean(o * o, axis=-1, keepdims=True) + RMS_EPS) * ng_ref[...]
        return carry

    def hgrn_factored():
        cq = z_ref[:, OFF_CQ:OFF_CQ + D]
        qsil[...] = cq * _sigmoid(cq)
        b_last = bf[chunk - 1:chunk, :]
        qe[...] = (qsil[...] * jnp.exp(bf[...])).astype(BF)
        kd[...] = (kf[...] * jnp.exp(b_last - bf[...])).astype(BF)
        for i in range(nsub):
            lo, hi = i * SUB, (i + 1) * SUB
            start = bf[lo - 1:lo, :] if i else 0.0
            q_i = (qsil[lo:hi, :] * jnp.exp(bf[lo:hi, :] - start)).astype(BF)
            k_i = (kf[0:hi, :] * jnp.exp(jnp.minimum(start - bf[0:hi, :], INTRA_CAP))).astype(BF)
            causal = (lax.broadcasted_iota(jnp.int32, (SUB, hi), 1)
                      <= lax.broadcasted_iota(jnp.int32, (SUB, hi), 0) + lo)
            for h in range(HEADS):
                hs = slice(h * HD, (h + 1) * HD)
                att_all[h, lo:hi, 0:hi] = jnp.where(causal, _nt_dot(q_i[:, hs], k_i[:, hs]), 0.0)
        for h in range(HEADS):
            hs = slice(h * HD, (h + 1) * HD)
            v_h = z_ref[:, OFF_CI + h * HD:OFF_CI + (h + 1) * HD].astype(BF)
            st_h = st[h]
            o = (_nt_dot(qe[:, hs], st_h.astype(BF))
                 + jnp.dot(att_all[h].astype(BF), v_h, preferred_element_type=F32))
            st[h] = st_h * jnp.exp(b_last[:, hs]) + _tn_dot(v_h, kd[:, hs])
            oc[:, hs] = o * lax.rsqrt(jnp.mean(o * o, axis=-1, keepdims=True) + RMS_EPS) * ng_ref[...]

    def hgrn_direct():
        lax.fori_loop(0, HEADS, head_step, 0)

    lax.cond(intra_bound <= INTRA_CAP, hgrn_factored, hgrn_direct)

    @pl.when(c == last)
    def _():
        for h in range(HEADS):
            ns_ref[h] = st[h].T

    cg = z_ref[:, OFF_CG:OFF_CG + D]
    y_c = jnp.dot((oc[...] * (cg * _sigmoid(cg))).astype(BF), woc_ref[...], preferred_element_type=F32)
    merged = merged + _sigmoid(z_ref[:, OFF_GC:OFF_GC + D]) * y_c
    res = ALPHA * x_ref[...] + jnp.dot(merged.astype(BF), wo_ref[...], preferred_element_type=F32)
    x1_ref[...] = _layer_norm(res, g1_ref[...], b1_ref[...])


def _mixer(x, z, past_a, past_b, h0, s0, lb, lw):
    bsz, t_len, _ = x.shape
    chunk = min(t_len, 128)
    assert t_len % chunk == 0 and chunk % SUB == 0
    nc = t_len // chunk
    full2 = lambda shape: pl.BlockSpec(shape, lambda b, c: (0, 0))
    full3 = lambda shape: pl.BlockSpec(shape, lambda b, c: (0, 0, 0))
    in_specs = [
        pl.BlockSpec((None, chunk, D), lambda b, c: (b, c, 0)),
        pl.BlockSpec((None, chunk, IN_COLS), lambda b, c: (b, c, 0)),
        pl.BlockSpec((None, CONV_A_W - 1, D), lambda b, c: (b, 0, 0)),
        pl.BlockSpec((None, CONV_B_W - 1, D), lambda b, c: (b, 0, 0)),
        pl.BlockSpec((None, 1, D), lambda b, c: (b, 0, 0)),
        pl.BlockSpec((None, HEADS, HD, HD), lambda b, c: (b, 0, 0, 0)),
        full2((1, D)),
        full2((CONV_A_W, D)), full2((CONV_B_W, D)), full2((1, D)),
        full3((LRU_BLOCKS, LRU_BW, LRU_BW)), full2((1, D)),
        full3((LRU_BLOCKS, LRU_BW, LRU_BW)), full2((1, D)),
        full2((1, D)), full2((1, HD)),
        full2((D, D)), full2((D, D)), full2((D, D)), full2((D, D)),
        full2((1, D)), full2((1, D)),
    ]
    out_specs = [
        pl.BlockSpec((None, chunk, D), lambda b, c: (b, c, 0)),
        pl.BlockSpec((None, CONV_A_W - 1, D), lambda b, c: (b, 0, 0)),
        pl.BlockSpec((None, CONV_B_W - 1, D), lambda b, c: (b, 0, 0)),
        pl.BlockSpec((None, 1, D), lambda b, c: (b, 0, 0)),
        pl.BlockSpec((None, HEADS, HD, HD), lambda b, c: (b, 0, 0, 0)),
    ]
    out_shape = [
        jax.ShapeDtypeStruct((bsz, t_len, D), F32),
        jax.ShapeDtypeStruct((bsz, CONV_A_W - 1, D), F32),
        jax.ShapeDtypeStruct((bsz, CONV_B_W - 1, D), F32),
        jax.ShapeDtypeStruct((bsz, 1, D), F32),
        jax.ShapeDtypeStruct((bsz, HEADS, HD, HD), F32),
    ]
    scratch = [
        pltpu.VMEM((SUBLANES + chunk, D), F32),
        pltpu.VMEM((SUBLANES + chunk, D), F32),
        pltpu.VMEM((SUBLANES, D), F32),
        pltpu.VMEM((HEADS, HD, HD), F32),
        pltpu.VMEM((chunk, HD), F32),
        pltpu.VMEM((chunk, HD), F32),
        pltpu.VMEM((chunk, HD), F32),
        pltpu.VMEM((chunk, HD), F32),
        pltpu.VMEM((chunk, chunk), F32),
        pltpu.VMEM((chunk, D), F32),
        pltpu.VMEM((chunk, D), F32),
        pltpu.VMEM((chunk, D), F32),
        pltpu.VMEM((chunk, D), F32),
        pltpu.VMEM((chunk, D), BF),
        pltpu.VMEM((chunk, D), BF),
        pltpu.VMEM((HEADS, chunk, chunk), F32),
    ]
    return pl.pallas_call(
        functools.partial(_mixer_body, chunk=chunk),
        grid=(bsz, nc),
        in_specs=in_specs, out_specs=out_specs, out_shape=out_shape, scratch_shapes=scratch,
        compiler_params=pltpu.CompilerParams(
            dimension_semantics=("arbitrary", "arbitrary"), vmem_limit_bytes=VMEM_LIMIT),
        name="mixer",
    )(x, z, past_a, past_b, h0, s0, lb, lw["conv_a_w"], lw["conv_b_w"], lw["conv_b_b"],
      lw["lru_wa"], lw["lru_ba"], lw["lru_wx"], lw["lru_bx"], lw["lru_lambda"], lw["hgrn_norm_g"],
      lw["w_out_a"], lw["w_out_b"], lw["w_out_c"], lw["w_o"], lw["ln1_g"], lw["ln1_b"])


def _sort_network(n):
    pairs = []

    def merge(lo, m, r):
        step = 2 * r
        if step < m:
            merge(lo, m, step)
            merge(lo + r, m, step)
            pairs.extend((i, i + r) for i in range(lo + r, lo + m - r, step))
        else:
            pairs.append((lo, lo + r))

    def sort(lo, m):
        if m > 1:
            sort(lo, m // 2)
            sort(lo + m // 2, m // 2)
            merge(lo, m, 1)

    sort(0, n)
    return pairs


_SORT16 = _sort_network(TOPK)


def _cmp_exchange(v, i, j):
    if v[j] is None:
        return
    if v[i] is None:
        v[i], v[j] = v[j], None
        return
    v[i], v[j] = jnp.maximum(v[i], v[j]), jnp.minimum(v[i], v[j])


def _sort_desc(v):
    v = list(v) + [None] * (TOPK - len(v))
    for i, j in _SORT16:
        _cmp_exchange(v, i, j)
    return v


def _bitonic_to_desc(v):
    v = list(v)
    d = TOPK // 2
    while d >= 1:
        for i in range(TOPK):
            if i & d == 0:
                _cmp_exchange(v, i, i + d)
        d //= 2
    return v


def _merge_rolled(v, shift):
    w = [None if x is None else pltpu.roll(x, shift, 0) for x in v]
    out = []
    for i in range(TOPK):
        a, b = v[i], w[TOPK - 1 - i]
        out.append(b if a is None else a if b is None else jnp.maximum(a, b))
    return out


def _top16_of_rows(vregs):
    v = _sort_desc(vregs)
    for shift in (4, 2, 1):
        v = _bitonic_to_desc(_merge_rolled(v, shift))
    return v


def _kth16_of_rows(vregs):
    v = _sort_desc(vregs)
    for shift in (4, 2):
        v = _bitonic_to_desc(_merge_rolled(v, shift))
    return functools.reduce(jnp.minimum, _merge_rolled(v, 1))


def _pack_rows(vs, sub):
    out = vs[0]
    for j in range(1, SUBLANES):
        out = jnp.where(sub == j, vs[j], out)
    return out


def _scores_body(x_ref, wq_ref, keys_ref, xt_ref, p1_ref, p2_ref, th_ref, qbuf, *, tn):
    xt_ref[...] = x_ref[...].T.astype(BF)
    qbuf[...] = jnp.dot(x_ref[...].astype(BF), wq_ref[...], preferred_element_type=F32).astype(BF)
    sub = lax.broadcasted_iota(jnp.int32, (SUBLANES, LANES), 0)
    n_tc = tn // LANES

    def step(idx, carry):
        h = idx // n_tc
        tc = idx % n_tc
        t0 = pl.multiple_of(tc * LANES, LANES)
        c0 = pl.multiple_of(h * (2 * PEER_DH), 2 * PEER_DH)
        s1 = _nt_dot(keys_ref[h, 0], qbuf[pl.ds(t0, LANES), pl.ds(c0, PEER_DH)])
        s2 = _nt_dot(keys_ref[h, 1], qbuf[pl.ds(t0, LANES), pl.ds(c0 + PEER_DH, PEER_DH)])
        split = lambda s: [s[r * SUBLANES:(r + 1) * SUBLANES, :] for r in range(N_KEYS // SUBLANES)]
        a = _top16_of_rows(split(s1))
        b = _top16_of_rows(split(s2))
        pa = [jnp.exp(x - a[0]) for x in a]
        pb = [jnp.exp(x - b[0]) for x in b]
        pb_lo, pb_hi = _pack_rows(pb[:SUBLANES], sub), _pack_rows(pb[SUBLANES:], sub)
        pa_hi = _pack_rows(pa[SUBLANES:], sub)
        lhs = [pa[0], pa[0]] + pa[1:SUBLANES] + [pa_hi]
        rhs = [pb_lo, pb_hi] + [pb_lo] * (SUBLANES - 1) + [pb[0]]
        t = [x * y for x, y in zip(lhs, rhs)]
        t16 = _kth16_of_rows(t)
        keep = [x >= t16 for x in t]
        z_sum = jnp.sum(sum(jnp.where(m, x, 0.0) for m, x in zip(keep, t)), axis=0, keepdims=True)
        rz = 1.0 / z_sum
        gates = [(x * rz) * y for x, y in zip(lhs, rhs)]
        th = jnp.min(functools.reduce(jnp.minimum, [jnp.where(m, g, jnp.inf) for m, g in zip(keep, gates)]),
                     axis=0, keepdims=True)
        p1_ref[h, :, pl.ds(t0, LANES)] = jnp.exp(s1 - a[0][0:1]) * rz
        p2_ref[h, tc] = jnp.exp(s2 - b[0][0:1])
        th_ref[h, :, pl.ds(t0, LANES)] = th
        return carry

    lax.fori_loop(0, PEER_HEADS * n_tc, step, 0)


def _peer_scores(x2d, wq_bf, keys_bf, tn):
    n = x2d.shape[0]
    n_tc = tn // LANES
    return pl.pallas_call(
        functools.partial(_scores_body, tn=tn),
        grid=(n // tn,),
        in_specs=[pl.BlockSpec((tn, D), lambda i: (i, 0)),
                  pl.BlockSpec((D, 2 * PEER_HEADS * PEER_DH), lambda i: (0, 0)),
                  pl.BlockSpec((PEER_HEADS, 2, N_KEYS, PEER_DH), lambda i: (0, 0, 0, 0))],
        out_specs=[pl.BlockSpec((D, tn), lambda i: (0, i)),
                   pl.BlockSpec((PEER_HEADS, N_KEYS, tn), lambda i: (0, 0, i)),
                   pl.BlockSpec((PEER_HEADS, n_tc, N_KEYS, LANES), lambda i: (0, i, 0, 0)),
                   pl.BlockSpec((PEER_HEADS, 1, tn), lambda i: (0, 0, i))],
        out_shape=[jax.ShapeDtypeStruct((D, n), BF),
                   jax.ShapeDtypeStruct((PEER_HEADS, N_KEYS, n), F32),
                   jax.ShapeDtypeStruct((PEER_HEADS, n // LANES, N_KEYS, LANES), F32),
                   jax.ShapeDtypeStruct((PEER_HEADS, 1, n), F32)],
        scratch_shapes=[pltpu.VMEM((tn, 2 * PEER_HEADS * PEER_DH), BF)],
        compiler_params=pltpu.CompilerParams(
            dimension_semantics=("arbitrary",), vmem_limit_bytes=VMEM_LIMIT),
        name="peer_scores",
    )(x2d, wq_bf, keys_bf)


def _dense_body(xt_ref, x_ref, u0_ref, un_ref, vp_ref, p1_ref, p2_ref, th_ref, g2_ref, b2_ref,
                y_ref, acc, h_even, h_odd, g_even, g_odd, *, tn, eb, nb):
    j = pl.program_id(1)
    kb = eb // N_KEYS
    n_tc = tn // LANES

    @pl.when(j == 0)
    def _():
        acc[...] = jnp.zeros((D, tn), F32)
        g_odd[:, 0:tn] = jnp.zeros((eb, tn), BF)
        h_even[:, 0:tn] = jnp.dot(u0_ref[...], xt_ref[...], preferred_element_type=F32)

    mxu_w = min(tn, 2 * LANES)
    n_mc = tn // mxu_w

    mxu_rows = MXU_ROWS

    def h_piece(ig, mc, h_next):
        rows = slice(ig * mxu_rows, (ig + 1) * mxu_rows)
        cols = slice(mc * mxu_w, (mc + 1) * mxu_w)
        h_next[rows, cols] = jnp.dot(un_ref[rows, :], xt_ref[:, cols], preferred_element_type=F32)

    def acc_piece(ig, mc, g_prev):
        rows = slice(ig * mxu_rows, (ig + 1) * mxu_rows)
        cols = slice(mc * mxu_w, (mc + 1) * mxu_w)
        acc[rows, cols] += jnp.dot(vp_ref[rows, :], g_prev[:, cols], preferred_element_type=F32)

    def gate_piece(k0, tc, half, h_cur, g_cur):
        lanes = slice(tc * LANES, (tc + 1) * LANES)
        sub_rows = slice(half * GATE_ROWS, (half + 1) * GATE_ROWS)
        w = [jnp.zeros((GATE_ROWS, LANES), F32) for _ in range(GATE_KEYS)]
        for h in range(PEER_HEADS):
            p2 = p2_ref[h, tc, sub_rows, :]
            th = th_ref[h, :, lanes]
            for a in range(GATE_KEYS):
                prod = p1_ref[h, k0 + a:k0 + a + 1, lanes] * p2
                w[a] = w[a] + jnp.where(prod >= th, prod, 0.0)
        for a in range(GATE_KEYS):
            r0 = (k0 + a) * N_KEYS + half * GATE_ROWS
            g_cur[r0:r0 + GATE_ROWS, lanes] = (w[a] * _gelu(h_cur[r0:r0 + GATE_ROWS, lanes])).astype(BF)

    def block_step(h_cur, h_next, g_cur, g_prev):
        mxu = [functools.partial(piece, ig, mc, buf)
               for ig in range(eb // mxu_rows)
               for piece, buf in ((h_piece, h_next), (acc_piece, g_prev))
               for mc in range(n_mc)]
        vpu = [functools.partial(gate_piece, k0, tc, half, h_cur, g_cur)
               for k0 in range(0, kb, GATE_KEYS)
               for tc in range(n_tc) for half in range(N_KEYS // GATE_ROWS)]
        per_mxu = -(-len(vpu) // len(mxu))
        while mxu or vpu:
            if mxu:
                mxu.pop(0)()
            for _ in range(per_mxu):
                if vpu:
                    vpu.pop(0)()

    @pl.when((j < nb) & (j % 2 == 0))
    def _():
        block_step(h_even, h_odd, g_even, g_odd)

    @pl.when((j < nb) & (j % 2 == 1))
    def _():
        block_step(h_odd, h_even, g_odd, g_even)

    @pl.when(j == nb)
    def _():
        g_last = g_even if (nb - 1) % 2 == 0 else g_odd

        for ig in range(D // mxu_rows):
            for mc in range(n_mc):
                acc_piece(ig, mc, g_last)
        y_ref[...] = _layer_norm(ALPHA * x_ref[...] + acc[...].T, g2_ref[...], b2_ref[...])


def _peer_dense(xt_bf, x2d, u_bf, vt_bf, p1, p2, th, g2, b2, tn):
    n = x2d.shape[0]
    eb = D
    nb = N_EXPERTS // eb
    n_tc = tn // LANES
    assert eb // N_KEYS == SUBLANES and (eb // N_KEYS) % GATE_KEYS == 0
    return pl.pallas_call(
        functools.partial(_dense_body, tn=tn, eb=eb, nb=nb),
        grid=(n // tn, nb + 1),
        in_specs=[pl.BlockSpec((D, tn), lambda i, j: (0, i)),
                  pl.BlockSpec((tn, D), lambda i, j: (i, 0)),
                  pl.BlockSpec((eb, D), lambda i, j: (0, 0)),
                  pl.BlockSpec((eb, D), lambda i, j: (jnp.minimum(j + 1, nb - 1), 0)),
                  pl.BlockSpec((D, eb), lambda i, j: (0, jnp.maximum(j - 1, 0))),
                  pl.BlockSpec((PEER_HEADS, eb // N_KEYS, tn),
                               lambda i, j: (0, jnp.minimum(j, nb - 1), i)),
                  pl.BlockSpec((PEER_HEADS, n_tc, N_KEYS, LANES), lambda i, j: (0, i, 0, 0)),
                  pl.BlockSpec((PEER_HEADS, 1, tn), lambda i, j: (0, 0, i)),
                  pl.BlockSpec((1, D), lambda i, j: (0, 0)),
                  pl.BlockSpec((1, D), lambda i, j: (0, 0))],
        out_specs=pl.BlockSpec((tn, D), lambda i, j: (i, 0)),
        out_shape=jax.ShapeDtypeStruct((n, D), F32),
        scratch_shapes=[pltpu.VMEM((D, tn), F32),
                        pltpu.VMEM((eb, tn + LANES), F32), pltpu.VMEM((eb, tn + LANES), F32),
                        pltpu.VMEM((eb, tn + LANES), BF), pltpu.VMEM((eb, tn + LANES), BF)],
        compiler_params=pltpu.CompilerParams(
            dimension_semantics=("arbitrary", "arbitrary"), vmem_limit_bytes=VMEM_LIMIT),
        name="peer_dense",
    )(xt_bf, x2d, u_bf, u_bf, vt_bf, p1, p2, th, g2, b2)


def _trunk_layer(x, past_a, past_b, h0, s0, lb, lw):
    bsz, t_len, _ = x.shape
    n = bsz * t_len
    z = _in_proj(x.reshape(n, D), lw["w_in"], lw["b_in"]).reshape(bsz, t_len, IN_COLS)
    x1, new_a, new_b, h_t, s_t = _mixer(x, z, past_a, past_b, h0, s0, lb, lw)
    x1 = x1.reshape(n, D)
    tn = min(n, 512)
    xt, p1, p2, th = _peer_scores(x1, lw["peer_wq"], lw["peer_keys"], tn)
    x2 = _peer_dense(xt, x1, lw["peer_u"], lw["peer_vt"], p1, p2, th, lw["ln2_g"], lw["ln2_b"], tn)
    return x2.reshape(bsz, t_len, D), new_a, new_b, h_t.reshape(bsz, D), s_t


def kernel(x_prompt, x_sample, state_conv_a, state_conv_b, state_lru, state_hgrn, w_in, b_in, conv_a_w,
           conv_b_w, conv_b_b, lru_wa, lru_ba, lru_wx, lru_bx, lru_lambda, hgrn_lb_logits, hgrn_norm_g,
           w_out_a, w_out_b, w_out_c, w_o, ln1_g, ln1_b, peer_wq, peer_keys, peer_u, peer_v, ln2_g,
           ln2_b):
    p = jax.nn.softmax(hgrn_lb_logits.astype(F32), axis=0)
    lbs = jnp.cumsum(p, axis=0) - p[0]
    bp = x_prompt.shape[0]
    xp, xs = x_prompt, x_sample
    outs_p, outs_s = [], []
    for l in range(DEPTH):
        row = lambda a: a[l].reshape(1, -1)
        lw = dict(
            w_in=w_in[l].astype(BF), b_in=row(b_in),
            conv_a_w=conv_a_w[l], conv_b_w=conv_b_w[l], conv_b_b=row(conv_b_b),
            lru_wa=lru_wa[l].astype(BF), lru_ba=row(lru_ba), lru_wx=lru_wx[l].astype(BF),
            lru_bx=row(lru_bx), lru_lambda=row(lru_lambda), hgrn_norm_g=row(hgrn_norm_g),
            w_out_a=w_out_a[l].astype(BF), w_out_b=w_out_b[l].astype(BF),
            w_out_c=w_out_c[l].astype(BF), w_o=w_o[l].astype(BF),
            ln1_g=row(ln1_g), ln1_b=row(ln1_b),
            peer_wq=peer_wq[l].astype(BF), peer_keys=peer_keys[l].astype(BF),
            peer_u=peer_u[l].astype(BF), peer_vt=peer_v[l].T.astype(BF),
            ln2_g=row(ln2_g), ln2_b=row(ln2_b))
        lb = lbs[l].reshape(1, D)
        xp, pa, pb, ph, ps = _trunk_layer(
            xp, jnp.zeros((bp, CONV_A_W - 1, D), F32), jnp.zeros((bp, CONV_B_W - 1, D), F32),
            jnp.zeros((bp, 1, D), F32), jnp.zeros((bp, HEADS, HD, HD), F32), lb, lw)
        xs, sa, sb, sh, ss = _trunk_layer(
            xs, state_conv_a[l], state_conv_b[l], state_lru[l][:, None, :], state_hgrn[l], lb, lw)
        outs_p.append((pa, pb, ph, ps))
        outs_s.append((sa, sb, sh, ss))
    stack = lambda outs, k: jnp.stack([o[k] for o in outs])
    return (xp, xs,
            stack(outs_p, 0), stack(outs_p, 1), stack(outs_p, 2), stack(outs_p, 3),
            stack(outs_s, 0), stack(outs_s, 1), stack(outs_s, 2), stack(outs_s, 3))
```

```python
import functools

import jax
import jax.numpy as jnp
from jax import lax
from jax.experimental import pallas as pl
from jax.experimental.pallas import tpu as pltpu

D = 1024
DEPTH = 2
CONV_A_W = 3
CONV_B_W = 4
LRU_BLOCKS = 8
LRU_BW = D // LRU_BLOCKS
LRU_C = 8.0
HEADS = 8
HD = D // HEADS
PEER_HEADS = 8
PEER_DH = 128
N_KEYS = 128
N_EXPERTS = N_KEYS * N_KEYS
TOPK = 16
ALPHA = (2.0 * DEPTH) ** 0.25
LN_EPS = 1e-5
RMS_EPS = 1e-6
IN_COLS = 12 * D
SUB = 16
INTRA_CAP = 60.0
LANES = 128
SUBLANES = 8
VMEM_LIMIT = 56 * 1024 * 1024
GATE_KEYS = 4
GATE_ROWS = 64
SCORE_TOKENS = 256
MXU_ROWS = 512

OFF_AB, OFF_AC, OFF_AX, OFF_BX, OFF_BG, OFF_CQ, OFF_CF, OFF_CI, OFF_CG, OFF_GA, OFF_GB, OFF_GC = (
    i * D for i in range(12))

BF = jnp.bfloat16
F32 = jnp.float32


def _sigmoid(x):
    return 0.5 * jnp.tanh(0.5 * x) + 0.5


def _gelu(x):
    return 0.5 * x * (1.0 + lax.erf(x * (2.0 ** -0.5)))


def _layer_norm(x, g, b):
    mu = jnp.mean(x, axis=-1, keepdims=True)
    xc = x - mu
    var = jnp.mean(xc * xc, axis=-1, keepdims=True)
    return xc * lax.rsqrt(var + LN_EPS) * g + b


def _nt_dot(a, b):
    return lax.dot_general(a, b, (((1,), (1,)), ((), ())), preferred_element_type=F32)


def _tn_dot(a, b):
    return lax.dot_general(a, b, (((0,), (0,)), ((), ())), preferred_element_type=F32)


def _shift_rows(x, d, fill, rows):
    return jnp.where(rows >= d, pltpu.roll(x, d, 0), fill)


def _scan_rows(a, u, carry, rows):
    in_group = rows & (SUBLANES - 1)
    d = 1
    while d < SUBLANES:
        if a is None:
            u = u + _shift_rows(u, d, 0.0, in_group)
        else:
            u = a * _shift_rows(u, d, 0.0, in_group) + u
            a = a * _shift_rows(a, d, 1.0, in_group)
        d *= 2
    out = []
    for g in range(u.shape[0] // SUBLANES):
        rs = slice(g * SUBLANES, (g + 1) * SUBLANES)
        h_g = u[rs, :] + carry if a is None else u[rs, :] + a[rs, :] * carry
        out.append(h_g)
        carry = h_g[SUBLANES - 1:SUBLANES, :]
    return jnp.concatenate(out, axis=0)


def _inproj_body(x_ref, w_ref, b_ref, o_ref):
    x = x_ref[...].astype(BF)
    o_ref[...] = jnp.dot(x, w_ref[...], preferred_element_type=F32) + b_ref[...]


def _in_proj(x2d, w_bf, b_row):
    n = x2d.shape[0]
    tm = min(n, 1024)
    tn = 2048
    return pl.pallas_call(
        _inproj_body,
        grid=(n // tm, IN_COLS // tn),
        in_specs=[pl.BlockSpec((tm, D), lambda i, j: (i, 0)),
                  pl.BlockSpec((D, tn), lambda i, j: (0, j)),
                  pl.BlockSpec((1, tn), lambda i, j: (0, j))],
        out_specs=pl.BlockSpec((tm, tn), lambda i, j: (i, j)),
        out_shape=jax.ShapeDtypeStruct((n, IN_COLS), F32),
        compiler_params=pltpu.CompilerParams(
            dimension_semantics=("arbitrary", "arbitrary"), vmem_limit_bytes=VMEM_LIMIT),
        name="in_proj",
    )(x2d, w_bf, b_row)


def _mixer_body(x_ref, z_ref, pa_ref, pb_ref, h0_ref, s0_ref, lb_ref, cwa_ref, cwb_ref, cbb_ref,
                wa_ref, ba_ref, wx_ref, bx_ref, lam_ref, ng_ref, woa_ref, wob_ref, woc_ref, wo_ref,
                g1_ref, b1_ref,
                x1_ref, na_ref, nb_ref, nh_ref, ns_ref,
                ua, ub, hc, st, qs, ks, vs, bs, att, oc, kf, bf, qsil, qe, kd, att_all, *, chunk):
    c = pl.program_id(1)
    last = pl.num_programs(1) - 1
    nsub = chunk // SUB
    rows = lax.broadcasted_iota(jnp.int32, (chunk, D), 0)

    rows8 = lax.broadcasted_iota(jnp.int32, (SUBLANES, D), 0)

    def delayed(x, d, prev):
        out = []
        below = pltpu.roll(prev, d, 0)
        for g in range(chunk // SUBLANES):
            cur = pltpu.roll(x[g * SUBLANES:(g + 1) * SUBLANES, :], d, 0)
            out.append(jnp.where(rows8 < d, below, cur))
            below = cur
        return jnp.concatenate(out, axis=0)

    @pl.when(c == 0)
    def _():
        ua[...] = jnp.zeros((SUBLANES, D), F32)
        ub[...] = jnp.zeros((SUBLANES, D), F32)
        ua[SUBLANES - 2:SUBLANES, :] = pa_ref[...]
        ub[SUBLANES - 3:SUBLANES, :] = pb_ref[...]
        hc[0:1, :] = h0_ref[...]
        for h in range(HEADS):
            st[h] = s0_ref[h].T
        att_all[...] = jnp.zeros((HEADS, chunk, chunk), F32)

    u = z_ref[:, OFF_AC:OFF_AC + D] * z_ref[:, OFF_AX:OFF_AX + D]
    prev_a = ua[...]
    conv_a = (cwa_ref[0:1, :] * delayed(u, 2, prev_a) + cwa_ref[1:2, :] * delayed(u, 1, prev_a)
              + cwa_ref[2:3, :] * u)
    na_ref[...] = u[chunk - (CONV_A_W - 1):chunk, :]
    ua[...] = u[chunk - SUBLANES:chunk, :]
    y_a = jnp.dot((z_ref[:, OFF_AB:OFF_AB + D] * conv_a).astype(BF), woa_ref[...],
                  preferred_element_type=F32)
    merged = _sigmoid(z_ref[:, OFF_GA:OFF_GA + D]) * y_a

    xb = z_ref[:, OFF_BX:OFF_BX + D]
    prev_b = ub[...]
    xl = (cwb_ref[0:1, :] * delayed(xb, 3, prev_b) + cwb_ref[1:2, :] * delayed(xb, 2, prev_b)
          + cwb_ref[2:3, :] * delayed(xb, 1, prev_b) + cwb_ref[3:4, :] * xb) + cbb_ref[...]
    nb_ref[...] = xb[chunk - (CONV_B_W - 1):chunk, :]
    ub[...] = xb[chunk - SUBLANES:chunk, :]

    xl_bf = xl.astype(BF)
    ra = jnp.concatenate(
        [jnp.dot(xl_bf[:, n * LRU_BW:(n + 1) * LRU_BW], wa_ref[n], preferred_element_type=F32)
         for n in range(LRU_BLOCKS)], axis=1)
    rx = jnp.concatenate(
        [jnp.dot(xl_bf[:, n * LRU_BW:(n + 1) * LRU_BW], wx_ref[n], preferred_element_type=F32)
         for n in range(LRU_BLOCKS)], axis=1)
    r = _sigmoid(ra + ba_ref[...])
    gi = _sigmoid(rx + bx_ref[...])
    nlam = -lam_ref[...]
    softplus = jnp.maximum(nlam, 0.0) + jnp.log1p(jnp.exp(-jnp.abs(nlam)))
    log_a = (-LRU_C) * r * softplus
    a = jnp.exp(log_a)
    uu = jnp.sqrt(jnp.tanh(-log_a) * (a * a + 1.0)) * (gi * xl)
    hh = _scan_rows(a, uu, hc[0:1, :], rows)
    h_last = hh[chunk - 1:chunk, :]
    hc[0:1, :] = h_last
    nh_ref[...] = h_last
    y_b = jnp.dot((_gelu(z_ref[:, OFF_BG:OFF_BG + D]) * hh).astype(BF), wob_ref[...],
                  preferred_element_type=F32)
    merged = merged + _sigmoid(z_ref[:, OFF_GB:OFF_GB + D]) * y_b

    cols_a = lax.broadcasted_iota(jnp.int32, (SUB, chunk), 1)
    sub_rows = lax.broadcasted_iota(jnp.int32, (SUB, 1), 0)
    f_all = lb_ref[...] + (1.0 - lb_ref[...]) * _sigmoid(z_ref[:, OFF_CF:OFF_CF + D])
    kf[...] = 1.0 - f_all
    bf[...] = _scan_rows(None, jnp.log(f_all), jnp.zeros((1, D), F32), rows)
    intra_bound = jnp.max(functools.reduce(jnp.maximum, [
        (bf[i * SUB - 1:i * SUB, :] if i else 0.0) - bf[(i + 1) * SUB - 1:(i + 1) * SUB, :]
        for i in range(nsub)]))

    def head_step(h, carry):
        off = pl.multiple_of(h * HD, HD)
        k = kf[:, pl.ds(off, HD)]
        b = bf[:, pl.ds(off, HD)]
        cq = z_ref[:, pl.ds(OFF_CQ + off, HD)]
        q = cq * _sigmoid(cq)
        v = z_ref[:, pl.ds(OFF_CI + off, HD)]
        qs[...] = q
        ks[...] = k
        vs[...] = v
        bs[...] = b
        st_h = st[h]
        b_last = b[chunk - 1:chunk, :]
        o = _nt_dot((q * jnp.exp(b)).astype(BF), st_h.astype(BF))
        kdec = k * jnp.exp(b_last - b)
        st[h] = st_h * jnp.exp(b_last) + _tn_dot(v.astype(BF), kdec.astype(BF))

        def sub_start(i):
            return bs[i * SUB - 1:i * SUB, :] if i else jnp.zeros((1, HD), F32)

        def intra_direct():
            att[0:SUB, :] = jnp.zeros((SUB, chunk), F32)
            for i in range(1, nsub):
                lo, hi = i * SUB, (i + 1) * SUB
                q_i = qs[lo:hi, :] * jnp.exp(bs[lo:hi, :] - sub_start(i))
                k_i = k * jnp.exp(jnp.minimum(sub_start(i) - b, 0.0))
                a_i = _nt_dot(q_i.astype(BF), k_i.astype(BF))
                att[lo:hi, :] = jnp.where(cols_a < lo, a_i, 0.0)
            o_off = jnp.dot(att[...].astype(BF), v.astype(BF), preferred_element_type=F32)
            o_parts = []
            for i in range(nsub):
                q_i = qs[i * SUB:(i + 1) * SUB, :]
                b_i = bs[i * SUB:(i + 1) * SUB, :]
                o_i = jnp.zeros((SUB, HD), F32)
                for s in range(SUB):
                    r0 = i * SUB + s
                    p = q_i * ks[r0:r0 + 1, :] * jnp.exp(jnp.minimum(b_i - bs[r0:r0 + 1, :], 0.0))
                    a_col = jnp.sum(p, axis=-1, keepdims=True)
                    a_col = jnp.where(sub_rows >= s, a_col, 0.0)
                    o_i = o_i + a_col * vs[r0:r0 + 1, :]
                o_parts.append(o_i)
            return o_off + jnp.concatenate(o_parts, axis=0)

        o = o + intra_direct()
        oc[:, pl.ds(off, HD)] = o * lax.rsqrt(jnp.mean(o * o, axis=-1, keepdims=True) + RMS_EPS) * ng_ref[...]
        return carry

    def hgrn_factored():
        cq = z_ref[:, OFF_CQ:OFF_CQ + D]
        qsil[...] = cq * _sigmoid(cq)
        b_last = bf[chunk - 1:chunk, :]
        qe[...] = (qsil[...] * jnp.exp(bf[...])).astype(BF)
        kd[...] = (kf[...] * jnp.exp(b_last - bf[...])).astype(BF)
        for i in range(nsub):
            lo, hi = i * SUB, (i + 1) * SUB
            start = bf[lo - 1:lo, :] if i else 0.0
            q_i = (qsil[lo:hi, :] * jnp.exp(bf[lo:hi, :] - start)).astype(BF)
            k_i = (kf[0:hi, :] * jnp.exp(jnp.minimum(start - bf[0:hi, :], INTRA_CAP))).astype(BF)
            causal = (lax.broadcasted_iota(jnp.int32, (SUB, hi), 1)
                      <= lax.broadcasted_iota(jnp.int32, (SUB, hi), 0) + lo)
            for h in range(HEADS):
                hs = slice(h * HD, (h + 1) * HD)
                att_all[h, lo:hi, 0:hi] = jnp.where(causal, _nt_dot(q_i[:, hs], k_i[:, hs]), 0.0)
        for h in range(HEADS):
            hs = slice(h * HD, (h + 1) * HD)
            v_h = z_ref[:, OFF_CI + h * HD:OFF_CI + (h + 1) * HD].astype(BF)
            st_h = st[h]
            o = (_nt_dot(qe[:, hs], st_h.astype(BF))
                 + jnp.dot(att_all[h].astype(BF), v_h, preferred_element_type=F32))
            st[h] = st_h * jnp.exp(b_last[:, hs]) + _tn_dot(v_h, kd[:, hs])
            oc[:, hs] = o * lax.rsqrt(jnp.mean(o * o, axis=-1, keepdims=True) + RMS_EPS) * ng_ref[...]

    def hgrn_direct():
        lax.fori_loop(0, HEADS, head_step, 0)

    lax.cond(intra_bound <= INTRA_CAP, hgrn_factored, hgrn_direct)

    @pl.when(c == last)
    def _():
        for h in range(HEADS):
            ns_ref[h] = st[h].T

    cg = z_ref[:, OFF_CG:OFF_CG + D]
    y_c = jnp.dot((oc[...] * (cg * _sigmoid(cg))).astype(BF), woc_ref[...], preferred_element_type=F32)
    merged = merged + _sigmoid(z_ref[:, OFF_GC:OFF_GC + D]) * y_c
    res = ALPHA * x_ref[...] + jnp.dot(merged.astype(BF), wo_ref[...], preferred_element_type=F32)
    x1_ref[...] = _layer_norm(res, g1_ref[...], b1_ref[...])


def _mixer(x, z, past_a, past_b, h0, s0, lb, lw):
    bsz, t_len, _ = x.shape
    chunk = min(t_len, 128)
    assert t_len % chunk == 0 and chunk % SUB == 0
    nc = t_len // chunk
    full2 = lambda shape: pl.BlockSpec(shape, lambda b, c: (0, 0))
    full3 = lambda shape: pl.BlockSpec(shape, lambda b, c: (0, 0, 0))
    in_specs = [
        pl.BlockSpec((None, chunk, D), lambda b, c: (b, c, 0)),
        pl.BlockSpec((None, chunk, IN_COLS), lambda b, c: (b, c, 0)),
        pl.BlockSpec((None, CONV_A_W - 1, D), lambda b, c: (b, 0, 0)),
        pl.BlockSpec((None, CONV_B_W - 1, D), lambda b, c: (b, 0, 0)),
        pl.BlockSpec((None, 1, D), lambda b, c: (b, 0, 0)),
        pl.BlockSpec((None, HEADS, HD, HD), lambda b, c: (b, 0, 0, 0)),
        full2((1, D)),
        full2((CONV_A_W, D)), full2((CONV_B_W, D)), full2((1, D)),
        full3((LRU_BLOCKS, LRU_BW, LRU_BW)), full2((1, D)),
        full3((LRU_BLOCKS, LRU_BW, LRU_BW)), full2((1, D)),
        full2((1, D)), full2((1, HD)),
        full2((D, D)), full2((D, D)), full2((D, D)), full2((D, D)),
        full2((1, D)), full2((1, D)),
    ]
    out_specs = [
        pl.BlockSpec((None, chunk, D), lambda b, c: (b, c, 0)),
        pl.BlockSpec((None, CONV_A_W - 1, D), lambda b, c: (b, 0, 0)),
        pl.BlockSpec((None, CONV_B_W - 1, D), lambda b, c: (b, 0, 0)),
        pl.BlockSpec((None, 1, D), lambda b, c: (b, 0, 0)),
        pl.BlockSpec((None, HEADS, HD, HD), lambda b, c: (b, 0, 0, 0)),
    ]
    out_shape = [
        jax.ShapeDtypeStruct((bsz, t_len, D), F32),
        jax.ShapeDtypeStruct((bsz, CONV_A_W - 1, D), F32),
        jax.ShapeDtypeStruct((bsz, CONV_B_W - 1, D), F32),
        jax.ShapeDtypeStruct((bsz, 1, D), F32),
        jax.ShapeDtypeStruct((bsz, HEADS, HD, HD), F32),
    ]
    scratch = [
        pltpu.VMEM((SUBLANES, D), F32),
        pltpu.VMEM((SUBLANES, D), F32),
        pltpu.VMEM((SUBLANES, D), F32),
        pltpu.VMEM((HEADS, HD, HD), F32),
        pltpu.VMEM((chunk, HD), F32),
        pltpu.VMEM((chunk, HD), F32),
        pltpu.VMEM((chunk, HD), F32),
        pltpu.VMEM((chunk, HD), F32),
        pltpu.VMEM((chunk, chunk), F32),
        pltpu.VMEM((chunk, D), F32),
        pltpu.VMEM((chunk, D), F32),
        pltpu.VMEM((chunk, D), F32),
        pltpu.VMEM((chunk, D), F32),
        pltpu.VMEM((chunk, D), BF),
        pltpu.VMEM((chunk, D), BF),
        pltpu.VMEM((HEADS, chunk, chunk), F32),
    ]
    return pl.pallas_call(
        functools.partial(_mixer_body, chunk=chunk),
        grid=(bsz, nc),
        in_specs=in_specs, out_specs=out_specs, out_shape=out_shape, scratch_shapes=scratch,
        compiler_params=pltpu.CompilerParams(
            dimension_semantics=("arbitrary", "arbitrary"), vmem_limit_bytes=VMEM_LIMIT),
        name="mixer",
    )(x, z, past_a, past_b, h0, s0, lb, lw["conv_a_w"], lw["conv_b_w"], lw["conv_b_b"],
      lw["lru_wa"], lw["lru_ba"], lw["lru_wx"], lw["lru_bx"], lw["lru_lambda"], lw["hgrn_norm_g"],
      lw["w_out_a"], lw["w_out_b"], lw["w_out_c"], lw["w_o"], lw["ln1_g"], lw["ln1_b"])


def _sort_network(n):
    pairs = []

    def merge(lo, m, r):
        step = 2 * r
        if step < m:
            merge(lo, m, step)
            merge(lo + r, m, step)
            pairs.extend((i, i + r) for i in range(lo + r, lo + m - r, step))
        else:
            pairs.append((lo, lo + r))

    def sort(lo, m):
        if m > 1:
            sort(lo, m // 2)
            sort(lo + m // 2, m // 2)
            merge(lo, m, 1)

    sort(0, n)
    return pairs


_SORT16 = _sort_network(TOPK)


def _cmp_exchange(v, i, j):
    if v[j] is None:
        return
    if v[i] is None:
        v[i], v[j] = v[j], None
        return
    v[i], v[j] = jnp.maximum(v[i], v[j]), jnp.minimum(v[i], v[j])


def _sort_desc(v):
    v = list(v) + [None] * (TOPK - len(v))
    for i, j in _SORT16:
        _cmp_exchange(v, i, j)
    return v


def _bitonic_to_desc(v):
    v = list(v)
    d = TOPK // 2
    while d >= 1:
        for i in range(TOPK):
            if i & d == 0:
                _cmp_exchange(v, i, i + d)
        d //= 2
    return v


def _merge_rolled(v, shift):
    w = [None if x is None else pltpu.roll(x, shift, 0) for x in v]
    out = []
    for i in range(TOPK):
        a, b = v[i], w[TOPK - 1 - i]
        out.append(b if a is None else a if b is None else jnp.maximum(a, b))
    return out


def _top16_of_rows(vregs):
    v = _sort_desc(vregs)
    for shift in (4, 2, 1):
        v = _bitonic_to_desc(_merge_rolled(v, shift))
    return v


def _kth16_of_rows(vregs):
    v = _sort_desc(vregs)
    for shift in (4, 2):
        v = _bitonic_to_desc(_merge_rolled(v, shift))
    return functools.reduce(jnp.minimum, _merge_rolled(v, 1))


def _pack_rows(vs, sub):
    out = vs[0]
    for j in range(1, SUBLANES):
        out = jnp.where(sub == j, vs[j], out)
    return out


def _scores_body(x_ref, wq_ref, keys_ref, xt_ref, p1_ref, p2_ref, th_ref, qbuf, *, tn):
    xt_ref[...] = x_ref[...].T.astype(BF)
    qbuf[...] = jnp.dot(x_ref[...].astype(BF), wq_ref[...], preferred_element_type=F32).astype(BF)
    tw = SCORE_TOKENS
    sub = lax.broadcasted_iota(jnp.int32, (SUBLANES, tw), 0)
    n_tc = tn // tw

    def step(idx, carry):
        h = idx // n_tc
        tc = idx % n_tc
        t0 = pl.multiple_of(tc * tw, tw)
        c0 = pl.multiple_of(h * (2 * PEER_DH), 2 * PEER_DH)
        s1 = _nt_dot(keys_ref[h, 0], qbuf[pl.ds(t0, tw), pl.ds(c0, PEER_DH)])
        s2 = _nt_dot(keys_ref[h, 1], qbuf[pl.ds(t0, tw), pl.ds(c0 + PEER_DH, PEER_DH)])
        split = lambda s: [s[r * SUBLANES:(r + 1) * SUBLANES, :] for r in range(N_KEYS // SUBLANES)]
        a = _top16_of_rows(split(s1))
        b = _top16_of_rows(split(s2))
        pa = [jnp.exp(x - a[0]) for x in a]
        pb = [jnp.exp(x - b[0]) for x in b]
        pb_lo, pb_hi = _pack_rows(pb[:SUBLANES], sub), _pack_rows(pb[SUBLANES:], sub)
        pa_hi = _pack_rows(pa[SUBLANES:], sub)
        lhs = [pa[0], pa[0]] + pa[1:SUBLANES] + [pa_hi]
        rhs = [pb_lo, pb_hi] + [pb_lo] * (SUBLANES - 1) + [pb[0]]
        t = [x * y for x, y in zip(lhs, rhs)]
        t16 = _kth16_of_rows(t)
        keep = [x >= t16 for x in t]
        z_sum = jnp.sum(sum(jnp.where(m, x, 0.0) for m, x in zip(keep, t)), axis=0, keepdims=True)
        rz = 1.0 / z_sum
        gates = [(x * rz) * y for x, y in zip(lhs, rhs)]
        th = jnp.min(functools.reduce(jnp.minimum, [jnp.where(m, g, jnp.inf) for m, g in zip(keep, gates)]),
                     axis=0, keepdims=True)
        p1_ref[h, :, pl.ds(t0, tw)] = jnp.exp(s1 - a[0][0:1]) * rz
        p2 = jnp.exp(s2 - b[0][0:1])
        for part in range(tw // LANES):
            p2_ref[h, tc * (tw // LANES) + part] = p2[:, part * LANES:(part + 1) * LANES]
        th_ref[h, :, pl.ds(t0, tw)] = th
        return carry

    lax.fori_loop(0, PEER_HEADS * n_tc, step, 0)


def _peer_scores(x2d, wq_bf, keys_bf, tn):
    n = x2d.shape[0]
    n_tc = tn // LANES
    return pl.pallas_call(
        functools.partial(_scores_body, tn=tn),
        grid=(n // tn,),
        in_specs=[pl.BlockSpec((tn, D), lambda i: (i, 0)),
                  pl.BlockSpec((D, 2 * PEER_HEADS * PEER_DH), lambda i: (0, 0)),
                  pl.BlockSpec((PEER_HEADS, 2, N_KEYS, PEER_DH), lambda i: (0, 0, 0, 0))],
        out_specs=[pl.BlockSpec((D, tn), lambda i: (0, i)),
                   pl.BlockSpec((PEER_HEADS, N_KEYS, tn), lambda i: (0, 0, i)),
                   pl.BlockSpec((PEER_HEADS, n_tc, N_KEYS, LANES), lambda i: (0, i, 0, 0)),
                   pl.BlockSpec((PEER_HEADS, 1, tn), lambda i: (0, 0, i))],
        out_shape=[jax.ShapeDtypeStruct((D, n), BF),
                   jax.ShapeDtypeStruct((PEER_HEADS, N_KEYS, n), F32),
                   jax.ShapeDtypeStruct((PEER_HEADS, n // LANES, N_KEYS, LANES), F32),
                   jax.ShapeDtypeStruct((PEER_HEADS, 1, n), F32)],
        scratch_shapes=[pltpu.VMEM((tn, 2 * PEER_HEADS * PEER_DH), BF)],
        compiler_params=pltpu.CompilerParams(
            dimension_semantics=("arbitrary",), vmem_limit_bytes=VMEM_LIMIT),
        name="peer_scores",
    )(x2d, wq_bf, keys_bf)


def _dense_body(xt_ref, x_ref, u0_ref, un_ref, vp_ref, p1_ref, p2_ref, th_ref, g2_ref, b2_ref,
                y_ref, acc, h_even, h_odd, g_even, g_odd, *, tn, eb, nb):
    j = pl.program_id(1)
    kb = eb // N_KEYS
    n_tc = tn // LANES

    @pl.when(j == 0)
    def _():
        acc[...] = jnp.zeros((D, tn), F32)
        g_odd[:, 0:tn] = jnp.zeros((eb, tn), BF)
        h_even[:, 0:tn] = jnp.dot(u0_ref[...], xt_ref[...], preferred_element_type=F32)

    mxu_w = min(tn, 2 * LANES)
    n_mc = tn // mxu_w

    mxu_rows = MXU_ROWS

    def h_piece(ig, mc, h_next):
        rows = slice(ig * mxu_rows, (ig + 1) * mxu_rows)
        cols = slice(mc * mxu_w, (mc + 1) * mxu_w)
        h_next[rows, cols] = jnp.dot(un_ref[rows, :], xt_ref[:, cols], preferred_element_type=F32)

    def acc_piece(ig, mc, g_prev):
        rows = slice(ig * mxu_rows, (ig + 1) * mxu_rows)
        cols = slice(mc * mxu_w, (mc + 1) * mxu_w)
        acc[rows, cols] += jnp.dot(vp_ref[rows, :], g_prev[:, cols], preferred_element_type=F32)

    def gate_piece(k0, tc, half, h_cur, g_cur):
        lanes = slice(tc * LANES, (tc + 1) * LANES)
        sub_rows = slice(half * GATE_ROWS, (half + 1) * GATE_ROWS)
        w = [jnp.zeros((GATE_ROWS, LANES), F32) for _ in range(GATE_KEYS)]
        for h in range(PEER_HEADS):
            p2 = p2_ref[h, tc, sub_rows, :]
            th = th_ref[h, :, lanes]
            for a in range(GATE_KEYS):
                prod = p1_ref[h, k0 + a:k0 + a + 1, lanes] * p2
                w[a] = w[a] + jnp.where(prod >= th, prod, 0.0)
        for a in range(GATE_KEYS):
            r0 = (k0 + a) * N_KEYS + half * GATE_ROWS
            g_cur[r0:r0 + GATE_ROWS, lanes] = (w[a] * _gelu(h_cur[r0:r0 + GATE_ROWS, lanes])).astype(BF)

    def block_step(h_cur, h_next, g_cur, g_prev):
        mxu = [functools.partial(piece, ig, mc, buf)
               for ig in range(eb // mxu_rows)
               for piece, buf in ((h_piece, h_next), (acc_piece, g_prev))
               for mc in range(n_mc)]
        vpu = [functools.partial(gate_piece, k0, tc, half, h_cur, g_cur)
               for k0 in range(0, kb, GATE_KEYS)
               for tc in range(n_tc) for half in range(N_KEYS // GATE_ROWS)]
        per_mxu = -(-len(vpu) // len(mxu))
        while mxu or vpu:
            if mxu:
                mxu.pop(0)()
            for _ in range(per_mxu):
                if vpu:
                    vpu.pop(0)()

    @pl.when((j < nb) & (j % 2 == 0))
    def _():
        block_step(h_even, h_odd, g_even, g_odd)

    @pl.when((j < nb) & (j % 2 == 1))
    def _():
        block_step(h_odd, h_even, g_odd, g_even)

    @pl.when(j == nb)
    def _():
        g_last = g_even if (nb - 1) % 2 == 0 else g_odd

        for ig in range(D // mxu_rows):
            for mc in range(n_mc):
                acc_piece(ig, mc, g_last)
        y_ref[...] = _layer_norm(ALPHA * x_ref[...] + acc[...].T, g2_ref[...], b2_ref[...])


def _peer_dense(xt_bf, x2d, u_bf, vt_bf, p1, p2, th, g2, b2, tn):
    n = x2d.shape[0]
    eb = D
    nb = N_EXPERTS // eb
    n_tc = tn // LANES
    assert eb // N_KEYS == SUBLANES and (eb // N_KEYS) % GATE_KEYS == 0
    return pl.pallas_call(
        functools.partial(_dense_body, tn=tn, eb=eb, nb=nb),
        grid=(n // tn, nb + 1),
        in_specs=[pl.BlockSpec((D, tn), lambda i, j: (0, i)),
                  pl.BlockSpec((tn, D), lambda i, j: (i, 0)),
                  pl.BlockSpec((eb, D), lambda i, j: (0, 0)),
                  pl.BlockSpec((eb, D), lambda i, j: (jnp.minimum(j + 1, nb - 1), 0)),
                  pl.BlockSpec((D, eb), lambda i, j: (0, jnp.maximum(j - 1, 0))),
                  pl.BlockSpec((PEER_HEADS, eb // N_KEYS, tn),
                               lambda i, j: (0, jnp.minimum(j, nb - 1), i)),
                  pl.BlockSpec((PEER_HEADS, n_tc, N_KEYS, LANES), lambda i, j: (0, i, 0, 0)),
                  pl.BlockSpec((PEER_HEADS, 1, tn), lambda i, j: (0, 0, i)),
                  pl.BlockSpec((1, D), lambda i, j: (0, 0)),
                  pl.BlockSpec((1, D), lambda i, j: (0, 0))],
        out_specs=pl.BlockSpec((tn, D), lambda i, j: (i, 0)),
        out_shape=jax.ShapeDtypeStruct((n, D), F32),
        scratch_shapes=[pltpu.VMEM((D, tn), F32),
                        pltpu.VMEM((eb, tn + LANES), F32), pltpu.VMEM((eb, tn + LANES), F32),
                        pltpu.VMEM((eb, tn + LANES), BF), pltpu.VMEM((eb, tn + LANES), BF)],
        compiler_params=pltpu.CompilerParams(
            dimension_semantics=("arbitrary", "arbitrary"), vmem_limit_bytes=VMEM_LIMIT),
        name="peer_dense",
    )(xt_bf, x2d, u_bf, u_bf, vt_bf, p1, p2, th, g2, b2)


def _trunk_layer(x, past_a, past_b, h0, s0, lb, lw):
    bsz, t_len, _ = x.shape
    n = bsz * t_len
    z = _in_proj(x.reshape(n, D), lw["w_in"], lw["b_in"]).reshape(bsz, t_len, IN_COLS)
    x1, new_a, new_b, h_t, s_t = _mixer(x, z, past_a, past_b, h0, s0, lb, lw)
    x1 = x1.reshape(n, D)
    tn = min(n, 512)
    xt, p1, p2, th = _peer_scores(x1, lw["peer_wq"], lw["peer_keys"], tn)
    x2 = _peer_dense(xt, x1, lw["peer_u"], lw["peer_vt"], p1, p2, th, lw["ln2_g"], lw["ln2_b"], tn)
    return x2.reshape(bsz, t_len, D), new_a, new_b, h_t.reshape(bsz, D), s_t


def kernel(x_prompt, x_sample, state_conv_a, state_conv_b, state_lru, state_hgrn, w_in, b_in, conv_a_w,
           conv_b_w, conv_b_b, lru_wa, lru_ba, lru_wx, lru_bx, lru_lambda, hgrn_lb_logits, hgrn_norm_g,
           w_out_a, w_out_b, w_out_c, w_o, ln1_g, ln1_b, peer_wq, peer_keys, peer_u, peer_v, ln2_g,
           ln2_b):
    p = jax.nn.softmax(hgrn_lb_logits.astype(F32), axis=0)
    lbs = jnp.cumsum(p, axis=0) - p[0]
    bp = x_prompt.shape[0]
    xp, xs = x_prompt, x_sample
    outs_p, outs_s = [], []
    for l in range(DEPTH):
        row = lambda a: a[l].reshape(1, -1)
        lw = dict(
            w_in=w_in[l].astype(BF), b_in=row(b_in),
            conv_a_w=conv_a_w[l], conv_b_w=conv_b_w[l], conv_b_b=row(conv_b_b),
            lru_wa=lru_wa[l].astype(BF), lru_ba=row(lru_ba), lru_wx=lru_wx[l].astype(BF),
            lru_bx=row(lru_bx), lru_lambda=row(lru_lambda), hgrn_norm_g=row(hgrn_norm_g),
            w_out_a=w_out_a[l].astype(BF), w_out_b=w_out_b[l].astype(BF),
            w_out_c=w_out_c[l].astype(BF), w_o=w_o[l].astype(BF),
            ln1_g=row(ln1_g), ln1_b=row(ln1_b),
            peer_wq=peer_wq[l].astype(BF), peer_keys=peer_keys[l].astype(BF),
            peer_u=peer_u[l].astype(BF), peer_vt=peer_v[l].T.astype(BF),
            ln2_g=row(ln2_g), ln2_b=row(ln2_b))
        lb = lbs[l].reshape(1, D)
        xp, pa, pb, ph, ps = _trunk_layer(
            xp, jnp.zeros((bp, CONV_A_W - 1, D), F32), jnp.zeros((bp, CONV_B_W - 1, D), F32),
            jnp.zeros((bp, 1, D), F32), jnp.zeros((bp, HEADS, HD, HD), F32), lb, lw)
        xs, sa, sb, sh, ss = _trunk_layer(
            xs, state_conv_a[l], state_conv_b[l], state_lru[l][:, None, :], state_hgrn[l], lb, lw)
        outs_p.append((pa, pb, ph, ps))
        outs_s.append((sa, sb, sh, ss))
    stack = lambda outs, k: jnp.stack([o[k] for o in outs])
    return (xp, xs,
            stack(outs_p, 0), stack(outs_p, 1), stack(outs_p, 2), stack(outs_p, 3),
            stack(outs_s, 0), stack(outs_s, 1), stack(outs_s, 2), stack(outs_s, 3))
```

```python
import functools

import jax
import jax.numpy as jnp
from jax import lax
from jax.experimental import pallas as pl
from jax.experimental.pallas import tpu as pltpu

D = 1024
DEPTH = 2
CONV_A_W = 3
CONV_B_W = 4
LRU_BLOCKS = 8
LRU_BW = D // LRU_BLOCKS
LRU_C = 8.0
HEADS = 8
HD = D // HEADS
PEER_HEADS = 8
PEER_DH = 128
N_KEYS = 128
N_EXPERTS = N_KEYS * N_KEYS
TOPK = 16
ALPHA = (2.0 * DEPTH) ** 0.25
LN_EPS = 1e-5
RMS_EPS = 1e-6
IN_COLS = 12 * D
SUB = 16
INTRA_CAP = 60.0
LANES = 128
SUBLANES = 8
VMEM_LIMIT = 56 * 1024 * 1024
GATE_KEYS = 4
GATE_ROWS = 64
SCORE_TOKENS = 256
MXU_TILE = 256
MXU_ROWS = 512
MXU_STEP = 64

OFF_AB, OFF_AC, OFF_AX, OFF_BX, OFF_BG, OFF_CQ, OFF_CF, OFF_CI, OFF_CG, OFF_GA, OFF_GB, OFF_GC = (
    i * D for i in range(12))

BF = jnp.bfloat16
F32 = jnp.float32


def _sigmoid(x):
    return 0.5 * jnp.tanh(0.5 * x) + 0.5


def _gelu(x):
    return 0.5 * x * (1.0 + lax.erf(x * (2.0 ** -0.5)))


def _layer_norm(x, g, b):
    mu = jnp.mean(x, axis=-1, keepdims=True)
    xc = x - mu
    var = jnp.mean(xc * xc, axis=-1, keepdims=True)
    return xc * lax.rsqrt(var + LN_EPS) * g + b


def _nt_dot(a, b):
    return lax.dot_general(a, b, (((1,), (1,)), ((), ())), preferred_element_type=F32)


def _tn_dot(a, b):
    return lax.dot_general(a, b, (((0,), (0,)), ((), ())), preferred_element_type=F32)


def _shift_rows(x, d, fill, rows):
    return jnp.where(rows >= d, pltpu.roll(x, d, 0), fill)


def _scan_rows(a, u, carry, rows):
    in_group = rows & (SUBLANES - 1)
    d = 1
    while d < SUBLANES:
        if a is None:
            u = u + _shift_rows(u, d, 0.0, in_group)
        else:
            u = a * _shift_rows(u, d, 0.0, in_group) + u
            a = a * _shift_rows(a, d, 1.0, in_group)
        d *= 2
    out = []
    for g in range(u.shape[0] // SUBLANES):
        rs = slice(g * SUBLANES, (g + 1) * SUBLANES)
        h_g = u[rs, :] + carry if a is None else u[rs, :] + a[rs, :] * carry
        out.append(h_g)
        carry = h_g[SUBLANES - 1:SUBLANES, :]
    return jnp.concatenate(out, axis=0)


def _inproj_body(x_ref, w_ref, b_ref, o_ref):
    x = x_ref[...].astype(BF)
    o_ref[...] = jnp.dot(x, w_ref[...], preferred_element_type=F32) + b_ref[...]


def _in_proj(x2d, w_bf, b_row):
    n = x2d.shape[0]
    tm = min(n, 1024)
    tn = 2048
    return pl.pallas_call(
        _inproj_body,
        grid=(n // tm, IN_COLS // tn),
        in_specs=[pl.BlockSpec((tm, D), lambda i, j: (i, 0)),
                  pl.BlockSpec((D, tn), lambda i, j: (0, j)),
                  pl.BlockSpec((1, tn), lambda i, j: (0, j))],
        out_specs=pl.BlockSpec((tm, tn), lambda i, j: (i, j)),
        out_shape=jax.ShapeDtypeStruct((n, IN_COLS), F32),
        compiler_params=pltpu.CompilerParams(
            dimension_semantics=("arbitrary", "arbitrary"), vmem_limit_bytes=VMEM_LIMIT),
        name="in_proj",
    )(x2d, w_bf, b_row)


def _mixer_body(x_ref, z_ref, pa_ref, pb_ref, h0_ref, s0_ref, lb_ref, cwa_ref, cwb_ref, cbb_ref,
                wa_ref, ba_ref, wx_ref, bx_ref, lam_ref, ng_ref, woa_ref, wob_ref, woc_ref, wo_ref,
                g1_ref, b1_ref,
                x1_ref, na_ref, nb_ref, nh_ref, ns_ref,
                ua, ub, hc, st, qs, ks, vs, bs, att, oc, kf, bf, qsil, qe, kd, att_all, *, chunk):
    c = pl.program_id(1)
    last = pl.num_programs(1) - 1
    nsub = chunk // SUB
    rows = lax.broadcasted_iota(jnp.int32, (chunk, D), 0)

    rows8 = lax.broadcasted_iota(jnp.int32, (SUBLANES, D), 0)

    def delayed(x, d, prev):
        out = []
        below = pltpu.roll(prev, d, 0)
        for g in range(chunk // SUBLANES):
            cur = pltpu.roll(x[g * SUBLANES:(g + 1) * SUBLANES, :], d, 0)
            out.append(jnp.where(rows8 < d, below, cur))
            below = cur
        return jnp.concatenate(out, axis=0)

    @pl.when(c == 0)
    def _():
        ua[...] = jnp.zeros((SUBLANES, D), F32)
        ub[...] = jnp.zeros((SUBLANES, D), F32)
        ua[SUBLANES - 2:SUBLANES, :] = pa_ref[...]
        ub[SUBLANES - 3:SUBLANES, :] = pb_ref[...]
        hc[0:1, :] = h0_ref[...]
        for h in range(HEADS):
            st[h] = s0_ref[h].T
        att_all[...] = jnp.zeros((HEADS, chunk, chunk), F32)

    u = z_ref[:, OFF_AC:OFF_AC + D] * z_ref[:, OFF_AX:OFF_AX + D]
    prev_a = ua[...]
    conv_a = (cwa_ref[0:1, :] * delayed(u, 2, prev_a) + cwa_ref[1:2, :] * delayed(u, 1, prev_a)
              + cwa_ref[2:3, :] * u)
    na_ref[...] = u[chunk - (CONV_A_W - 1):chunk, :]
    ua[...] = u[chunk - SUBLANES:chunk, :]
    y_a = jnp.dot((z_ref[:, OFF_AB:OFF_AB + D] * conv_a).astype(BF), woa_ref[...],
                  preferred_element_type=F32)
    merged = _sigmoid(z_ref[:, OFF_GA:OFF_GA + D]) * y_a

    xb = z_ref[:, OFF_BX:OFF_BX + D]
    prev_b = ub[...]
    xl = (cwb_ref[0:1, :] * delayed(xb, 3, prev_b) + cwb_ref[1:2, :] * delayed(xb, 2, prev_b)
          + cwb_ref[2:3, :] * delayed(xb, 1, prev_b) + cwb_ref[3:4, :] * xb) + cbb_ref[...]
    nb_ref[...] = xb[chunk - (CONV_B_W - 1):chunk, :]
    ub[...] = xb[chunk - SUBLANES:chunk, :]

    xl_bf = xl.astype(BF)
    ra = jnp.concatenate(
        [jnp.dot(xl_bf[:, n * LRU_BW:(n + 1) * LRU_BW], wa_ref[n], preferred_element_type=F32)
         for n in range(LRU_BLOCKS)], axis=1)
    rx = jnp.concatenate(
        [jnp.dot(xl_bf[:, n * LRU_BW:(n + 1) * LRU_BW], wx_ref[n], preferred_element_type=F32)
         for n in range(LRU_BLOCKS)], axis=1)
    r = _sigmoid(ra + ba_ref[...])
    gi = _sigmoid(rx + bx_ref[...])
    nlam = -lam_ref[...]
    softplus = jnp.maximum(nlam, 0.0) + jnp.log1p(jnp.exp(-jnp.abs(nlam)))
    log_a = (-LRU_C) * r * softplus
    a = jnp.exp(log_a)
    uu = jnp.sqrt(jnp.tanh(-log_a) * (a * a + 1.0)) * (gi * xl)
    hh = _scan_rows(a, uu, hc[0:1, :], rows)
    h_last = hh[chunk - 1:chunk, :]
    hc[0:1, :] = h_last
    nh_ref[...] = h_last
    y_b = jnp.dot((_gelu(z_ref[:, OFF_BG:OFF_BG + D]) * hh).astype(BF), wob_ref[...],
                  preferred_element_type=F32)
    merged = merged + _sigmoid(z_ref[:, OFF_GB:OFF_GB + D]) * y_b

    cols_a = lax.broadcasted_iota(jnp.int32, (SUB, chunk), 1)
    sub_rows = lax.broadcasted_iota(jnp.int32, (SUB, 1), 0)
    f_all = lb_ref[...] + (1.0 - lb_ref[...]) * _sigmoid(z_ref[:, OFF_CF:OFF_CF + D])
    kf[...] = 1.0 - f_all
    bf[...] = _scan_rows(None, jnp.log(f_all), jnp.zeros((1, D), F32), rows)
    intra_bound = jnp.max(functools.reduce(jnp.maximum, [
        (bf[i * SUB - 1:i * SUB, :] if i else 0.0) - bf[(i + 1) * SUB - 1:(i + 1) * SUB, :]
        for i in range(nsub)]))

    def head_step(h, carry):
        off = pl.multiple_of(h * HD, HD)
        k = kf[:, pl.ds(off, HD)]
        b = bf[:, pl.ds(off, HD)]
        cq = z_ref[:, pl.ds(OFF_CQ + off, HD)]
        q = cq * _sigmoid(cq)
        v = z_ref[:, pl.ds(OFF_CI + off, HD)]
        qs[...] = q
        ks[...] = k
        vs[...] = v
        bs[...] = b
        st_h = st[h]
        b_last = b[chunk - 1:chunk, :]
        o = _nt_dot((q * jnp.exp(b)).astype(BF), st_h.astype(BF))
        kdec = k * jnp.exp(b_last - b)
        st[h] = st_h * jnp.exp(b_last) + _tn_dot(v.astype(BF), kdec.astype(BF))

        def sub_start(i):
            return bs[i * SUB - 1:i * SUB, :] if i else jnp.zeros((1, HD), F32)

        def intra_direct():
            att[0:SUB, :] = jnp.zeros((SUB, chunk), F32)
            for i in range(1, nsub):
                lo, hi = i * SUB, (i + 1) * SUB
                q_i = qs[lo:hi, :] * jnp.exp(bs[lo:hi, :] - sub_start(i))
                k_i = k * jnp.exp(jnp.minimum(sub_start(i) - b, 0.0))
                a_i = _nt_dot(q_i.astype(BF), k_i.astype(BF))
                att[lo:hi, :] = jnp.where(cols_a < lo, a_i, 0.0)
            o_off = jnp.dot(att[...].astype(BF), v.astype(BF), preferred_element_type=F32)
            o_parts = []
            for i in range(nsub):
                q_i = qs[i * SUB:(i + 1) * SUB, :]
                b_i = bs[i * SUB:(i + 1) * SUB, :]
                o_i = jnp.zeros((SUB, HD), F32)
                for s in range(SUB):
                    r0 = i * SUB + s
                    p = q_i * ks[r0:r0 + 1, :] * jnp.exp(jnp.minimum(b_i - bs[r0:r0 + 1, :], 0.0))
                    a_col = jnp.sum(p, axis=-1, keepdims=True)
                    a_col = jnp.where(sub_rows >= s, a_col, 0.0)
                    o_i = o_i + a_col * vs[r0:r0 + 1, :]
                o_parts.append(o_i)
            return o_off + jnp.concatenate(o_parts, axis=0)

        o = o + intra_direct()
        oc[:, pl.ds(off, HD)] = o * lax.rsqrt(jnp.mean(o * o, axis=-1, keepdims=True) + RMS_EPS) * ng_ref[...]
        return carry

    def hgrn_factored():
        cq = z_ref[:, OFF_CQ:OFF_CQ + D]
        qsil[...] = cq * _sigmoid(cq)
        b_last = bf[chunk - 1:chunk, :]
        qe[...] = (qsil[...] * jnp.exp(bf[...])).astype(BF)
        kd[...] = (kf[...] * jnp.exp(b_last - bf[...])).astype(BF)
        for i in range(nsub):
            lo, hi = i * SUB, (i + 1) * SUB
            start = bf[lo - 1:lo, :] if i else 0.0
            q_i = (qsil[lo:hi, :] * jnp.exp(bf[lo:hi, :] - start)).astype(BF)
            k_i = (kf[0:hi, :] * jnp.exp(jnp.minimum(start - bf[0:hi, :], INTRA_CAP))).astype(BF)
            causal = (lax.broadcasted_iota(jnp.int32, (SUB, hi), 1)
                      <= lax.broadcasted_iota(jnp.int32, (SUB, hi), 0) + lo)
            for h in range(HEADS):
                hs = slice(h * HD, (h + 1) * HD)
                att_all[h, lo:hi, 0:hi] = jnp.where(causal, _nt_dot(q_i[:, hs], k_i[:, hs]), 0.0)
        for h in range(HEADS):
            hs = slice(h * HD, (h + 1) * HD)
            v_h = z_ref[:, OFF_CI + h * HD:OFF_CI + (h + 1) * HD].astype(BF)
            st_h = st[h]
            o = (_nt_dot(qe[:, hs], st_h.astype(BF))
                 + jnp.dot(att_all[h].astype(BF), v_h, preferred_element_type=F32))
            st[h] = st_h * jnp.exp(b_last[:, hs]) + _tn_dot(v_h, kd[:, hs])
            oc[:, hs] = o * lax.rsqrt(jnp.mean(o * o, axis=-1, keepdims=True) + RMS_EPS) * ng_ref[...]

    def hgrn_direct():
        lax.fori_loop(0, HEADS, head_step, 0)

    lax.cond(intra_bound <= INTRA_CAP, hgrn_factored, hgrn_direct)

    @pl.when(c == last)
    def _():
        for h in range(HEADS):
            ns_ref[h] = st[h].T

    cg = z_ref[:, OFF_CG:OFF_CG + D]
    y_c = jnp.dot((oc[...] * (cg * _sigmoid(cg))).astype(BF), woc_ref[...], preferred_element_type=F32)
    merged = merged + _sigmoid(z_ref[:, OFF_GC:OFF_GC + D]) * y_c
    res = ALPHA * x_ref[...] + jnp.dot(merged.astype(BF), wo_ref[...], preferred_element_type=F32)
    x1_ref[...] = _layer_norm(res, g1_ref[...], b1_ref[...])


def _mixer(x, z, past_a, past_b, h0, s0, lb, lw):
    bsz, t_len, _ = x.shape
    chunk = min(t_len, 128)
    assert t_len % chunk == 0 and chunk % SUB == 0
    nc = t_len // chunk
    full2 = lambda shape: pl.BlockSpec(shape, lambda b, c: (0, 0))
    full3 = lambda shape: pl.BlockSpec(shape, lambda b, c: (0, 0, 0))
    in_specs = [
        pl.BlockSpec((None, chunk, D), lambda b, c: (b, c, 0)),
        pl.BlockSpec((None, chunk, IN_COLS), lambda b, c: (b, c, 0)),
        pl.BlockSpec((None, CONV_A_W - 1, D), lambda b, c: (b, 0, 0)),
        pl.BlockSpec((None, CONV_B_W - 1, D), lambda b, c: (b, 0, 0)),
        pl.BlockSpec((None, 1, D), lambda b, c: (b, 0, 0)),
        pl.BlockSpec((None, HEADS, HD, HD), lambda b, c: (b, 0, 0, 0)),
        full2((1, D)),
        full2((CONV_A_W, D)), full2((CONV_B_W, D)), full2((1, D)),
        full3((LRU_BLOCKS, LRU_BW, LRU_BW)), full2((1, D)),
        full3((LRU_BLOCKS, LRU_BW, LRU_BW)), full2((1, D)),
        full2((1, D)), full2((1, HD)),
        full2((D, D)), full2((D, D)), full2((D, D)), full2((D, D)),
        full2((1, D)), full2((1, D)),
    ]
    out_specs = [
        pl.BlockSpec((None, chunk, D), lambda b, c: (b, c, 0)),
        pl.BlockSpec((None, CONV_A_W - 1, D), lambda b, c: (b, 0, 0)),
        pl.BlockSpec((None, CONV_B_W - 1, D), lambda b, c: (b, 0, 0)),
        pl.BlockSpec((None, 1, D), lambda b, c: (b, 0, 0)),
        pl.BlockSpec((None, HEADS, HD, HD), lambda b, c: (b, 0, 0, 0)),
    ]
    out_shape = [
        jax.ShapeDtypeStruct((bsz, t_len, D), F32),
        jax.ShapeDtypeStruct((bsz, CONV_A_W - 1, D), F32),
        jax.ShapeDtypeStruct((bsz, CONV_B_W - 1, D), F32),
        jax.ShapeDtypeStruct((bsz, 1, D), F32),
        jax.ShapeDtypeStruct((bsz, HEADS, HD, HD), F32),
    ]
    scratch = [
        pltpu.VMEM((SUBLANES, D), F32),
        pltpu.VMEM((SUBLANES, D), F32),
        pltpu.VMEM((SUBLANES, D), F32),
        pltpu.VMEM((HEADS, HD, HD), F32),
        pltpu.VMEM((chunk, HD), F32),
        pltpu.VMEM((chunk, HD), F32),
        pltpu.VMEM((chunk, HD), F32),
        pltpu.VMEM((chunk, HD), F32),
        pltpu.VMEM((chunk, chunk), F32),
        pltpu.VMEM((chunk, D), F32),
        pltpu.VMEM((chunk, D), F32),
        pltpu.VMEM((chunk, D), F32),
        pltpu.VMEM((chunk, D), F32),
        pltpu.VMEM((chunk, D), BF),
        pltpu.VMEM((chunk, D), BF),
        pltpu.VMEM((HEADS, chunk, chunk), F32),
    ]
    return pl.pallas_call(
        functools.partial(_mixer_body, chunk=chunk),
        grid=(bsz, nc),
        in_specs=in_specs, out_specs=out_specs, out_shape=out_shape, scratch_shapes=scratch,
        compiler_params=pltpu.CompilerParams(
            dimension_semantics=("arbitrary", "arbitrary"), vmem_limit_bytes=VMEM_LIMIT),
        name="mixer",
    )(x, z, past_a, past_b, h0, s0, lb, lw["conv_a_w"], lw["conv_b_w"], lw["conv_b_b"],
      lw["lru_wa"], lw["lru_ba"], lw["lru_wx"], lw["lru_bx"], lw["lru_lambda"], lw["hgrn_norm_g"],
      lw["w_out_a"], lw["w_out_b"], lw["w_out_c"], lw["w_o"], lw["ln1_g"], lw["ln1_b"])


def _sort_network(n):
    pairs = []

    def merge(lo, m, r):
        step = 2 * r
        if step < m:
            merge(lo, m, step)
            merge(lo + r, m, step)
            pairs.extend((i, i + r) for i in range(lo + r, lo + m - r, step))
        else:
            pairs.append((lo, lo + r))

    def sort(lo, m):
        if m > 1:
            sort(lo, m // 2)
            sort(lo + m // 2, m // 2)
            merge(lo, m, 1)

    sort(0, n)
    return pairs


_SORT16 = _sort_network(TOPK)


def _cmp_exchange(v, i, j):
    if v[j] is None:
        return
    if v[i] is None:
        v[i], v[j] = v[j], None
        return
    v[i], v[j] = jnp.maximum(v[i], v[j]), jnp.minimum(v[i], v[j])


def _sort_desc(v):
    v = list(v) + [None] * (TOPK - len(v))
    for i, j in _SORT16:
        _cmp_exchange(v, i, j)
    return v


def _bitonic_to_desc(v):
    v = list(v)
    d = TOPK // 2
    while d >= 1:
        for i in range(TOPK):
            if i & d == 0:
                _cmp_exchange(v, i, i + d)
        d //= 2
    return v


def _merge_rolled(v, shift):
    w = [None if x is None else pltpu.roll(x, shift, 0) for x in v]
    out = []
    for i in range(TOPK):
        a, b = v[i], w[TOPK - 1 - i]
        out.append(b if a is None else a if b is None else jnp.maximum(a, b))
    return out


def _top16_of_rows(vregs):
    v = _sort_desc(vregs)
    for shift in (4, 2, 1):
        v = _bitonic_to_desc(_merge_rolled(v, shift))
    return v


def _kth16_of_rows(vregs):
    v = _sort_desc(vregs)
    for shift in (4, 2):
        v = _bitonic_to_desc(_merge_rolled(v, shift))
    return functools.reduce(jnp.minimum, _merge_rolled(v, 1))


def _pack_rows(vs, sub):
    out = vs[0]
    for j in range(1, SUBLANES):
        out = jnp.where(sub == j, vs[j], out)
    return out


def _scores_body(x_ref, wq_ref, keys_ref, xt_ref, p1_ref, p2_ref, th_ref, qbuf, *, tn):
    xt_ref[...] = x_ref[...].T.astype(BF)
    qbuf[...] = jnp.dot(x_ref[...].astype(BF), wq_ref[...], preferred_element_type=F32).astype(BF)
    tw = SCORE_TOKENS
    sub = lax.broadcasted_iota(jnp.int32, (SUBLANES, tw), 0)
    n_tc = tn // tw

    def step(idx, carry):
        h = idx // n_tc
        tc = idx % n_tc
        t0 = pl.multiple_of(tc * tw, tw)
        c0 = pl.multiple_of(h * (2 * PEER_DH), 2 * PEER_DH)
        s1 = _nt_dot(keys_ref[h, 0], qbuf[pl.ds(t0, tw), pl.ds(c0, PEER_DH)])
        s2 = _nt_dot(keys_ref[h, 1], qbuf[pl.ds(t0, tw), pl.ds(c0 + PEER_DH, PEER_DH)])
        split = lambda s: [s[r * SUBLANES:(r + 1) * SUBLANES, :] for r in range(N_KEYS // SUBLANES)]
        a = _top16_of_rows(split(s1))
        b = _top16_of_rows(split(s2))
        pa = [jnp.exp(x - a[0]) for x in a]
        pb = [jnp.exp(x - b[0]) for x in b]
        pb_lo, pb_hi = _pack_rows(pb[:SUBLANES], sub), _pack_rows(pb[SUBLANES:], sub)
        pa_hi = _pack_rows(pa[SUBLANES:], sub)
        lhs = [pa[0], pa[0]] + pa[1:SUBLANES] + [pa_hi]
        rhs = [pb_lo, pb_hi] + [pb_lo] * (SUBLANES - 1) + [pb[0]]
        t = [x * y for x, y in zip(lhs, rhs)]
        t16 = _kth16_of_rows(t)
        keep = [x >= t16 for x in t]
        z_sum = jnp.sum(sum(jnp.where(m, x, 0.0) for m, x in zip(keep, t)), axis=0, keepdims=True)
        rz = 1.0 / z_sum
        gates = [(x * rz) * y for x, y in zip(lhs, rhs)]
        th = jnp.min(functools.reduce(jnp.minimum, [jnp.where(m, g, jnp.inf) for m, g in zip(keep, gates)]),
                     axis=0, keepdims=True)
        p1_ref[h, :, pl.ds(t0, tw)] = jnp.exp(s1 - a[0][0:1]) * rz
        p2 = jnp.exp(s2 - b[0][0:1])
        for part in range(tw // LANES):
            p2_ref[h, tc * (tw // LANES) + part] = p2[:, part * LANES:(part + 1) * LANES]
        th_ref[h, :, pl.ds(t0, tw)] = th
        return carry

    lax.fori_loop(0, PEER_HEADS * n_tc, step, 0)


def _peer_scores(x2d, wq_bf, keys_bf, tn):
    n = x2d.shape[0]
    n_tc = tn // LANES
    return pl.pallas_call(
        functools.partial(_scores_body, tn=tn),
        grid=(n // tn,),
        in_specs=[pl.BlockSpec((tn, D), lambda i: (i, 0)),
                  pl.BlockSpec((D, 2 * PEER_HEADS * PEER_DH), lambda i: (0, 0)),
                  pl.BlockSpec((PEER_HEADS, 2, N_KEYS, PEER_DH), lambda i: (0, 0, 0, 0))],
        out_specs=[pl.BlockSpec((D, tn), lambda i: (0, i)),
                   pl.BlockSpec((PEER_HEADS, N_KEYS, tn), lambda i: (0, 0, i)),
                   pl.BlockSpec((PEER_HEADS, n_tc, N_KEYS, LANES), lambda i: (0, i, 0, 0)),
                   pl.BlockSpec((PEER_HEADS, 1, tn), lambda i: (0, 0, i))],
        out_shape=[jax.ShapeDtypeStruct((D, n), BF),
                   jax.ShapeDtypeStruct((PEER_HEADS, N_KEYS, n), F32),
                   jax.ShapeDtypeStruct((PEER_HEADS, n // LANES, N_KEYS, LANES), F32),
                   jax.ShapeDtypeStruct((PEER_HEADS, 1, n), F32)],
        scratch_shapes=[pltpu.VMEM((tn, 2 * PEER_HEADS * PEER_DH), BF)],
        compiler_params=pltpu.CompilerParams(
            dimension_semantics=("arbitrary",), vmem_limit_bytes=VMEM_LIMIT),
        name="peer_scores",
    )(x2d, wq_bf, keys_bf)


def _dense_body(xt_ref, x_ref, u0_ref, un_ref, vp_ref, p1_ref, p2_ref, th_ref, g2_ref, b2_ref,
                y_ref, acc, h_even, h_odd, g_even, g_odd, *, tn, eb, nb):
    j = pl.program_id(1)
    kb = eb // N_KEYS
    n_tc = tn // LANES

    def matmul_jobs(jobs):
        assert len(jobs) % 2 == 0
        for p in range(0, len(jobs), 2):
            pair = jobs[p:p + 2]
            base = (p // 2 % 2) * (MXU_ROWS // 4)
            for k in range(D // MXU_TILE):
                ks = slice(k * MXU_TILE, (k + 1) * MXU_TILE)
                for q, (lhs, rhs, r0, c0, _) in enumerate(pair):
                    pltpu.matmul_push_rhs(rhs[ks, c0:c0 + MXU_TILE], staging_register=k % 2, mxu_index=q)
                for m in range(MXU_ROWS // MXU_STEP):
                    for q, (lhs, rhs, r0, c0, _) in enumerate(pair):
                        rows = slice(r0 + m * MXU_STEP, r0 + (m + 1) * MXU_STEP)
                        pltpu.matmul_acc_lhs(base + m * (MXU_STEP // 4), lhs[rows, ks], q,
                                             load_staged_rhs=(k % 2) if m == 0 else None)
                    yield
            for q, (lhs, rhs, r0, c0, sink) in enumerate(pair):
                sink(r0, c0, pltpu.matmul_pop(base, (MXU_ROWS, MXU_TILE), F32, q))

    def tiles(lhs, rhs, sink):
        return [(lhs, rhs, r0, c0, sink) for r0 in range(0, eb, MXU_ROWS) for c0 in range(0, tn, MXU_TILE)]

    def store_into(dst):
        def sink(r0, c0, val):
            dst[r0:r0 + MXU_ROWS, c0:c0 + MXU_TILE] = val
        return sink

    def add_into_acc(r0, c0, val):
        acc[r0:r0 + MXU_ROWS, c0:c0 + MXU_TILE] += val

    def run(*programs):
        live = list(programs)
        while live:
            for g in list(live):
                if next(g, StopIteration) is StopIteration:
                    live.remove(g)

    @pl.when(j == 0)
    def _():
        acc[...] = jnp.zeros((D, tn), F32)
        g_odd[:, 0:tn] = jnp.zeros((eb, tn), BF)
        run(matmul_jobs(tiles(u0_ref, xt_ref, store_into(h_even))))

    def gate_build(h_cur, g_cur):
        for k0 in range(0, kb, GATE_KEYS):
            for tc in range(n_tc):
                for half in range(N_KEYS // GATE_ROWS):
                    lanes = slice(tc * LANES, (tc + 1) * LANES)
                    sub_rows = slice(half * GATE_ROWS, (half + 1) * GATE_ROWS)
                    w = [jnp.zeros((GATE_ROWS, LANES), F32) for _ in range(GATE_KEYS)]
                    for h in range(PEER_HEADS):
                        p2 = p2_ref[h, tc, sub_rows, :]
                        th = th_ref[h, :, lanes]
                        for a in range(GATE_KEYS):
                            prod = p1_ref[h, k0 + a:k0 + a + 1, lanes] * p2
                            w[a] = w[a] + jnp.where(prod >= th, prod, 0.0)
                        yield
                    for a in range(GATE_KEYS):
                        r0 = (k0 + a) * N_KEYS + half * GATE_ROWS
                        g_cur[r0:r0 + GATE_ROWS, lanes] = (
                            w[a] * _gelu(h_cur[r0:r0 + GATE_ROWS, lanes])).astype(BF)

    def block_step(h_cur, h_next, g_cur, g_prev):
        run(matmul_jobs(tiles(un_ref, xt_ref, store_into(h_next)) + tiles(vp_ref, g_prev, add_into_acc)),
            gate_build(h_cur, g_cur))

    @pl.when((j < nb) & (j % 2 == 0))
    def _():
        block_step(h_even, h_odd, g_even, g_odd)

    @pl.when((j < nb) & (j % 2 == 1))
    def _():
        block_step(h_odd, h_even, g_odd, g_even)

    @pl.when(j == nb)
    def _():
        g_last = g_even if (nb - 1) % 2 == 0 else g_odd

        run(matmul_jobs(tiles(vp_ref, g_last, add_into_acc)))
        y_ref[...] = _layer_norm(ALPHA * x_ref[...] + acc[...].T, g2_ref[...], b2_ref[...])


def _peer_dense(xt_bf, x2d, u_bf, vt_bf, p1, p2, th, g2, b2, tn):
    n = x2d.shape[0]
    eb = D
    nb = N_EXPERTS // eb
    n_tc = tn // LANES
    assert eb // N_KEYS == SUBLANES and (eb // N_KEYS) % GATE_KEYS == 0
    return pl.pallas_call(
        functools.partial(_dense_body, tn=tn, eb=eb, nb=nb),
        grid=(n // tn, nb + 1),
        in_specs=[pl.BlockSpec((D, tn), lambda i, j: (0, i)),
                  pl.BlockSpec((tn, D), lambda i, j: (i, 0)),
                  pl.BlockSpec((eb, D), lambda i, j: (0, 0)),
                  pl.BlockSpec((eb, D), lambda i, j: (jnp.minimum(j + 1, nb - 1), 0)),
                  pl.BlockSpec((D, eb), lambda i, j: (0, jnp.maximum(j - 1, 0))),
                  pl.BlockSpec((PEER_HEADS, eb // N_KEYS, tn),
                               lambda i, j: (0, jnp.minimum(j, nb - 1), i)),
                  pl.BlockSpec((PEER_HEADS, n_tc, N_KEYS, LANES), lambda i, j: (0, i, 0, 0)),
                  pl.BlockSpec((PEER_HEADS, 1, tn), lambda i, j: (0, 0, i)),
                  pl.BlockSpec((1, D), lambda i, j: (0, 0)),
                  pl.BlockSpec((1, D), lambda i, j: (0, 0))],
        out_specs=pl.BlockSpec((tn, D), lambda i, j: (i, 0)),
        out_shape=jax.ShapeDtypeStruct((n, D), F32),
        scratch_shapes=[pltpu.VMEM((D, tn), F32),
                        pltpu.VMEM((eb, tn + LANES), F32), pltpu.VMEM((eb, tn + LANES), F32),
                        pltpu.VMEM((eb, tn + LANES), BF), pltpu.VMEM((eb, tn + LANES), BF)],
        compiler_params=pltpu.CompilerParams(
            dimension_semantics=("arbitrary", "arbitrary"), vmem_limit_bytes=VMEM_LIMIT),
        name="peer_dense",
    )(xt_bf, x2d, u_bf, u_bf, vt_bf, p1, p2, th, g2, b2)


def _trunk_layer(x, past_a, past_b, h0, s0, lb, lw):
    bsz, t_len, _ = x.shape
    n = bsz * t_len
    z = _in_proj(x.reshape(n, D), lw["w_in"], lw["b_in"]).reshape(bsz, t_len, IN_COLS)
    x1, new_a, new_b, h_t, s_t = _mixer(x, z, past_a, past_b, h0, s0, lb, lw)
    x1 = x1.reshape(n, D)
    tn = min(n, 512)
    xt, p1, p2, th = _peer_scores(x1, lw["peer_wq"], lw["peer_keys"], tn)
    x2 = _peer_dense(xt, x1, lw["peer_u"], lw["peer_vt"], p1, p2, th, lw["ln2_g"], lw["ln2_b"], tn)
    return x2.reshape(bsz, t_len, D), new_a, new_b, h_t.reshape(bsz, D), s_t


def kernel(x_prompt, x_sample, state_conv_a, state_conv_b, state_lru, state_hgrn, w_in, b_in, conv_a_w,
           conv_b_w, conv_b_b, lru_wa, lru_ba, lru_wx, lru_bx, lru_lambda, hgrn_lb_logits, hgrn_norm_g,
           w_out_a, w_out_b, w_out_c, w_o, ln1_g, ln1_b, peer_wq, peer_keys, peer_u, peer_v, ln2_g,
           ln2_b):
    p = jax.nn.softmax(hgrn_lb_logits.astype(F32), axis=0)
    lbs = jnp.cumsum(p, axis=0) - p[0]
    bp = x_prompt.shape[0]
    xp, xs = x_prompt, x_sample
    outs_p, outs_s = [], []
    for l in range(DEPTH):
        row = lambda a: a[l].reshape(1, -1)
        lw = dict(
            w_in=w_in[l].astype(BF), b_in=row(b_in),
            conv_a_w=conv_a_w[l], conv_b_w=conv_b_w[l], conv_b_b=row(conv_b_b),
            lru_wa=lru_wa[l].astype(BF), lru_ba=row(lru_ba), lru_wx=lru_wx[l].astype(BF),
            lru_bx=row(lru_bx), lru_lambda=row(lru_lambda), hgrn_norm_g=row(hgrn_norm_g),
            w_out_a=w_out_a[l].astype(BF), w_out_b=w_out_b[l].astype(BF),
            w_out_c=w_out_c[l].astype(BF), w_o=w_o[l].astype(BF),
            ln1_g=row(ln1_g), ln1_b=row(ln1_b),
            peer_wq=peer_wq[l].astype(BF), peer_keys=peer_keys[l].astype(BF),
            peer_u=peer_u[l].astype(BF), peer_vt=peer_v[l].T.astype(BF),
            ln2_g=row(ln2_g), ln2_b=row(ln2_b))
        lb = lbs[l].reshape(1, D)
        xp, pa, pb, ph, ps = _trunk_layer(
            xp, jnp.zeros((bp, CONV_A_W - 1, D), F32), jnp.zeros((bp, CONV_B_W - 1, D), F32),
            jnp.zeros((bp, 1, D), F32), jnp.zeros((bp, HEADS, HD, HD), F32), lb, lw)
        xs, sa, sb, sh, ss = _trunk_layer(
            xs, state_conv_a[l], state_conv_b[l], state_lru[l][:, None, :], state_hgrn[l], lb, lw)
        outs_p.append((pa, pb, ph, ps))
        outs_s.append((sa, sb, sh, ss))
    stack = lambda outs, k: jnp.stack([o[k] for o in outs])
    return (xp, xs,
            stack(outs_p, 0), stack(outs_p, 1), stack(outs_p, 2), stack(outs_p, 3),
            stack(outs_s, 0), stack(outs_s, 1), stack(outs_s, 2), stack(outs_s, 3))
```

```python
import functools

import jax
import jax.numpy as jnp
from jax import lax
from jax.experimental import pallas as pl
from jax.experimental.pallas import tpu as pltpu

D = 1024
DEPTH = 2
CONV_A_W = 3
CONV_B_W = 4
LRU_BLOCKS = 8
LRU_BW = D // LRU_BLOCKS
LRU_C = 8.0
HEADS = 8
HD = D // HEADS
PEER_HEADS = 8
PEER_DH = 128
N_KEYS = 128
N_EXPERTS = N_KEYS * N_KEYS
TOPK = 16
ALPHA = (2.0 * DEPTH) ** 0.25
LN_EPS = 1e-5
RMS_EPS = 1e-6
IN_COLS = 12 * D
SUB = 16
INTRA_CAP = 60.0
LANES = 128
SUBLANES = 8
VMEM_LIMIT = 56 * 1024 * 1024
GATE_KEYS = 4
GATE_ROWS = 64
SCORE_TOKENS = 256
MXU_TILE = 256
MXU_ROWS = 512
MXU_STEP = 32
GATE_YIELD = 2

OFF_AB, OFF_AC, OFF_AX, OFF_BX, OFF_BG, OFF_CQ, OFF_CF, OFF_CI, OFF_CG, OFF_GA, OFF_GB, OFF_GC = (
    i * D for i in range(12))

BF = jnp.bfloat16
F32 = jnp.float32


def _sigmoid(x):
    return 0.5 * jnp.tanh(0.5 * x) + 0.5


def _gelu(x):
    return 0.5 * x * (1.0 + lax.erf(x * (2.0 ** -0.5)))


def _layer_norm(x, g, b):
    mu = jnp.mean(x, axis=-1, keepdims=True)
    xc = x - mu
    var = jnp.mean(xc * xc, axis=-1, keepdims=True)
    return xc * lax.rsqrt(var + LN_EPS) * g + b


def _nt_dot(a, b):
    return lax.dot_general(a, b, (((1,), (1,)), ((), ())), preferred_element_type=F32)


def _tn_dot(a, b):
    return lax.dot_general(a, b, (((0,), (0,)), ((), ())), preferred_element_type=F32)


def _shift_rows(x, d, fill, rows):
    return jnp.where(rows >= d, pltpu.roll(x, d, 0), fill)


def _scan_rows(a, u, carry, rows):
    in_group = rows & (SUBLANES - 1)
    d = 1
    while d < SUBLANES:
        if a is None:
            u = u + _shift_rows(u, d, 0.0, in_group)
        else:
            u = a * _shift_rows(u, d, 0.0, in_group) + u
            a = a * _shift_rows(a, d, 1.0, in_group)
        d *= 2
    out = []
    for g in range(u.shape[0] // SUBLANES):
        rs = slice(g * SUBLANES, (g + 1) * SUBLANES)
        h_g = u[rs, :] + carry if a is None else u[rs, :] + a[rs, :] * carry
        out.append(h_g)
        carry = h_g[SUBLANES - 1:SUBLANES, :]
    return jnp.concatenate(out, axis=0)


def _inproj_body(x_ref, w_ref, b_ref, o_ref):
    x = x_ref[...].astype(BF)
    o_ref[...] = jnp.dot(x, w_ref[...], preferred_element_type=F32) + b_ref[...]


def _in_proj(x2d, w_bf, b_row):
    n = x2d.shape[0]
    tm = min(n, 1024)
    tn = 2048
    return pl.pallas_call(
        _inproj_body,
        grid=(n // tm, IN_COLS // tn),
        in_specs=[pl.BlockSpec((tm, D), lambda i, j: (i, 0)),
                  pl.BlockSpec((D, tn), lambda i, j: (0, j)),
                  pl.BlockSpec((1, tn), lambda i, j: (0, j))],
        out_specs=pl.BlockSpec((tm, tn), lambda i, j: (i, j)),
        out_shape=jax.ShapeDtypeStruct((n, IN_COLS), F32),
        compiler_params=pltpu.CompilerParams(
            dimension_semantics=("arbitrary", "arbitrary"), vmem_limit_bytes=VMEM_LIMIT),
        name="in_proj",
    )(x2d, w_bf, b_row)


def _mixer_body(x_ref, z_ref, pa_ref, pb_ref, h0_ref, s0_ref, lb_ref, cwa_ref, cwb_ref, cbb_ref,
                wa_ref, ba_ref, wx_ref, bx_ref, lam_ref, ng_ref, woa_ref, wob_ref, woc_ref, wo_ref,
                g1_ref, b1_ref,
                x1_ref, na_ref, nb_ref, nh_ref, ns_ref,
                ua, ub, hc, st, qs, ks, vs, bs, att, oc, kf, bf, qsil, qe, kd, att_all, *, chunk):
    c = pl.program_id(1)
    last = pl.num_programs(1) - 1
    nsub = chunk // SUB
    rows = lax.broadcasted_iota(jnp.int32, (chunk, D), 0)

    rows8 = lax.broadcasted_iota(jnp.int32, (SUBLANES, D), 0)

    def delayed(x, d, prev):
        out = []
        below = pltpu.roll(prev, d, 0)
        for g in range(chunk // SUBLANES):
            cur = pltpu.roll(x[g * SUBLANES:(g + 1) * SUBLANES, :], d, 0)
            out.append(jnp.where(rows8 < d, below, cur))
            below = cur
        return jnp.concatenate(out, axis=0)

    @pl.when(c == 0)
    def _():
        ua[...] = jnp.zeros((SUBLANES, D), F32)
        ub[...] = jnp.zeros((SUBLANES, D), F32)
        ua[SUBLANES - 2:SUBLANES, :] = pa_ref[...]
        ub[SUBLANES - 3:SUBLANES, :] = pb_ref[...]
        hc[0:1, :] = h0_ref[...]
        for h in range(HEADS):
            st[h] = s0_ref[h].T
        att_all[...] = jnp.zeros((HEADS, chunk, chunk), F32)

    u = z_ref[:, OFF_AC:OFF_AC + D] * z_ref[:, OFF_AX:OFF_AX + D]
    prev_a = ua[...]
    conv_a = (cwa_ref[0:1, :] * delayed(u, 2, prev_a) + cwa_ref[1:2, :] * delayed(u, 1, prev_a)
              + cwa_ref[2:3, :] * u)
    na_ref[...] = u[chunk - (CONV_A_W - 1):chunk, :]
    ua[...] = u[chunk - SUBLANES:chunk, :]
    y_a = jnp.dot((z_ref[:, OFF_AB:OFF_AB + D] * conv_a).astype(BF), woa_ref[...],
                  preferred_element_type=F32)
    merged = _sigmoid(z_ref[:, OFF_GA:OFF_GA + D]) * y_a

    xb = z_ref[:, OFF_BX:OFF_BX + D]
    prev_b = ub[...]
    xl = (cwb_ref[0:1, :] * delayed(xb, 3, prev_b) + cwb_ref[1:2, :] * delayed(xb, 2, prev_b)
          + cwb_ref[2:3, :] * delayed(xb, 1, prev_b) + cwb_ref[3:4, :] * xb) + cbb_ref[...]
    nb_ref[...] = xb[chunk - (CONV_B_W - 1):chunk, :]
    ub[...] = xb[chunk - SUBLANES:chunk, :]

    xl_bf = xl.astype(BF)
    ra = jnp.concatenate(
        [jnp.dot(xl_bf[:, n * LRU_BW:(n + 1) * LRU_BW], wa_ref[n], preferred_element_type=F32)
         for n in range(LRU_BLOCKS)], axis=1)
    rx = jnp.concatenate(
        [jnp.dot(xl_bf[:, n * LRU_BW:(n + 1) * LRU_BW], wx_ref[n], preferred_element_type=F32)
         for n in range(LRU_BLOCKS)], axis=1)
    r = _sigmoid(ra + ba_ref[...])
    gi = _sigmoid(rx + bx_ref[...])
    nlam = -lam_ref[...]
    softplus = jnp.maximum(nlam, 0.0) + jnp.log1p(jnp.exp(-jnp.abs(nlam)))
    log_a = (-LRU_C) * r * softplus
    a = jnp.exp(log_a)
    uu = jnp.sqrt(jnp.tanh(-log_a) * (a * a + 1.0)) * (gi * xl)
    hh = _scan_rows(a, uu, hc[0:1, :], rows)
    h_last = hh[chunk - 1:chunk, :]
    hc[0:1, :] = h_last
    nh_ref[...] = h_last
    y_b = jnp.dot((_gelu(z_ref[:, OFF_BG:OFF_BG + D]) * hh).astype(BF), wob_ref[...],
                  preferred_element_type=F32)
    merged = merged + _sigmoid(z_ref[:, OFF_GB:OFF_GB + D]) * y_b

    cols_a = lax.broadcasted_iota(jnp.int32, (SUB, chunk), 1)
    sub_rows = lax.broadcasted_iota(jnp.int32, (SUB, 1), 0)
    f_all = lb_ref[...] + (1.0 - lb_ref[...]) * _sigmoid(z_ref[:, OFF_CF:OFF_CF + D])
    kf[...] = 1.0 - f_all
    bf[...] = _scan_rows(None, jnp.log(f_all), jnp.zeros((1, D), F32), rows)
    intra_bound = jnp.max(functools.reduce(jnp.maximum, [
        (bf[i * SUB - 1:i * SUB, :] if i else 0.0) - bf[(i + 1) * SUB - 1:(i + 1) * SUB, :]
        for i in range(nsub)]))

    def head_step(h, carry):
        off = pl.multiple_of(h * HD, HD)
        k = kf[:, pl.ds(off, HD)]
        b = bf[:, pl.ds(off, HD)]
        cq = z_ref[:, pl.ds(OFF_CQ + off, HD)]
        q = cq * _sigmoid(cq)
        v = z_ref[:, pl.ds(OFF_CI + off, HD)]
        qs[...] = q
        ks[...] = k
        vs[...] = v
        bs[...] = b
        st_h = st[h]
        b_last = b[chunk - 1:chunk, :]
        o = _nt_dot((q * jnp.exp(b)).astype(BF), st_h.astype(BF))
        kdec = k * jnp.exp(b_last - b)
        st[h] = st_h * jnp.exp(b_last) + _tn_dot(v.astype(BF), kdec.astype(BF))

        def sub_start(i):
            return bs[i * SUB - 1:i * SUB, :] if i else jnp.zeros((1, HD), F32)

        def intra_direct():
            att[0:SUB, :] = jnp.zeros((SUB, chunk), F32)
            for i in range(1, nsub):
                lo, hi = i * SUB, (i + 1) * SUB
                q_i = qs[lo:hi, :] * jnp.exp(bs[lo:hi, :] - sub_start(i))
                k_i = k * jnp.exp(jnp.minimum(sub_start(i) - b, 0.0))
                a_i = _nt_dot(q_i.astype(BF), k_i.astype(BF))
                att[lo:hi, :] = jnp.where(cols_a < lo, a_i, 0.0)
            o_off = jnp.dot(att[...].astype(BF), v.astype(BF), preferred_element_type=F32)
            o_parts = []
            for i in range(nsub):
                q_i = qs[i * SUB:(i + 1) * SUB, :]
                b_i = bs[i * SUB:(i + 1) * SUB, :]
                o_i = jnp.zeros((SUB, HD), F32)
                for s in range(SUB):
                    r0 = i * SUB + s
                    p = q_i * ks[r0:r0 + 1, :] * jnp.exp(jnp.minimum(b_i - bs[r0:r0 + 1, :], 0.0))
                    a_col = jnp.sum(p, axis=-1, keepdims=True)
                    a_col = jnp.where(sub_rows >= s, a_col, 0.0)
                    o_i = o_i + a_col * vs[r0:r0 + 1, :]
                o_parts.append(o_i)
            return o_off + jnp.concatenate(o_parts, axis=0)

        o = o + intra_direct()
        oc[:, pl.ds(off, HD)] = o * lax.rsqrt(jnp.mean(o * o, axis=-1, keepdims=True) + RMS_EPS) * ng_ref[...]
        return carry

    def hgrn_factored():
        cq = z_ref[:, OFF_CQ:OFF_CQ + D]
        qsil[...] = cq * _sigmoid(cq)
        b_last = bf[chunk - 1:chunk, :]
        qe[...] = (qsil[...] * jnp.exp(bf[...])).astype(BF)
        kd[...] = (kf[...] * jnp.exp(b_last - bf[...])).astype(BF)
        for i in range(nsub):
            lo, hi = i * SUB, (i + 1) * SUB
            start = bf[lo - 1:lo, :] if i else 0.0
            q_i = (qsil[lo:hi, :] * jnp.exp(bf[lo:hi, :] - start)).astype(BF)
            k_i = (kf[0:hi, :] * jnp.exp(jnp.minimum(start - bf[0:hi, :], INTRA_CAP))).astype(BF)
            causal = (lax.broadcasted_iota(jnp.int32, (SUB, hi), 1)
                      <= lax.broadcasted_iota(jnp.int32, (SUB, hi), 0) + lo)
            for h in range(HEADS):
                hs = slice(h * HD, (h + 1) * HD)
                att_all[h, lo:hi, 0:hi] = jnp.where(causal, _nt_dot(q_i[:, hs], k_i[:, hs]), 0.0)
        for h in range(HEADS):
            hs = slice(h * HD, (h + 1) * HD)
            v_h = z_ref[:, OFF_CI + h * HD:OFF_CI + (h + 1) * HD].astype(BF)
            st_h = st[h]
            o = (_nt_dot(qe[:, hs], st_h.astype(BF))
                 + jnp.dot(att_all[h].astype(BF), v_h, preferred_element_type=F32))
            st[h] = st_h * jnp.exp(b_last[:, hs]) + _tn_dot(v_h, kd[:, hs])
            oc[:, hs] = o * lax.rsqrt(jnp.mean(o * o, axis=-1, keepdims=True) + RMS_EPS) * ng_ref[...]

    def hgrn_direct():
        lax.fori_loop(0, HEADS, head_step, 0)

    lax.cond(intra_bound <= INTRA_CAP, hgrn_factored, hgrn_direct)

    @pl.when(c == last)
    def _():
        for h in range(HEADS):
            ns_ref[h] = st[h].T

    cg = z_ref[:, OFF_CG:OFF_CG + D]
    y_c = jnp.dot((oc[...] * (cg * _sigmoid(cg))).astype(BF), woc_ref[...], preferred_element_type=F32)
    merged = merged + _sigmoid(z_ref[:, OFF_GC:OFF_GC + D]) * y_c
    res = ALPHA * x_ref[...] + jnp.dot(merged.astype(BF), wo_ref[...], preferred_element_type=F32)
    x1_ref[...] = _layer_norm(res, g1_ref[...], b1_ref[...])


def _mixer(x, z, past_a, past_b, h0, s0, lb, lw):
    bsz, t_len, _ = x.shape
    chunk = min(t_len, 128)
    assert t_len % chunk == 0 and chunk % SUB == 0
    nc = t_len // chunk
    full2 = lambda shape: pl.BlockSpec(shape, lambda b, c: (0, 0))
    full3 = lambda shape: pl.BlockSpec(shape, lambda b, c: (0, 0, 0))
    in_specs = [
        pl.BlockSpec((None, chunk, D), lambda b, c: (b, c, 0)),
        pl.BlockSpec((None, chunk, IN_COLS), lambda b, c: (b, c, 0)),
        pl.BlockSpec((None, CONV_A_W - 1, D), lambda b, c: (b, 0, 0)),
        pl.BlockSpec((None, CONV_B_W - 1, D), lambda b, c: (b, 0, 0)),
        pl.BlockSpec((None, 1, D), lambda b, c: (b, 0, 0)),
        pl.BlockSpec((None, HEADS, HD, HD), lambda b, c: (b, 0, 0, 0)),
        full2((1, D)),
        full2((CONV_A_W, D)), full2((CONV_B_W, D)), full2((1, D)),
        full3((LRU_BLOCKS, LRU_BW, LRU_BW)), full2((1, D)),
        full3((LRU_BLOCKS, LRU_BW, LRU_BW)), full2((1, D)),
        full2((1, D)), full2((1, HD)),
        full2((D, D)), full2((D, D)), full2((D, D)), full2((D, D)),
        full2((1, D)), full2((1, D)),
    ]
    out_specs = [
        pl.BlockSpec((None, chunk, D), lambda b, c: (b, c, 0)),
        pl.BlockSpec((None, CONV_A_W - 1, D), lambda b, c: (b, 0, 0)),
        pl.BlockSpec((None, CONV_B_W - 1, D), lambda b, c: (b, 0, 0)),
        pl.BlockSpec((None, 1, D), lambda b, c: (b, 0, 0)),
        pl.BlockSpec((None, HEADS, HD, HD), lambda b, c: (b, 0, 0, 0)),
    ]
    out_shape = [
        jax.ShapeDtypeStruct((bsz, t_len, D), F32),
        jax.ShapeDtypeStruct((bsz, CONV_A_W - 1, D), F32),
        jax.ShapeDtypeStruct((bsz, CONV_B_W - 1, D), F32),
        jax.ShapeDtypeStruct((bsz, 1, D), F32),
        jax.ShapeDtypeStruct((bsz, HEADS, HD, HD), F32),
    ]
    scratch = [
        pltpu.VMEM((SUBLANES, D), F32),
        pltpu.VMEM((SUBLANES, D), F32),
        pltpu.VMEM((SUBLANES, D), F32),
        pltpu.VMEM((HEADS, HD, HD), F32),
        pltpu.VMEM((chunk, HD), F32),
        pltpu.VMEM((chunk, HD), F32),
        pltpu.VMEM((chunk, HD), F32),
        pltpu.VMEM((chunk, HD), F32),
        pltpu.VMEM((chunk, chunk), F32),
        pltpu.VMEM((chunk, D), F32),
        pltpu.VMEM((chunk, D), F32),
        pltpu.VMEM((chunk, D), F32),
        pltpu.VMEM((chunk, D), F32),
        pltpu.VMEM((chunk, D), BF),
        pltpu.VMEM((chunk, D), BF),
        pltpu.VMEM((HEADS, chunk, chunk), F32),
    ]
    return pl.pallas_call(
        functools.partial(_mixer_body, chunk=chunk),
        grid=(bsz, nc),
        in_specs=in_specs, out_specs=out_specs, out_shape=out_shape, scratch_shapes=scratch,
        compiler_params=pltpu.CompilerParams(
            dimension_semantics=("arbitrary", "arbitrary"), vmem_limit_bytes=VMEM_LIMIT),
        name="mixer",
    )(x, z, past_a, past_b, h0, s0, lb, lw["conv_a_w"], lw["conv_b_w"], lw["conv_b_b"],
      lw["lru_wa"], lw["lru_ba"], lw["lru_wx"], lw["lru_bx"], lw["lru_lambda"], lw["hgrn_norm_g"],
      lw["w_out_a"], lw["w_out_b"], lw["w_out_c"], lw["w_o"], lw["ln1_g"], lw["ln1_b"])


def _sort_network(n):
    pairs = []

    def merge(lo, m, r):
        step = 2 * r
        if step < m:
            merge(lo, m, step)
            merge(lo + r, m, step)
            pairs.extend((i, i + r) for i in range(lo + r, lo + m - r, step))
        else:
            pairs.append((lo, lo + r))

    def sort(lo, m):
        if m > 1:
            sort(lo, m // 2)
            sort(lo + m // 2, m // 2)
            merge(lo, m, 1)

    sort(0, n)
    return pairs


_SORT16 = _sort_network(TOPK)


def _cmp_exchange(v, i, j):
    if v[j] is None:
        return
    if v[i] is None:
        v[i], v[j] = v[j], None
        return
    v[i], v[j] = jnp.maximum(v[i], v[j]), jnp.minimum(v[i], v[j])


def _sort_desc(v):
    v = list(v) + [None] * (TOPK - len(v))
    for i, j in _SORT16:
        _cmp_exchange(v, i, j)
    return v


def _bitonic_to_desc(v):
    v = list(v)
    d = TOPK // 2
    while d >= 1:
        for i in range(TOPK):
            if i & d == 0:
                _cmp_exchange(v, i, i + d)
        d //= 2
    return v


def _merge_rolled(v, shift):
    w = [None if x is None else pltpu.roll(x, shift, 0) for x in v]
    out = []
    for i in range(TOPK):
        a, b = v[i], w[TOPK - 1 - i]
        out.append(b if a is None else a if b is None else jnp.maximum(a, b))
    return out


def _top16_of_rows(vregs):
    v = _sort_desc(vregs)
    for shift in (4, 2, 1):
        v = _bitonic_to_desc(_merge_rolled(v, shift))
    return v


def _kth16_of_rows(vregs):
    v = _sort_desc(vregs)
    for shift in (4, 2):
        v = _bitonic_to_desc(_merge_rolled(v, shift))
    return functools.reduce(jnp.minimum, _merge_rolled(v, 1))


def _pack_rows(vs, sub):
    out = vs[0]
    for j in range(1, SUBLANES):
        out = jnp.where(sub == j, vs[j], out)
    return out


def _scores_body(x_ref, wq_ref, keys_ref, xt_ref, p1_ref, p2_ref, th_ref, qbuf, *, tn):
    xt_ref[...] = x_ref[...].T.astype(BF)
    qbuf[...] = jnp.dot(x_ref[...].astype(BF), wq_ref[...], preferred_element_type=F32).astype(BF)
    tw = SCORE_TOKENS
    sub = lax.broadcasted_iota(jnp.int32, (SUBLANES, tw), 0)
    n_tc = tn // tw

    def step(idx, carry):
        h = idx // n_tc
        tc = idx % n_tc
        t0 = pl.multiple_of(tc * tw, tw)
        c0 = pl.multiple_of(h * (2 * PEER_DH), 2 * PEER_DH)
        s1 = _nt_dot(keys_ref[h, 0], qbuf[pl.ds(t0, tw), pl.ds(c0, PEER_DH)])
        s2 = _nt_dot(keys_ref[h, 1], qbuf[pl.ds(t0, tw), pl.ds(c0 + PEER_DH, PEER_DH)])
        split = lambda s: [s[r * SUBLANES:(r + 1) * SUBLANES, :] for r in range(N_KEYS // SUBLANES)]
        a = _top16_of_rows(split(s1))
        b = _top16_of_rows(split(s2))
        pa = [jnp.exp(x - a[0]) for x in a]
        pb = [jnp.exp(x - b[0]) for x in b]
        pb_lo, pb_hi = _pack_rows(pb[:SUBLANES], sub), _pack_rows(pb[SUBLANES:], sub)
        pa_hi = _pack_rows(pa[SUBLANES:], sub)
        lhs = [pa[0], pa[0]] + pa[1:SUBLANES] + [pa_hi]
        rhs = [pb_lo, pb_hi] + [pb_lo] * (SUBLANES - 1) + [pb[0]]
        t = [x * y for x, y in zip(lhs, rhs)]
        t16 = _kth16_of_rows(t)
        keep = [x >= t16 for x in t]
        z_sum = jnp.sum(sum(jnp.where(m, x, 0.0) for m, x in zip(keep, t)), axis=0, keepdims=True)
        rz = 1.0 / z_sum
        gates = [(x * rz) * y for x, y in zip(lhs, rhs)]
        th = jnp.min(functools.reduce(jnp.minimum, [jnp.where(m, g, jnp.inf) for m, g in zip(keep, gates)]),
                     axis=0, keepdims=True)
        p1_ref[h, :, pl.ds(t0, tw)] = jnp.exp(s1 - a[0][0:1]) * rz
        p2 = jnp.exp(s2 - b[0][0:1])
        for part in range(tw // LANES):
            p2_ref[h, tc * (tw // LANES) + part] = p2[:, part * LANES:(part + 1) * LANES]
        th_ref[h, :, pl.ds(t0, tw)] = th
        return carry

    lax.fori_loop(0, PEER_HEADS * n_tc, step, 0)


def _peer_scores(x2d, wq_bf, keys_bf, tn):
    n = x2d.shape[0]
    n_tc = tn // LANES
    return pl.pallas_call(
        functools.partial(_scores_body, tn=tn),
        grid=(n // tn,),
        in_specs=[pl.BlockSpec((tn, D), lambda i: (i, 0)),
                  pl.BlockSpec((D, 2 * PEER_HEADS * PEER_DH), lambda i: (0, 0)),
                  pl.BlockSpec((PEER_HEADS, 2, N_KEYS, PEER_DH), lambda i: (0, 0, 0, 0))],
        out_specs=[pl.BlockSpec((D, tn), lambda i: (0, i)),
                   pl.BlockSpec((PEER_HEADS, N_KEYS, tn), lambda i: (0, 0, i)),
                   pl.BlockSpec((PEER_HEADS, n_tc, N_KEYS, LANES), lambda i: (0, i, 0, 0)),
                   pl.BlockSpec((PEER_HEADS, 1, tn), lambda i: (0, 0, i))],
        out_shape=[jax.ShapeDtypeStruct((D, n), BF),
                   jax.ShapeDtypeStruct((PEER_HEADS, N_KEYS, n), F32),
                   jax.ShapeDtypeStruct((PEER_HEADS, n // LANES, N_KEYS, LANES), F32),
                   jax.ShapeDtypeStruct((PEER_HEADS, 1, n), F32)],
        scratch_shapes=[pltpu.VMEM((tn, 2 * PEER_HEADS * PEER_DH), BF)],
        compiler_params=pltpu.CompilerParams(
            dimension_semantics=("arbitrary",), vmem_limit_bytes=VMEM_LIMIT),
        name="peer_scores",
    )(x2d, wq_bf, keys_bf)


def _dense_body(xt_ref, x_ref, u0_ref, un_ref, vp_ref, p1_ref, p2_ref, th_ref, g2_ref, b2_ref,
                y_ref, acc, h_even, h_odd, g_even, g_odd, *, tn, eb, nb):
    j = pl.program_id(1)
    kb = eb // N_KEYS
    n_tc = tn // LANES

    def matmul_jobs(jobs):
        assert len(jobs) % 2 == 0
        for p in range(0, len(jobs), 2):
            pair = jobs[p:p + 2]
            base = (p // 2 % 2) * (MXU_ROWS // 4)
            for k in range(D // MXU_TILE):
                ks = slice(k * MXU_TILE, (k + 1) * MXU_TILE)
                for q, (lhs, rhs, r0, c0, _) in enumerate(pair):
                    pltpu.matmul_push_rhs(rhs[ks, c0:c0 + MXU_TILE], staging_register=k % 2, mxu_index=q)
                shared_lhs = pair[0][0] is pair[1][0] and pair[0][2] == pair[1][2]
                for m in range(MXU_ROWS // MXU_STEP):
                    piece = None
                    for q, (lhs, rhs, r0, c0, _) in enumerate(pair):
                        if piece is None or not shared_lhs:
                            piece = lhs[r0 + m * MXU_STEP:r0 + (m + 1) * MXU_STEP, ks]
                        pltpu.matmul_acc_lhs(base + m * (MXU_STEP // 4), piece, q,
                                             load_staged_rhs=(k % 2) if m == 0 else None)
                    yield
            for q, (lhs, rhs, r0, c0, sink) in enumerate(pair):
                sink(r0, c0, pltpu.matmul_pop(base, (MXU_ROWS, MXU_TILE), F32, q))

    def tiles(lhs, rhs, sink):
        return [(lhs, rhs, r0, c0, sink) for r0 in range(0, eb, MXU_ROWS) for c0 in range(0, tn, MXU_TILE)]

    def store_into(dst):
        def sink(r0, c0, val):
            dst[r0:r0 + MXU_ROWS, c0:c0 + MXU_TILE] = val
        return sink

    def add_into_acc(r0, c0, val):
        acc[r0:r0 + MXU_ROWS, c0:c0 + MXU_TILE] += val

    def run(*programs):
        live = list(programs)
        while live:
            for g in list(live):
                if next(g, StopIteration) is StopIteration:
                    live.remove(g)

    @pl.when(j == 0)
    def _():
        acc[...] = jnp.zeros((D, tn), F32)
        g_odd[:, 0:tn] = jnp.zeros((eb, tn), BF)
        run(matmul_jobs(tiles(u0_ref, xt_ref, store_into(h_even))))

    def gate_build(h_cur, g_cur):
        for k0 in range(0, kb, GATE_KEYS):
            for tc in range(n_tc):
                for half in range(N_KEYS // GATE_ROWS):
                    lanes = slice(tc * LANES, (tc + 1) * LANES)
                    sub_rows = slice(half * GATE_ROWS, (half + 1) * GATE_ROWS)
                    w = [None] * GATE_KEYS
                    for h in range(PEER_HEADS):
                        p2 = p2_ref[h, tc, sub_rows, :]
                        th = th_ref[h, :, lanes]
                        for a in range(GATE_KEYS):
                            prod = p1_ref[h, k0 + a:k0 + a + 1, lanes] * p2
                            kept = jnp.where(prod >= th, prod, 0.0)
                            w[a] = kept if h == 0 else w[a] + kept
                            if a % GATE_YIELD == GATE_YIELD - 1:
                                yield
                    for a in range(GATE_KEYS):
                        r0 = (k0 + a) * N_KEYS + half * GATE_ROWS
                        g_cur[r0:r0 + GATE_ROWS, lanes] = (
                            w[a] * _gelu(h_cur[r0:r0 + GATE_ROWS, lanes])).astype(BF)

    def block_step(h_cur, h_next, g_cur, g_prev):
        run(matmul_jobs(tiles(un_ref, xt_ref, store_into(h_next)) + tiles(vp_ref, g_prev, add_into_acc)),
            gate_build(h_cur, g_cur))

    @pl.when((j < nb) & (j % 2 == 0))
    def _():
        block_step(h_even, h_odd, g_even, g_odd)

    @pl.when((j < nb) & (j % 2 == 1))
    def _():
        block_step(h_odd, h_even, g_odd, g_even)

    @pl.when(j == nb)
    def _():
        g_last = g_even if (nb - 1) % 2 == 0 else g_odd

        run(matmul_jobs(tiles(vp_ref, g_last, add_into_acc)))
        y_ref[...] = _layer_norm(ALPHA * x_ref[...] + acc[...].T, g2_ref[...], b2_ref[...])


def _peer_dense(xt_bf, x2d, u_bf, vt_bf, p1, p2, th, g2, b2, tn):
    n = x2d.shape[0]
    eb = D
    nb = N_EXPERTS // eb
    n_tc = tn // LANES
    assert eb // N_KEYS == SUBLANES and (eb // N_KEYS) % GATE_KEYS == 0
    return pl.pallas_call(
        functools.partial(_dense_body, tn=tn, eb=eb, nb=nb),
        grid=(n // tn, nb + 1),
        in_specs=[pl.BlockSpec((D, tn), lambda i, j: (0, i)),
                  pl.BlockSpec((tn, D), lambda i, j: (i, 0)),
                  pl.BlockSpec((eb, D), lambda i, j: (0, 0)),
                  pl.BlockSpec((eb, D), lambda i, j: (jnp.minimum(j + 1, nb - 1), 0)),
                  pl.BlockSpec((None, D, eb), lambda i, j: (jnp.maximum(j - 1, 0), 0, 0)),
                  pl.BlockSpec((PEER_HEADS, eb // N_KEYS, tn),
                               lambda i, j: (0, jnp.minimum(j, nb - 1), i)),
                  pl.BlockSpec((PEER_HEADS, n_tc, N_KEYS, LANES), lambda i, j: (0, i, 0, 0)),
                  pl.BlockSpec((PEER_HEADS, 1, tn), lambda i, j: (0, 0, i)),
                  pl.BlockSpec((1, D), lambda i, j: (0, 0)),
                  pl.BlockSpec((1, D), lambda i, j: (0, 0))],
        out_specs=pl.BlockSpec((tn, D), lambda i, j: (i, 0)),
        out_shape=jax.ShapeDtypeStruct((n, D), F32),
        scratch_shapes=[pltpu.VMEM((D, tn), F32),
                        pltpu.VMEM((eb, tn + LANES), F32), pltpu.VMEM((eb, tn + LANES), F32),
                        pltpu.VMEM((eb, tn + LANES), BF), pltpu.VMEM((eb, tn + LANES), BF)],
        compiler_params=pltpu.CompilerParams(
            dimension_semantics=("arbitrary", "arbitrary"), vmem_limit_bytes=VMEM_LIMIT),
        name="peer_dense",
    )(xt_bf, x2d, u_bf, u_bf, vt_bf, p1, p2, th, g2, b2)


def _trunk_layer(x, past_a, past_b, h0, s0, lb, lw):
    bsz, t_len, _ = x.shape
    n = bsz * t_len
    z = _in_proj(x.reshape(n, D), lw["w_in"], lw["b_in"]).reshape(bsz, t_len, IN_COLS)
    x1, new_a, new_b, h_t, s_t = _mixer(x, z, past_a, past_b, h0, s0, lb, lw)
    x1 = x1.reshape(n, D)
    tn = min(n, 512)
    xt, p1, p2, th = _peer_scores(x1, lw["peer_wq"], lw["peer_keys"], tn)
    x2 = _peer_dense(xt, x1, lw["peer_u"], lw["peer_vt"], p1, p2, th, lw["ln2_g"], lw["ln2_b"], tn)
    return x2.reshape(bsz, t_len, D), new_a, new_b, h_t.reshape(bsz, D), s_t


def kernel(x_prompt, x_sample, state_conv_a, state_conv_b, state_lru, state_hgrn, w_in, b_in, conv_a_w,
           conv_b_w, conv_b_b, lru_wa, lru_ba, lru_wx, lru_bx, lru_lambda, hgrn_lb_logits, hgrn_norm_g,
           w_out_a, w_out_b, w_out_c, w_o, ln1_g, ln1_b, peer_wq, peer_keys, peer_u, peer_v, ln2_g,
           ln2_b):
    p = jax.nn.softmax(hgrn_lb_logits.astype(F32), axis=0)
    lbs = jnp.cumsum(p, axis=0) - p[0]
    bp = x_prompt.shape[0]
    xp, xs = x_prompt, x_sample
    outs_p, outs_s = [], []
    for l in range(DEPTH):
        row = lambda a: a[l].reshape(1, -1)
        lw = dict(
            w_in=w_in[l].astype(BF), b_in=row(b_in),
            conv_a_w=conv_a_w[l], conv_b_w=conv_b_w[l], conv_b_b=row(conv_b_b),
            lru_wa=lru_wa[l].astype(BF), lru_ba=row(lru_ba), lru_wx=lru_wx[l].astype(BF),
            lru_bx=row(lru_bx), lru_lambda=row(lru_lambda), hgrn_norm_g=row(hgrn_norm_g),
            w_out_a=w_out_a[l].astype(BF), w_out_b=w_out_b[l].astype(BF),
            w_out_c=w_out_c[l].astype(BF), w_o=w_o[l].astype(BF),
            ln1_g=row(ln1_g), ln1_b=row(ln1_b),
            peer_wq=peer_wq[l].astype(BF), peer_keys=peer_keys[l].astype(BF),
            peer_u=peer_u[l].astype(BF),
            peer_vt=peer_v[l].reshape(N_EXPERTS // D, D, D).transpose(0, 2, 1).astype(BF),
            ln2_g=row(ln2_g), ln2_b=row(ln2_b))
        lb = lbs[l].reshape(1, D)
        xp, pa, pb, ph, ps = _trunk_layer(
            xp, jnp.zeros((bp, CONV_A_W - 1, D), F32), jnp.zeros((bp, CONV_B_W - 1, D), F32),
            jnp.zeros((bp, 1, D), F32), jnp.zeros((bp, HEADS, HD, HD), F32), lb, lw)
        xs, sa, sb, sh, ss = _trunk_layer(
            xs, state_conv_a[l], state_conv_b[l], state_lru[l][:, None, :], state_hgrn[l], lb, lw)
        outs_p.append((pa, pb, ph, ps))
        outs_s.append((sa, sb, sh, ss))
    stack = lambda outs, k: jnp.stack([o[k] for o in outs])
    return (xp, xs,
            stack(outs_p, 0), stack(outs_p, 1), stack(outs_p, 2), stack(outs_p, 3),
            stack(outs_s, 0), stack(outs_s, 1), stack(outs_s, 2), stack(outs_s, 3))
```

```python
import functools

import jax
import jax.numpy as jnp
from jax import lax
from jax.experimental import pallas as pl
from jax.experimental.pallas import tpu as pltpu

D = 1024
DEPTH = 2
CONV_A_W = 3
CONV_B_W = 4
LRU_BLOCKS = 8
LRU_BW = D // LRU_BLOCKS
LRU_C = 8.0
HEADS = 8
HD = D // HEADS
PEER_HEADS = 8
PEER_DH = 128
N_KEYS = 128
N_EXPERTS = N_KEYS * N_KEYS
TOPK = 16
ALPHA = (2.0 * DEPTH) ** 0.25
LN_EPS = 1e-5
RMS_EPS = 1e-6
IN_COLS = 12 * D
SUB = 16
INTRA_CAP = 60.0
LANES = 128
SUBLANES = 8
VMEM_LIMIT = 56 * 1024 * 1024
GATE_KEYS = 4
GATE_ROWS = 64
SCORE_TOKENS = 256
DENSE_TOKENS = 1024
MXU_TILE = 256
MXU_ROWS = 512
MXU_STEP = 32
GATE_YIELD = 2

OFF_AB, OFF_AC, OFF_AX, OFF_BX, OFF_BG, OFF_CQ, OFF_CF, OFF_CI, OFF_CG, OFF_GA, OFF_GB, OFF_GC = (
    i * D for i in range(12))

BF = jnp.bfloat16
F32 = jnp.float32


def _sigmoid(x):
    return 0.5 * jnp.tanh(0.5 * x) + 0.5


def _gelu(x):
    return 0.5 * x * (1.0 + lax.erf(x * (2.0 ** -0.5)))


def _layer_norm(x, g, b):
    mu = jnp.mean(x, axis=-1, keepdims=True)
    xc = x - mu
    var = jnp.mean(xc * xc, axis=-1, keepdims=True)
    return xc * lax.rsqrt(var + LN_EPS) * g + b


def _nt_dot(a, b):
    return lax.dot_general(a, b, (((1,), (1,)), ((), ())), preferred_element_type=F32)


def _tn_dot(a, b):
    return lax.dot_general(a, b, (((0,), (0,)), ((), ())), preferred_element_type=F32)


def _shift_rows(x, d, fill, rows):
    return jnp.where(rows >= d, pltpu.roll(x, d, 0), fill)


def _scan_rows(a, u, carry, rows):
    in_group = rows & (SUBLANES - 1)
    d = 1
    while d < SUBLANES:
        if a is None:
            u = u + _shift_rows(u, d, 0.0, in_group)
        else:
            u = a * _shift_rows(u, d, 0.0, in_group) + u
            a = a * _shift_rows(a, d, 1.0, in_group)
        d *= 2
    out = []
    for g in range(u.shape[0] // SUBLANES):
        rs = slice(g * SUBLANES, (g + 1) * SUBLANES)
        h_g = u[rs, :] + carry if a is None else u[rs, :] + a[rs, :] * carry
        out.append(h_g)
        carry = h_g[SUBLANES - 1:SUBLANES, :]
    return jnp.concatenate(out, axis=0)


def _inproj_body(x_ref, w_ref, b_ref, o_ref):
    x = x_ref[...].astype(BF)
    o_ref[...] = jnp.dot(x, w_ref[...], preferred_element_type=F32) + b_ref[...]


def _in_proj(x2d, w_bf, b_row):
    n = x2d.shape[0]
    tm = min(n, 1024)
    tn = 2048
    return pl.pallas_call(
        _inproj_body,
        grid=(n // tm, IN_COLS // tn),
        in_specs=[pl.BlockSpec((tm, D), lambda i, j: (i, 0)),
                  pl.BlockSpec((D, tn), lambda i, j: (0, j)),
                  pl.BlockSpec((1, tn), lambda i, j: (0, j))],
        out_specs=pl.BlockSpec((tm, tn), lambda i, j: (i, j)),
        out_shape=jax.ShapeDtypeStruct((n, IN_COLS), F32),
        compiler_params=pltpu.CompilerParams(
            dimension_semantics=("arbitrary", "arbitrary"), vmem_limit_bytes=VMEM_LIMIT),
        name="in_proj",
    )(x2d, w_bf, b_row)


def _mixer_body(x_ref, z_ref, pa_ref, pb_ref, h0_ref, s0_ref, lb_ref, cwa_ref, cwb_ref, cbb_ref,
                wa_ref, ba_ref, wx_ref, bx_ref, lam_ref, ng_ref, woa_ref, wob_ref, woc_ref, wo_ref,
                g1_ref, b1_ref,
                x1_ref, na_ref, nb_ref, nh_ref, ns_ref,
                ua, ub, hc, st, qs, ks, vs, bs, att, oc, kf, bf, qsil, qe, kd, att_all, *, chunk):
    c = pl.program_id(1)
    last = pl.num_programs(1) - 1
    nsub = chunk // SUB
    rows = lax.broadcasted_iota(jnp.int32, (chunk, D), 0)

    rows8 = lax.broadcasted_iota(jnp.int32, (SUBLANES, D), 0)

    def causal_conv(slab, w_ref, width, prev):
        out = []
        below = [pltpu.roll(prev, d, 0) for d in range(1, width)]
        for g in range(chunk // SUBLANES):
            x_g = slab(g)
            y = w_ref[width - 1:width, :] * x_g
            for d in range(1, width):
                cur = pltpu.roll(x_g, d, 0)
                y = y + w_ref[width - 1 - d:width - d, :] * jnp.where(rows8 < d, below[d - 1], cur)
                below[d - 1] = cur
            out.append(y)
        return jnp.concatenate(out, axis=0), x_g

    @pl.when(c == 0)
    def _():
        ua[...] = jnp.zeros((SUBLANES, D), F32)
        ub[...] = jnp.zeros((SUBLANES, D), F32)
        ua[SUBLANES - 2:SUBLANES, :] = pa_ref[...]
        ub[SUBLANES - 3:SUBLANES, :] = pb_ref[...]
        hc[0:1, :] = h0_ref[...]
        for h in range(HEADS):
            st[h] = s0_ref[h].T
        att_all[...] = jnp.zeros((HEADS, chunk, chunk), F32)

    def slab_a(g):
        rs = slice(g * SUBLANES, (g + 1) * SUBLANES)
        return z_ref[rs, OFF_AC:OFF_AC + D] * z_ref[rs, OFF_AX:OFF_AX + D]

    conv_a, last_a = causal_conv(slab_a, cwa_ref, CONV_A_W, ua[...])
    na_ref[...] = last_a[SUBLANES - (CONV_A_W - 1):SUBLANES, :]
    ua[...] = last_a
    y_a = jnp.dot((z_ref[:, OFF_AB:OFF_AB + D] * conv_a).astype(BF), woa_ref[...],
                  preferred_element_type=F32)
    merged = _sigmoid(z_ref[:, OFF_GA:OFF_GA + D]) * y_a

    conv_b, last_b = causal_conv(
        lambda g: z_ref[g * SUBLANES:(g + 1) * SUBLANES, OFF_BX:OFF_BX + D], cwb_ref, CONV_B_W, ub[...])
    xl = conv_b + cbb_ref[...]
    nb_ref[...] = last_b[SUBLANES - (CONV_B_W - 1):SUBLANES, :]
    ub[...] = last_b

    xl_bf = xl.astype(BF)
    ra = jnp.concatenate(
        [jnp.dot(xl_bf[:, n * LRU_BW:(n + 1) * LRU_BW], wa_ref[n], preferred_element_type=F32)
         for n in range(LRU_BLOCKS)], axis=1)
    rx = jnp.concatenate(
        [jnp.dot(xl_bf[:, n * LRU_BW:(n + 1) * LRU_BW], wx_ref[n], preferred_element_type=F32)
         for n in range(LRU_BLOCKS)], axis=1)
    r = _sigmoid(ra + ba_ref[...])
    gi = _sigmoid(rx + bx_ref[...])
    nlam = -lam_ref[...]
    softplus = jnp.maximum(nlam, 0.0) + jnp.log1p(jnp.exp(-jnp.abs(nlam)))
    log_a = (-LRU_C) * r * softplus
    a = jnp.exp(log_a)
    uu = jnp.sqrt(jnp.tanh(-log_a) * (a * a + 1.0)) * (gi * xl)
    hh = _scan_rows(a, uu, hc[0:1, :], rows)
    h_last = hh[chunk - 1:chunk, :]
    hc[0:1, :] = h_last
    nh_ref[...] = h_last
    y_b = jnp.dot((_gelu(z_ref[:, OFF_BG:OFF_BG + D]) * hh).astype(BF), wob_ref[...],
                  preferred_element_type=F32)
    merged = merged + _sigmoid(z_ref[:, OFF_GB:OFF_GB + D]) * y_b

    cols_a = lax.broadcasted_iota(jnp.int32, (SUB, chunk), 1)
    sub_rows = lax.broadcasted_iota(jnp.int32, (SUB, 1), 0)
    f_all = lb_ref[...] + (1.0 - lb_ref[...]) * _sigmoid(z_ref[:, OFF_CF:OFF_CF + D])
    kf[...] = 1.0 - f_all
    bf[...] = _scan_rows(None, jnp.log(f_all), jnp.zeros((1, D), F32), rows)
    intra_bound = jnp.max(functools.reduce(jnp.maximum, [
        (bf[i * SUB - 1:i * SUB, :] if i else 0.0) - bf[(i + 1) * SUB - 1:(i + 1) * SUB, :]
        for i in range(nsub)]))

    def head_step(h, carry):
        off = pl.multiple_of(h * HD, HD)
        k = kf[:, pl.ds(off, HD)]
        b = bf[:, pl.ds(off, HD)]
        cq = z_ref[:, pl.ds(OFF_CQ + off, HD)]
        q = cq * _sigmoid(cq)
        v = z_ref[:, pl.ds(OFF_CI + off, HD)]
        qs[...] = q
        ks[...] = k
        vs[...] = v
        bs[...] = b
        st_h = st[h]
        b_last = b[chunk - 1:chunk, :]
        o = _nt_dot((q * jnp.exp(b)).astype(BF), st_h.astype(BF))
        kdec = k * jnp.exp(b_last - b)
        st[h] = st_h * jnp.exp(b_last) + _tn_dot(v.astype(BF), kdec.astype(BF))

        def sub_start(i):
            return bs[i * SUB - 1:i * SUB, :] if i else jnp.zeros((1, HD), F32)

        def intra_direct():
            att[0:SUB, :] = jnp.zeros((SUB, chunk), F32)
            for i in range(1, nsub):
                lo, hi = i * SUB, (i + 1) * SUB
                q_i = qs[lo:hi, :] * jnp.exp(bs[lo:hi, :] - sub_start(i))
                k_i = k * jnp.exp(jnp.minimum(sub_start(i) - b, 0.0))
                a_i = _nt_dot(q_i.astype(BF), k_i.astype(BF))
                att[lo:hi, :] = jnp.where(cols_a < lo, a_i, 0.0)
            o_off = jnp.dot(att[...].astype(BF), v.astype(BF), preferred_element_type=F32)
            o_parts = []
            for i in range(nsub):
                q_i = qs[i * SUB:(i + 1) * SUB, :]
                b_i = bs[i * SUB:(i + 1) * SUB, :]
                o_i = jnp.zeros((SUB, HD), F32)
                for s in range(SUB):
                    r0 = i * SUB + s
                    p = q_i * ks[r0:r0 + 1, :] * jnp.exp(jnp.minimum(b_i - bs[r0:r0 + 1, :], 0.0))
                    a_col = jnp.sum(p, axis=-1, keepdims=True)
                    a_col = jnp.where(sub_rows >= s, a_col, 0.0)
                    o_i = o_i + a_col * vs[r0:r0 + 1, :]
                o_parts.append(o_i)
            return o_off + jnp.concatenate(o_parts, axis=0)

        o = o + intra_direct()
        oc[:, pl.ds(off, HD)] = o * lax.rsqrt(jnp.mean(o * o, axis=-1, keepdims=True) + RMS_EPS) * ng_ref[...]
        return carry

    def hgrn_factored():
        cq = z_ref[:, OFF_CQ:OFF_CQ + D]
        qsil[...] = cq * _sigmoid(cq)
        b_last = bf[chunk - 1:chunk, :]
        qe[...] = (qsil[...] * jnp.exp(bf[...])).astype(BF)
        kd[...] = (kf[...] * jnp.exp(b_last - bf[...])).astype(BF)
        for i in range(nsub):
            lo, hi = i * SUB, (i + 1) * SUB
            start = bf[lo - 1:lo, :] if i else 0.0
            q_i = (qsil[lo:hi, :] * jnp.exp(bf[lo:hi, :] - start)).astype(BF)
            k_i = (kf[0:hi, :] * jnp.exp(jnp.minimum(start - bf[0:hi, :], INTRA_CAP))).astype(BF)
            causal = (lax.broadcasted_iota(jnp.int32, (SUB, hi), 1)
                      <= lax.broadcasted_iota(jnp.int32, (SUB, hi), 0) + lo)
            for h in range(HEADS):
                hs = slice(h * HD, (h + 1) * HD)
                att_all[h, lo:hi, 0:hi] = jnp.where(causal, _nt_dot(q_i[:, hs], k_i[:, hs]), 0.0)
        for h in range(HEADS):
            hs = slice(h * HD, (h + 1) * HD)
            v_h = z_ref[:, OFF_CI + h * HD:OFF_CI + (h + 1) * HD].astype(BF)
            st_h = st[h]
            o = (_nt_dot(qe[:, hs], st_h.astype(BF))
                 + jnp.dot(att_all[h].astype(BF), v_h, preferred_element_type=F32))
            st[h] = st_h * jnp.exp(b_last[:, hs]) + _tn_dot(v_h, kd[:, hs])
            oc[:, hs] = o * lax.rsqrt(jnp.mean(o * o, axis=-1, keepdims=True) + RMS_EPS) * ng_ref[...]

    def hgrn_direct():
        lax.fori_loop(0, HEADS, head_step, 0)

    lax.cond(intra_bound <= INTRA_CAP, hgrn_factored, hgrn_direct)

    @pl.when(c == last)
    def _():
        for h in range(HEADS):
            ns_ref[h] = st[h].T

    cg = z_ref[:, OFF_CG:OFF_CG + D]
    y_c = jnp.dot((oc[...] * (cg * _sigmoid(cg))).astype(BF), woc_ref[...], preferred_element_type=F32)
    merged = merged + _sigmoid(z_ref[:, OFF_GC:OFF_GC + D]) * y_c
    res = ALPHA * x_ref[...] + jnp.dot(merged.astype(BF), wo_ref[...], preferred_element_type=F32)
    x1_ref[...] = _layer_norm(res, g1_ref[...], b1_ref[...])


def _mixer(x, z, past_a, past_b, h0, s0, lb, lw):
    bsz, t_len, _ = x.shape
    chunk = min(t_len, 128)
    assert t_len % chunk == 0 and chunk % SUB == 0
    nc = t_len // chunk
    full2 = lambda shape: pl.BlockSpec(shape, lambda b, c: (0, 0))
    full3 = lambda shape: pl.BlockSpec(shape, lambda b, c: (0, 0, 0))
    in_specs = [
        pl.BlockSpec((None, chunk, D), lambda b, c: (b, c, 0)),
        pl.BlockSpec((None, chunk, IN_COLS), lambda b, c: (b, c, 0)),
        pl.BlockSpec((None, CONV_A_W - 1, D), lambda b, c: (b, 0, 0)),
        pl.BlockSpec((None, CONV_B_W - 1, D), lambda b, c: (b, 0, 0)),
        pl.BlockSpec((None, 1, D), lambda b, c: (b, 0, 0)),
        pl.BlockSpec((None, HEADS, HD, HD), lambda b, c: (b, 0, 0, 0)),
        full2((1, D)),
        full2((CONV_A_W, D)), full2((CONV_B_W, D)), full2((1, D)),
        full3((LRU_BLOCKS, LRU_BW, LRU_BW)), full2((1, D)),
        full3((LRU_BLOCKS, LRU_BW, LRU_BW)), full2((1, D)),
        full2((1, D)), full2((1, HD)),
        full2((D, D)), full2((D, D)), full2((D, D)), full2((D, D)),
        full2((1, D)), full2((1, D)),
    ]
    out_specs = [
        pl.BlockSpec((None, chunk, D), lambda b, c: (b, c, 0)),
        pl.BlockSpec((None, CONV_A_W - 1, D), lambda b, c: (b, 0, 0)),
        pl.BlockSpec((None, CONV_B_W - 1, D), lambda b, c: (b, 0, 0)),
        pl.BlockSpec((None, 1, D), lambda b, c: (b, 0, 0)),
        pl.BlockSpec((None, HEADS, HD, HD), lambda b, c: (b, 0, 0, 0)),
    ]
    out_shape = [
        jax.ShapeDtypeStruct((bsz, t_len, D), F32),
        jax.ShapeDtypeStruct((bsz, CONV_A_W - 1, D), F32),
        jax.ShapeDtypeStruct((bsz, CONV_B_W - 1, D), F32),
        jax.ShapeDtypeStruct((bsz, 1, D), F32),
        jax.ShapeDtypeStruct((bsz, HEADS, HD, HD), F32),
    ]
    scratch = [
        pltpu.VMEM((SUBLANES, D), F32),
        pltpu.VMEM((SUBLANES, D), F32),
        pltpu.VMEM((SUBLANES, D), F32),
        pltpu.VMEM((HEADS, HD, HD), F32),
        pltpu.VMEM((chunk, HD), F32),
        pltpu.VMEM((chunk, HD), F32),
        pltpu.VMEM((chunk, HD), F32),
        pltpu.VMEM((chunk, HD), F32),
        pltpu.VMEM((chunk, chunk), F32),
        pltpu.VMEM((chunk, D), F32),
        pltpu.VMEM((chunk, D), F32),
        pltpu.VMEM((chunk, D), F32),
        pltpu.VMEM((chunk, D), F32),
        pltpu.VMEM((chunk, D), BF),
        pltpu.VMEM((chunk, D), BF),
        pltpu.VMEM((HEADS, chunk, chunk), F32),
    ]
    return pl.pallas_call(
        functools.partial(_mixer_body, chunk=chunk),
        grid=(bsz, nc),
        in_specs=in_specs, out_specs=out_specs, out_shape=out_shape, scratch_shapes=scratch,
        compiler_params=pltpu.CompilerParams(
            dimension_semantics=("arbitrary", "arbitrary"), vmem_limit_bytes=VMEM_LIMIT),
        name="mixer",
    )(x, z, past_a, past_b, h0, s0, lb, lw["conv_a_w"], lw["conv_b_w"], lw["conv_b_b"],
      lw["lru_wa"], lw["lru_ba"], lw["lru_wx"], lw["lru_bx"], lw["lru_lambda"], lw["hgrn_norm_g"],
      lw["w_out_a"], lw["w_out_b"], lw["w_out_c"], lw["w_o"], lw["ln1_g"], lw["ln1_b"])


def _sort_network(n):
    pairs = []

    def merge(lo, m, r):
        step = 2 * r
        if step < m:
            merge(lo, m, step)
            merge(lo + r, m, step)
            pairs.extend((i, i + r) for i in range(lo + r, lo + m - r, step))
        else:
            pairs.append((lo, lo + r))

    def sort(lo, m):
        if m > 1:
            sort(lo, m // 2)
            sort(lo + m // 2, m // 2)
            merge(lo, m, 1)

    sort(0, n)
    return pairs


_SORT16 = _sort_network(TOPK)


def _cmp_exchange(v, i, j):
    if v[j] is None:
        return
    if v[i] is None:
        v[i], v[j] = v[j], None
        return
    v[i], v[j] = jnp.maximum(v[i], v[j]), jnp.minimum(v[i], v[j])


def _sort_desc(v):
    v = list(v) + [None] * (TOPK - len(v))
    for i, j in _SORT16:
        _cmp_exchange(v, i, j)
    return v


def _bitonic_to_desc(v):
    v = list(v)
    d = TOPK // 2
    while d >= 1:
        for i in range(TOPK):
            if i & d == 0:
                _cmp_exchange(v, i, i + d)
        d //= 2
    return v


def _merge_rolled(v, shift):
    w = [None if x is None else pltpu.roll(x, shift, 0) for x in v]
    out = []
    for i in range(TOPK):
        a, b = v[i], w[TOPK - 1 - i]
        out.append(b if a is None else a if b is None else jnp.maximum(a, b))
    return out


def _top16_of_rows(vregs):
    v = _sort_desc(vregs)
    for shift in (4, 2, 1):
        v = _bitonic_to_desc(_merge_rolled(v, shift))
    return v


def _kth16_of_rows(vregs):
    v = _sort_desc(vregs)
    for shift in (4, 2):
        v = _bitonic_to_desc(_merge_rolled(v, shift))
    return functools.reduce(jnp.minimum, _merge_rolled(v, 1))


def _pack_rows(vs, sub):
    out = vs[0]
    for j in range(1, SUBLANES):
        out = jnp.where(sub == j, vs[j], out)
    return out


def _scores_body(x_ref, wq_ref, keys_ref, xt_ref, p1_ref, p2_ref, th_ref, qbuf, *, tn):
    xt_ref[...] = x_ref[...].T.astype(BF)
    qbuf[...] = jnp.dot(x_ref[...].astype(BF), wq_ref[...], preferred_element_type=F32).astype(BF)
    tw = SCORE_TOKENS
    sub = lax.broadcasted_iota(jnp.int32, (SUBLANES, tw), 0)
    n_tc = tn // tw

    def step(idx, carry):
        h = idx // n_tc
        tc = idx % n_tc
        t0 = pl.multiple_of(tc * tw, tw)
        c0 = pl.multiple_of(h * (2 * PEER_DH), 2 * PEER_DH)
        s1 = _nt_dot(keys_ref[h, 0], qbuf[pl.ds(t0, tw), pl.ds(c0, PEER_DH)])
        s2 = _nt_dot(keys_ref[h, 1], qbuf[pl.ds(t0, tw), pl.ds(c0 + PEER_DH, PEER_DH)])
        split = lambda s: [s[r * SUBLANES:(r + 1) * SUBLANES, :] for r in range(N_KEYS // SUBLANES)]
        a = _top16_of_rows(split(s1))
        b = _top16_of_rows(split(s2))
        pa = [jnp.exp(x - a[0]) for x in a]
        pb = [jnp.exp(x - b[0]) for x in b]
        pb_lo, pb_hi = _pack_rows(pb[:SUBLANES], sub), _pack_rows(pb[SUBLANES:], sub)
        pa_hi = _pack_rows(pa[SUBLANES:], sub)
        lhs = [pa[0], pa[0]] + pa[1:SUBLANES] + [pa_hi]
        rhs = [pb_lo, pb_hi] + [pb_lo] * (SUBLANES - 1) + [pb[0]]
        t = [x * y for x, y in zip(lhs, rhs)]
        t16 = _kth16_of_rows(t)
        keep = [x >= t16 for x in t]
        z_sum = jnp.sum(sum(jnp.where(m, x, 0.0) for m, x in zip(keep, t)), axis=0, keepdims=True)
        rz = 1.0 / z_sum
        gates = [(x * rz) * y for x, y in zip(lhs, rhs)]
        th = jnp.min(functools.reduce(jnp.minimum, [jnp.where(m, g, jnp.inf) for m, g in zip(keep, gates)]),
                     axis=0, keepdims=True)
        p1_ref[h, :, pl.ds(t0, tw)] = jnp.exp(s1 - a[0][0:1]) * rz
        p2 = jnp.exp(s2 - b[0][0:1])
        for part in range(tw // LANES):
            p2_ref[h, tc * (tw // LANES) + part] = p2[:, part * LANES:(part + 1) * LANES]
        th_ref[h, :, pl.ds(t0, tw)] = th
        return carry

    lax.fori_loop(0, PEER_HEADS * n_tc, step, 0)


def _peer_scores(x2d, wq_bf, keys_bf, tn):
    n = x2d.shape[0]
    n_tc = tn // LANES
    return pl.pallas_call(
        functools.partial(_scores_body, tn=tn),
        grid=(n // tn,),
        in_specs=[pl.BlockSpec((tn, D), lambda i: (i, 0)),
                  pl.BlockSpec((D, 2 * PEER_HEADS * PEER_DH), lambda i: (0, 0)),
                  pl.BlockSpec((PEER_HEADS, 2, N_KEYS, PEER_DH), lambda i: (0, 0, 0, 0))],
        out_specs=[pl.BlockSpec((D, tn), lambda i: (0, i)),
                   pl.BlockSpec((PEER_HEADS, N_KEYS, tn), lambda i: (0, 0, i)),
                   pl.BlockSpec((PEER_HEADS, n_tc, N_KEYS, LANES), lambda i: (0, i, 0, 0)),
                   pl.BlockSpec((PEER_HEADS, 1, tn), lambda i: (0, 0, i))],
        out_shape=[jax.ShapeDtypeStruct((D, n), BF),
                   jax.ShapeDtypeStruct((PEER_HEADS, N_KEYS, n), F32),
                   jax.ShapeDtypeStruct((PEER_HEADS, n // LANES, N_KEYS, LANES), F32),
                   jax.ShapeDtypeStruct((PEER_HEADS, 1, n), F32)],
        scratch_shapes=[pltpu.VMEM((tn, 2 * PEER_HEADS * PEER_DH), BF)],
        compiler_params=pltpu.CompilerParams(
            dimension_semantics=("arbitrary",), vmem_limit_bytes=VMEM_LIMIT),
        name="peer_scores",
    )(x2d, wq_bf, keys_bf)


def _dense_body(xt_ref, x_ref, u0_ref, un_ref, vp_ref, p1_ref, p2_ref, th_ref, g2_ref, b2_ref,
                y_ref, acc, h_even, h_odd, g_even, g_odd, *, tn, eb, nb):
    j = pl.program_id(1)
    kb = eb // N_KEYS
    n_tc = tn // LANES

    def matmul_jobs(jobs):
        assert len(jobs) % 2 == 0
        for p in range(0, len(jobs), 2):
            pair = jobs[p:p + 2]
            base = (p // 2 % 2) * (MXU_ROWS // 4)
            for k in range(D // MXU_TILE):
                ks = slice(k * MXU_TILE, (k + 1) * MXU_TILE)
                for q, (lhs, rhs, r0, c0, _) in enumerate(pair):
                    pltpu.matmul_push_rhs(rhs[ks, c0:c0 + MXU_TILE], staging_register=k % 2, mxu_index=q)
                shared_lhs = pair[0][0] is pair[1][0] and pair[0][2] == pair[1][2]
                for m in range(MXU_ROWS // MXU_STEP):
                    piece = None
                    for q, (lhs, rhs, r0, c0, _) in enumerate(pair):
                        if piece is None or not shared_lhs:
                            piece = lhs[r0 + m * MXU_STEP:r0 + (m + 1) * MXU_STEP, ks]
                        pltpu.matmul_acc_lhs(base + m * (MXU_STEP // 4), piece, q,
                                             load_staged_rhs=(k % 2) if m == 0 else None)
                    yield
            for q, (lhs, rhs, r0, c0, sink) in enumerate(pair):
                sink(r0, c0, pltpu.matmul_pop(base, (MXU_ROWS, MXU_TILE), F32, q))

    def tiles(lhs, rhs, sink):
        return [(lhs, rhs, r0, c0, sink) for r0 in range(0, eb, MXU_ROWS) for c0 in range(0, tn, MXU_TILE)]

    def store_into(dst):
        def sink(r0, c0, val):
            dst[r0:r0 + MXU_ROWS, c0:c0 + MXU_TILE] = val
        return sink

    def add_into_acc(r0, c0, val):
        acc[r0:r0 + MXU_ROWS, c0:c0 + MXU_TILE] += val

    def run(*programs):
        live = list(programs)
        while live:
            for g in list(live):
                if next(g, StopIteration) is StopIteration:
                    live.remove(g)

    @pl.when(j == 0)
    def _():
        acc[...] = jnp.zeros((D, tn), F32)
        g_odd[:, 0:tn] = jnp.zeros((eb, tn), BF)
        run(matmul_jobs(tiles(u0_ref, xt_ref, store_into(h_even))))

    def gate_build(h_cur, g_cur):
        for k0 in range(0, kb, GATE_KEYS):
            for tc in range(n_tc):
                for half in range(N_KEYS // GATE_ROWS):
                    lanes = slice(tc * LANES, (tc + 1) * LANES)
                    sub_rows = slice(half * GATE_ROWS, (half + 1) * GATE_ROWS)
                    w = [None] * GATE_KEYS
                    for h in range(PEER_HEADS):
                        p2 = p2_ref[h, tc, sub_rows, :]
                        th = th_ref[h, :, lanes]
                        for a in range(GATE_KEYS):
                            prod = p1_ref[h, k0 + a:k0 + a + 1, lanes] * p2
                            kept = jnp.where(prod >= th, prod, 0.0)
                            w[a] = kept if h == 0 else w[a] + kept
                            if a % GATE_YIELD == GATE_YIELD - 1:
                                yield
                    for a in range(GATE_KEYS):
                        r0 = (k0 + a) * N_KEYS + half * GATE_ROWS
                        g_cur[r0:r0 + GATE_ROWS, lanes] = (
                            w[a] * _gelu(h_cur[r0:r0 + GATE_ROWS, lanes])).astype(BF)

    def block_step(h_cur, h_next, g_cur, g_prev):
        run(matmul_jobs(tiles(un_ref, xt_ref, store_into(h_next)) + tiles(vp_ref, g_prev, add_into_acc)),
            gate_build(h_cur, g_cur))

    @pl.when((j < nb) & (j % 2 == 0))
    def _():
        block_step(h_even, h_odd, g_even, g_odd)

    @pl.when((j < nb) & (j % 2 == 1))
    def _():
        block_step(h_odd, h_even, g_odd, g_even)

    @pl.when(j == nb)
    def _():
        g_last = g_even if (nb - 1) % 2 == 0 else g_odd

        run(matmul_jobs(tiles(vp_ref, g_last, add_into_acc)))
        y_ref[...] = _layer_norm(ALPHA * x_ref[...] + acc[...].T, g2_ref[...], b2_ref[...])


def _peer_dense(xt_bf, x2d, u_bf, vt_bf, p1, p2, th, g2, b2, tn):
    n = x2d.shape[0]
    eb = D
    nb = N_EXPERTS // eb
    n_tc = tn // LANES
    assert eb // N_KEYS == SUBLANES and (eb // N_KEYS) % GATE_KEYS == 0
    once = pl.Buffered(1)
    return pl.pallas_call(
        functools.partial(_dense_body, tn=tn, eb=eb, nb=nb),
        grid=(n // tn, nb + 1),
        in_specs=[pl.BlockSpec((D, tn), lambda i, j: (0, i), pipeline_mode=once),
                  pl.BlockSpec((tn, D), lambda i, j: (i, 0), pipeline_mode=once),
                  pl.BlockSpec((eb, D), lambda i, j: (0, 0), pipeline_mode=once),
                  pl.BlockSpec((eb, D), lambda i, j: (jnp.minimum(j + 1, nb - 1), 0)),
                  pl.BlockSpec((None, D, eb), lambda i, j: (jnp.maximum(j - 1, 0), 0, 0)),
                  pl.BlockSpec((PEER_HEADS, eb // N_KEYS, tn),
                               lambda i, j: (0, jnp.minimum(j, nb - 1), i)),
                  pl.BlockSpec((PEER_HEADS, n_tc, N_KEYS, LANES), lambda i, j: (0, i, 0, 0),
                               pipeline_mode=once),
                  pl.BlockSpec((PEER_HEADS, 1, tn), lambda i, j: (0, 0, i), pipeline_mode=once),
                  pl.BlockSpec((1, D), lambda i, j: (0, 0), pipeline_mode=once),
                  pl.BlockSpec((1, D), lambda i, j: (0, 0), pipeline_mode=once)],
        out_specs=pl.BlockSpec((tn, D), lambda i, j: (i, 0)),
        out_shape=jax.ShapeDtypeStruct((n, D), F32),
        scratch_shapes=[pltpu.VMEM((D, tn), F32),
                        pltpu.VMEM((eb, tn + LANES), F32), pltpu.VMEM((eb, tn + LANES), F32),
                        pltpu.VMEM((eb, tn + LANES), BF), pltpu.VMEM((eb, tn + LANES), BF)],
        compiler_params=pltpu.CompilerParams(
            dimension_semantics=("arbitrary", "arbitrary"), vmem_limit_bytes=VMEM_LIMIT),
        name="peer_dense",
    )(xt_bf, x2d, u_bf, u_bf, vt_bf, p1, p2, th, g2, b2)


def _trunk_layer(x, past_a, past_b, h0, s0, lb, lw):
    bsz, t_len, _ = x.shape
    n = bsz * t_len
    z = _in_proj(x.reshape(n, D), lw["w_in"], lw["b_in"]).reshape(bsz, t_len, IN_COLS)
    x1, new_a, new_b, h_t, s_t = _mixer(x, z, past_a, past_b, h0, s0, lb, lw)
    x1 = x1.reshape(n, D)
    tn = min(n, 512)
    xt, p1, p2, th = _peer_scores(x1, lw["peer_wq"], lw["peer_keys"], tn)
    x2 = _peer_dense(xt, x1, lw["peer_u"], lw["peer_vt"], p1, p2, th, lw["ln2_g"], lw["ln2_b"],
                     min(n, DENSE_TOKENS))
    return x2.reshape(bsz, t_len, D), new_a, new_b, h_t.reshape(bsz, D), s_t


def kernel(x_prompt, x_sample, state_conv_a, state_conv_b, state_lru, state_hgrn, w_in, b_in, conv_a_w,
           conv_b_w, conv_b_b, lru_wa, lru_ba, lru_wx, lru_bx, lru_lambda, hgrn_lb_logits, hgrn_norm_g,
           w_out_a, w_out_b, w_out_c, w_o, ln1_g, ln1_b, peer_wq, peer_keys, peer_u, peer_v, ln2_g,
           ln2_b):
    p = jax.nn.softmax(hgrn_lb_logits.astype(F32), axis=0)
    lbs = jnp.cumsum(p, axis=0) - p[0]
    bp = x_prompt.shape[0]
    xp, xs = x_prompt, x_sample
    outs_p, outs_s = [], []
    for l in range(DEPTH):
        row = lambda a: a[l].reshape(1, -1)
        lw = dict(
            w_in=w_in[l].astype(BF), b_in=row(b_in),
            conv_a_w=conv_a_w[l], conv_b_w=conv_b_w[l], conv_b_b=row(conv_b_b),
            lru_wa=lru_wa[l].astype(BF), lru_ba=row(lru_ba), lru_wx=lru_wx[l].astype(BF),
            lru_bx=row(lru_bx), lru_lambda=row(lru_lambda), hgrn_norm_g=row(hgrn_norm_g),
            w_out_a=w_out_a[l].astype(BF), w_out_b=w_out_b[l].astype(BF),
            w_out_c=w_out_c[l].astype(BF), w_o=w_o[l].astype(BF),
            ln1_g=row(ln1_g), ln1_b=row(ln1_b),
            peer_wq=peer_wq[l].astype(BF), peer_keys=peer_keys[l].astype(BF),
            peer_u=peer_u[l].astype(BF),
            peer_vt=peer_v[l].reshape(N_EXPERTS // D, D, D).transpose(0, 2, 1).astype(BF),
            ln2_g=row(ln2_g), ln2_b=row(ln2_b))
        lb = lbs[l].reshape(1, D)
        xp, pa, pb, ph, ps = _trunk_layer(
            xp, jnp.zeros((bp, CONV_A_W - 1, D), F32), jnp.zeros((bp, CONV_B_W - 1, D), F32),
            jnp.zeros((bp, 1, D), F32), jnp.zeros((bp, HEADS, HD, HD), F32), lb, lw)
        xs, sa, sb, sh, ss = _trunk_layer(
            xs, state_conv_a[l], state_conv_b[l], state_lru[l][:, None, :], state_hgrn[l], lb, lw)
        outs_p.append((pa, pb, ph, ps))
        outs_s.append((sa, sb, sh, ss))
    stack = lambda outs, k: jnp.stack([o[k] for o in outs])
    return (xp, xs,
            stack(outs_p, 0), stack(outs_p, 1), stack(outs_p, 2), stack(outs_p, 3),
            stack(outs_s, 0), stack(outs_s, 1), stack(outs_s, 2), stack(outs_s, 3))
```

```python
import functools

import jax
import jax.numpy as jnp
from jax import lax
from jax.experimental import pallas as pl
from jax.experimental.pallas import tpu as pltpu

D = 1024
DEPTH = 2
CONV_A_W = 3
CONV_B_W = 4
LRU_BLOCKS = 8
LRU_BW = D // LRU_BLOCKS
LRU_C = 8.0
HEADS = 8
HD = D // HEADS
PEER_HEADS = 8
PEER_DH = 128
N_KEYS = 128
N_EXPERTS = N_KEYS * N_KEYS
TOPK = 16
ALPHA = (2.0 * DEPTH) ** 0.25
LN_EPS = 1e-5
RMS_EPS = 1e-6
IN_COLS = 12 * D
SUB = 16
INTRA_CAP = 60.0
LANES = 128
SUBLANES = 8
VMEM_LIMIT = 56 * 1024 * 1024
GATE_KEYS = 4
GATE_ROWS = 64
SCORE_TOKENS = 256
DENSE_TOKENS = 1024
MXU_TILE = 256
MXU_ROWS = 512
MXU_STEP = 32
GATE_YIELD = 2

OFF_AB, OFF_AC, OFF_AX, OFF_BX, OFF_BG, OFF_CQ, OFF_CF, OFF_CI, OFF_CG, OFF_GA, OFF_GB, OFF_GC = (
    i * D for i in range(12))

BF = jnp.bfloat16
F32 = jnp.float32


def _tanh_half(x):
    return jnp.tanh(0.5 * x)


def _sigmoid(x):
    return 0.5 * _tanh_half(x) + 0.5


def _silu(x):
    h = 0.5 * x
    return h + h * jnp.tanh(h)


def _gelu_twice(x):
    return x * (1.0 + lax.erf(x * (2.0 ** -0.5)))


def _gelu(x):
    return 0.5 * _gelu_twice(x)


def _layer_norm(x, g, b):
    mu = jnp.mean(x, axis=-1, keepdims=True)
    xc = x - mu
    var = jnp.mean(xc * xc, axis=-1, keepdims=True)
    return xc * lax.rsqrt(var + LN_EPS) * g + b


def _nt_dot(a, b):
    return lax.dot_general(a, b, (((1,), (1,)), ((), ())), preferred_element_type=F32)


def _tn_dot(a, b):
    return lax.dot_general(a, b, (((0,), (0,)), ((), ())), preferred_element_type=F32)


def _scan_rows(a, u, carry, rows8):
    out = []
    for g in range(u.shape[0] // SUBLANES):
        rs = slice(g * SUBLANES, (g + 1) * SUBLANES)
        u_g = u[rs, :]
        a_g = None if a is None else a[rs, :]
        d = 1
        while d < SUBLANES:
            keep = rows8 >= d
            if a_g is None:
                u_g = u_g + jnp.where(keep, pltpu.roll(u_g, d, 0), 0.0)
            else:
                u_g = a_g * jnp.where(keep, pltpu.roll(u_g, d, 0), 0.0) + u_g
                a_g = a_g * jnp.where(keep, pltpu.roll(a_g, d, 0), 1.0)
            d *= 2
        h_g = u_g + carry if a_g is None else u_g + a_g * carry
        out.append(h_g)
        carry = h_g[SUBLANES - 1:SUBLANES, :]
    return jnp.concatenate(out, axis=0)


def _inproj_body(x_ref, w_ref, b_ref, o_ref):
    x = x_ref[...].astype(BF)
    o_ref[...] = jnp.dot(x, w_ref[...], preferred_element_type=F32) + b_ref[...]


def _in_proj(x2d, w_bf, b_row):
    n = x2d.shape[0]
    tm = min(n, 1024)
    tn = 2048
    return pl.pallas_call(
        _inproj_body,
        grid=(n // tm, IN_COLS // tn),
        in_specs=[pl.BlockSpec((tm, D), lambda i, j: (i, 0)),
                  pl.BlockSpec((D, tn), lambda i, j: (0, j)),
                  pl.BlockSpec((1, tn), lambda i, j: (0, j))],
        out_specs=pl.BlockSpec((tm, tn), lambda i, j: (i, j)),
        out_shape=jax.ShapeDtypeStruct((n, IN_COLS), F32),
        compiler_params=pltpu.CompilerParams(
            dimension_semantics=("arbitrary", "arbitrary"), vmem_limit_bytes=VMEM_LIMIT),
        name="in_proj",
    )(x2d, w_bf, b_row)


def _mixer_body(x_ref, z_ref, pa_ref, pb_ref, h0_ref, s0_ref, lb_ref, cwa_ref, cwb_ref, cbb_ref,
                wa_ref, ba_ref, wx_ref, bx_ref, lam_ref, ng_ref, woa_ref, wob_ref, woc_ref, wo_ref,
                g1_ref, b1_ref,
                x1_ref, na_ref, nb_ref, nh_ref, ns_ref,
                ua, ub, hc, st, qs, ks, vs, bs, att, oc, kf, bf, qsil, qe, kd, att_all, *, chunk):
    c = pl.program_id(1)
    last = pl.num_programs(1) - 1
    nsub = chunk // SUB

    rows8 = lax.broadcasted_iota(jnp.int32, (SUBLANES, D), 0)

    def causal_conv(slab, w_ref, width, prev):
        out = []
        below = [pltpu.roll(prev, d, 0) for d in range(1, width)]
        for g in range(chunk // SUBLANES):
            x_g = slab(g)
            y = w_ref[width - 1:width, :] * x_g
            for d in range(1, width):
                cur = pltpu.roll(x_g, d, 0)
                y = y + w_ref[width - 1 - d:width - d, :] * jnp.where(rows8 < d, below[d - 1], cur)
                below[d - 1] = cur
            out.append(y)
        return jnp.concatenate(out, axis=0), x_g

    @pl.when(c == 0)
    def _():
        ua[...] = jnp.zeros((SUBLANES, D), F32)
        ub[...] = jnp.zeros((SUBLANES, D), F32)
        ua[SUBLANES - 2:SUBLANES, :] = pa_ref[...]
        ub[SUBLANES - 3:SUBLANES, :] = pb_ref[...]
        hc[0:1, :] = h0_ref[...]
        for h in range(HEADS):
            st[h] = s0_ref[h].T
        att_all[...] = jnp.zeros((HEADS, chunk, chunk), F32)

    def slab_a(g):
        rs = slice(g * SUBLANES, (g + 1) * SUBLANES)
        return z_ref[rs, OFF_AC:OFF_AC + D] * z_ref[rs, OFF_AX:OFF_AX + D]

    conv_a, last_a = causal_conv(slab_a, cwa_ref, CONV_A_W, ua[...])
    na_ref[...] = last_a[SUBLANES - (CONV_A_W - 1):SUBLANES, :]
    ua[...] = last_a
    y_a = jnp.dot((z_ref[:, OFF_AB:OFF_AB + D] * conv_a).astype(BF), woa_ref[...],
                  preferred_element_type=F32)
    merged2 = (1.0 + _tanh_half(z_ref[:, OFF_GA:OFF_GA + D])) * y_a

    conv_b, last_b = causal_conv(
        lambda g: z_ref[g * SUBLANES:(g + 1) * SUBLANES, OFF_BX:OFF_BX + D], cwb_ref, CONV_B_W, ub[...])
    xl = conv_b + cbb_ref[...]
    nb_ref[...] = last_b[SUBLANES - (CONV_B_W - 1):SUBLANES, :]
    ub[...] = last_b

    xl_bf = xl.astype(BF)
    ra = jnp.concatenate(
        [jnp.dot(xl_bf[:, n * LRU_BW:(n + 1) * LRU_BW], wa_ref[n], preferred_element_type=F32)
         for n in range(LRU_BLOCKS)], axis=1)
    rx = jnp.concatenate(
        [jnp.dot(xl_bf[:, n * LRU_BW:(n + 1) * LRU_BW], wx_ref[n], preferred_element_type=F32)
         for n in range(LRU_BLOCKS)], axis=1)
    gi = _sigmoid(rx + bx_ref[...])
    nlam = -lam_ref[...]
    softplus = jnp.maximum(nlam, 0.0) + jnp.log1p(jnp.exp(-jnp.abs(nlam)))
    log_a = ((-0.5 * LRU_C) * softplus) * (1.0 + _tanh_half(ra + ba_ref[...]))
    a = jnp.exp(log_a)
    uu = jnp.sqrt(jnp.tanh(-log_a) * (a * a + 1.0)) * (gi * xl)
    hh = _scan_rows(a, uu, hc[0:1, :], rows8)
    h_last = hh[chunk - 1:chunk, :]
    hc[0:1, :] = h_last
    nh_ref[...] = h_last
    y_b = jnp.dot((_gelu(z_ref[:, OFF_BG:OFF_BG + D]) * hh).astype(BF), wob_ref[...],
                  preferred_element_type=F32)
    merged2 = merged2 + (1.0 + _tanh_half(z_ref[:, OFF_GB:OFF_GB + D])) * y_b

    cols_a = lax.broadcasted_iota(jnp.int32, (SUB, chunk), 1)
    sub_rows = lax.broadcasted_iota(jnp.int32, (SUB, 1), 0)
    f_all = (0.5 * (1.0 + lb_ref[...])) + (0.5 * (1.0 - lb_ref[...])) * _tanh_half(z_ref[:, OFF_CF:OFF_CF + D])
    kf[...] = 1.0 - f_all
    bf[...] = _scan_rows(None, jnp.log(f_all), jnp.zeros((1, D), F32), rows8)
    intra_bound = jnp.max(functools.reduce(jnp.maximum, [
        (bf[i * SUB - 1:i * SUB, :] if i else 0.0) - bf[(i + 1) * SUB - 1:(i + 1) * SUB, :]
        for i in range(nsub)]))

    def head_step(h, carry):
        off = pl.multiple_of(h * HD, HD)
        k = kf[:, pl.ds(off, HD)]
        b = bf[:, pl.ds(off, HD)]
        cq = z_ref[:, pl.ds(OFF_CQ + off, HD)]
        q = _silu(cq)
        v = z_ref[:, pl.ds(OFF_CI + off, HD)]
        qs[...] = q
        ks[...] = k
        vs[...] = v
        bs[...] = b
        st_h = st[h]
        b_last = b[chunk - 1:chunk, :]
        o = _nt_dot((q * jnp.exp(b)).astype(BF), st_h.astype(BF))
        kdec = k * jnp.exp(b_last - b)
        st[h] = st_h * jnp.exp(b_last) + _tn_dot(v.astype(BF), kdec.astype(BF))

        def sub_start(i):
            return bs[i * SUB - 1:i * SUB, :] if i else jnp.zeros((1, HD), F32)

        def intra_direct():
            att[0:SUB, :] = jnp.zeros((SUB, chunk), F32)
            for i in range(1, nsub):
                lo, hi = i * SUB, (i + 1) * SUB
                q_i = qs[lo:hi, :] * jnp.exp(bs[lo:hi, :] - sub_start(i))
                k_i = k * jnp.exp(jnp.minimum(sub_start(i) - b, 0.0))
                a_i = _nt_dot(q_i.astype(BF), k_i.astype(BF))
                att[lo:hi, :] = jnp.where(cols_a < lo, a_i, 0.0)
            o_off = jnp.dot(att[...].astype(BF), v.astype(BF), preferred_element_type=F32)
            o_parts = []
            for i in range(nsub):
                q_i = qs[i * SUB:(i + 1) * SUB, :]
                b_i = bs[i * SUB:(i + 1) * SUB, :]
                o_i = jnp.zeros((SUB, HD), F32)
                for s in range(SUB):
                    r0 = i * SUB + s
                    p = q_i * ks[r0:r0 + 1, :] * jnp.exp(jnp.minimum(b_i - bs[r0:r0 + 1, :], 0.0))
                    a_col = jnp.sum(p, axis=-1, keepdims=True)
                    a_col = jnp.where(sub_rows >= s, a_col, 0.0)
                    o_i = o_i + a_col * vs[r0:r0 + 1, :]
                o_parts.append(o_i)
            return o_off + jnp.concatenate(o_parts, axis=0)

        o = o + intra_direct()
        oc[:, pl.ds(off, HD)] = o * lax.rsqrt(jnp.mean(o * o, axis=-1, keepdims=True) + RMS_EPS) * ng_ref[...]
        return carry

    def hgrn_factored():
        cq = z_ref[:, OFF_CQ:OFF_CQ + D]
        qsil[...] = _silu(cq)
        b_last = bf[chunk - 1:chunk, :]
        qe[...] = (qsil[...] * jnp.exp(bf[...])).astype(BF)
        kd[...] = (kf[...] * jnp.exp(b_last - bf[...])).astype(BF)
        for i in range(nsub):
            lo, hi = i * SUB, (i + 1) * SUB
            start = bf[lo - 1:lo, :] if i else 0.0
            q_i = (qsil[lo:hi, :] * jnp.exp(bf[lo:hi, :] - start)).astype(BF)
            k_i = (kf[0:hi, :] * jnp.exp(jnp.minimum(start - bf[0:hi, :], INTRA_CAP))).astype(BF)
            causal = (lax.broadcasted_iota(jnp.int32, (SUB, hi), 1)
                      <= lax.broadcasted_iota(jnp.int32, (SUB, hi), 0) + lo)
            for h in range(HEADS):
                hs = slice(h * HD, (h + 1) * HD)
                att_all[h, lo:hi, 0:hi] = jnp.where(causal, _nt_dot(q_i[:, hs], k_i[:, hs]), 0.0)
        for h in range(HEADS):
            hs = slice(h * HD, (h + 1) * HD)
            v_h = z_ref[:, OFF_CI + h * HD:OFF_CI + (h + 1) * HD].astype(BF)
            st_h = st[h]
            o = (_nt_dot(qe[:, hs], st_h.astype(BF))
                 + jnp.dot(att_all[h].astype(BF), v_h, preferred_element_type=F32))
            st[h] = st_h * jnp.exp(b_last[:, hs]) + _tn_dot(v_h, kd[:, hs])
            oc[:, hs] = o * lax.rsqrt(jnp.mean(o * o, axis=-1, keepdims=True) + RMS_EPS) * ng_ref[...]

    def hgrn_direct():
        lax.fori_loop(0, HEADS, head_step, 0)

    lax.cond(intra_bound <= INTRA_CAP, hgrn_factored, hgrn_direct)

    @pl.when(c == last)
    def _():
        for h in range(HEADS):
            ns_ref[h] = st[h].T

    cg = z_ref[:, OFF_CG:OFF_CG + D]
    y_c = jnp.dot((oc[...] * _silu(cg)).astype(BF), woc_ref[...], preferred_element_type=F32)
    merged2 = merged2 + (1.0 + _tanh_half(z_ref[:, OFF_GC:OFF_GC + D])) * y_c
    res = ALPHA * x_ref[...] + jnp.dot((0.5 * merged2).astype(BF), wo_ref[...], preferred_element_type=F32)
    x1_ref[...] = _layer_norm(res, g1_ref[...], b1_ref[...])


def _mixer(x, z, past_a, past_b, h0, s0, lb, lw):
    bsz, t_len, _ = x.shape
    chunk = min(t_len, 128)
    assert t_len % chunk == 0 and chunk % SUB == 0
    nc = t_len // chunk
    full2 = lambda shape: pl.BlockSpec(shape, lambda b, c: (0, 0))
    full3 = lambda shape: pl.BlockSpec(shape, lambda b, c: (0, 0, 0))
    in_specs = [
        pl.BlockSpec((None, chunk, D), lambda b, c: (b, c, 0)),
        pl.BlockSpec((None, chunk, IN_COLS), lambda b, c: (b, c, 0)),
        pl.BlockSpec((None, CONV_A_W - 1, D), lambda b, c: (b, 0, 0)),
        pl.BlockSpec((None, CONV_B_W - 1, D), lambda b, c: (b, 0, 0)),
        pl.BlockSpec((None, 1, D), lambda b, c: (b, 0, 0)),
        pl.BlockSpec((None, HEADS, HD, HD), lambda b, c: (b, 0, 0, 0)),
        full2((1, D)),
        full2((CONV_A_W, D)), full2((CONV_B_W, D)), full2((1, D)),
        full3((LRU_BLOCKS, LRU_BW, LRU_BW)), full2((1, D)),
        full3((LRU_BLOCKS, LRU_BW, LRU_BW)), full2((1, D)),
        full2((1, D)), full2((1, HD)),
        full2((D, D)), full2((D, D)), full2((D, D)), full2((D, D)),
        full2((1, D)), full2((1, D)),
    ]
    out_specs = [
        pl.BlockSpec((None, chunk, D), lambda b, c: (b, c, 0)),
        pl.BlockSpec((None, CONV_A_W - 1, D), lambda b, c: (b, 0, 0)),
        pl.BlockSpec((None, CONV_B_W - 1, D), lambda b, c: (b, 0, 0)),
        pl.BlockSpec((None, 1, D), lambda b, c: (b, 0, 0)),
        pl.BlockSpec((None, HEADS, HD, HD), lambda b, c: (b, 0, 0, 0)),
    ]
    out_shape = [
        jax.ShapeDtypeStruct((bsz, t_len, D), F32),
        jax.ShapeDtypeStruct((bsz, CONV_A_W - 1, D), F32),
        jax.ShapeDtypeStruct((bsz, CONV_B_W - 1, D), F32),
        jax.ShapeDtypeStruct((bsz, 1, D), F32),
        jax.ShapeDtypeStruct((bsz, HEADS, HD, HD), F32),
    ]
    scratch = [
        pltpu.VMEM((SUBLANES, D), F32),
        pltpu.VMEM((SUBLANES, D), F32),
        pltpu.VMEM((SUBLANES, D), F32),
        pltpu.VMEM((HEADS, HD, HD), F32),
        pltpu.VMEM((chunk, HD), F32),
        pltpu.VMEM((chunk, HD), F32),
        pltpu.VMEM((chunk, HD), F32),
        pltpu.VMEM((chunk, HD), F32),
        pltpu.VMEM((chunk, chunk), F32),
        pltpu.VMEM((chunk, D), F32),
        pltpu.VMEM((chunk, D), F32),
        pltpu.VMEM((chunk, D), F32),
        pltpu.VMEM((chunk, D), F32),
        pltpu.VMEM((chunk, D), BF),
        pltpu.VMEM((chunk, D), BF),
        pltpu.VMEM((HEADS, chunk, chunk), F32),
    ]
    return pl.pallas_call(
        functools.partial(_mixer_body, chunk=chunk),
        grid=(bsz, nc),
        in_specs=in_specs, out_specs=out_specs, out_shape=out_shape, scratch_shapes=scratch,
        compiler_params=pltpu.CompilerParams(
            dimension_semantics=("arbitrary", "arbitrary"), vmem_limit_bytes=VMEM_LIMIT),
        name="mixer",
    )(x, z, past_a, past_b, h0, s0, lb, lw["conv_a_w"], lw["conv_b_w"], lw["conv_b_b"],
      lw["lru_wa"], lw["lru_ba"], lw["lru_wx"], lw["lru_bx"], lw["lru_lambda"], lw["hgrn_norm_g"],
      lw["w_out_a"], lw["w_out_b"], lw["w_out_c"], lw["w_o"], lw["ln1_g"], lw["ln1_b"])


def _sort_network(n):
    pairs = []

    def merge(lo, m, r):
        step = 2 * r
        if step < m:
            merge(lo, m, step)
            merge(lo + r, m, step)
            pairs.extend((i, i + r) for i in range(lo + r, lo + m - r, step))
        else:
            pairs.append((lo, lo + r))

    def sort(lo, m):
        if m > 1:
            sort(lo, m // 2)
            sort(lo + m // 2, m // 2)
            merge(lo, m, 1)

    sort(0, n)
    return pairs


_SORT16 = _sort_network(TOPK)


def _cmp_exchange(v, i, j):
    if v[j] is None:
        return
    if v[i] is None:
        v[i], v[j] = v[j], None
        return
    v[i], v[j] = jnp.maximum(v[i], v[j]), jnp.minimum(v[i], v[j])


def _sort_desc(v):
    v = list(v) + [None] * (TOPK - len(v))
    for i, j in _SORT16:
        _cmp_exchange(v, i, j)
    return v


def _bitonic_to_desc(v):
    v = list(v)
    d = TOPK // 2
    while d >= 1:
        for i in range(TOPK):
            if i & d == 0:
                _cmp_exchange(v, i, i + d)
        d //= 2
    return v


def _merge_rolled(v, shift):
    w = [None if x is None else pltpu.roll(x, shift, 0) for x in v]
    out = []
    for i in range(TOPK):
        a, b = v[i], w[TOPK - 1 - i]
        out.append(b if a is None else a if b is None else jnp.maximum(a, b))
    return out


def _top16_of_rows(vregs):
    v = _sort_desc(vregs)
    for shift in (4, 2, 1):
        v = _bitonic_to_desc(_merge_rolled(v, shift))
    return v


def _kth16_of_rows(vregs):
    v = _sort_desc(vregs)
    for shift in (4, 2):
        v = _bitonic_to_desc(_merge_rolled(v, shift))
    return functools.reduce(jnp.minimum, _merge_rolled(v, 1))


def _pack_rows(vs, sub):
    out = vs[0]
    for j in range(1, SUBLANES):
        out = jnp.where(sub == j, vs[j], out)
    return out


def _scores_body(x_ref, wq_ref, keys_ref, xt_ref, p1_ref, p2_ref, th_ref, qbuf, *, tn):
    xt_ref[...] = x_ref[...].T.astype(BF)
    qbuf[...] = jnp.dot(x_ref[...].astype(BF), wq_ref[...], preferred_element_type=F32).astype(BF)
    tw = SCORE_TOKENS
    sub = lax.broadcasted_iota(jnp.int32, (SUBLANES, tw), 0)
    n_tc = tn // tw

    def step(idx, carry):
        h = idx // n_tc
        tc = idx % n_tc
        t0 = pl.multiple_of(tc * tw, tw)
        c0 = pl.multiple_of(h * (2 * PEER_DH), 2 * PEER_DH)
        s1 = _nt_dot(keys_ref[h, 0], qbuf[pl.ds(t0, tw), pl.ds(c0, PEER_DH)])
        s2 = _nt_dot(keys_ref[h, 1], qbuf[pl.ds(t0, tw), pl.ds(c0 + PEER_DH, PEER_DH)])
        split = lambda s: [s[r * SUBLANES:(r + 1) * SUBLANES, :] for r in range(N_KEYS // SUBLANES)]
        a = _top16_of_rows(split(s1))
        b = _top16_of_rows(split(s2))
        pa = [jnp.exp(x - a[0]) for x in a]
        pb = [jnp.exp(x - b[0]) for x in b]
        pb_lo, pb_hi = _pack_rows(pb[:SUBLANES], sub), _pack_rows(pb[SUBLANES:], sub)
        pa_hi = _pack_rows(pa[SUBLANES:], sub)
        lhs = [pa[0], pa[0]] + pa[1:SUBLANES] + [pa_hi]
        rhs = [pb_lo, pb_hi] + [pb_lo] * (SUBLANES - 1) + [pb[0]]
        t = [x * y for x, y in zip(lhs, rhs)]
        t16 = _kth16_of_rows(t)
        keep = [x >= t16 for x in t]
        z_sum = jnp.sum(sum(jnp.where(m, x, 0.0) for m, x in zip(keep, t)), axis=0, keepdims=True)
        rz = 0.5 / z_sum
        gates = [(x * rz) * y for x, y in zip(lhs, rhs)]
        th = jnp.min(functools.reduce(jnp.minimum, [jnp.where(m, g, jnp.inf) for m, g in zip(keep, gates)]),
                     axis=0, keepdims=True)
        p1_ref[h, :, pl.ds(t0, tw)] = jnp.exp(s1 - a[0][0:1]) * rz
        p2 = jnp.exp(s2 - b[0][0:1])
        for part in range(tw // LANES):
            p2_ref[h, tc * (tw // LANES) + part] = p2[:, part * LANES:(part + 1) * LANES]
        th_ref[h, :, pl.ds(t0, tw)] = th
        return carry

    lax.fori_loop(0, PEER_HEADS * n_tc, step, 0)


def _peer_scores(x2d, wq_bf, keys_bf, tn):
    n = x2d.shape[0]
    n_tc = tn // LANES
    return pl.pallas_call(
        functools.partial(_scores_body, tn=tn),
        grid=(n // tn,),
        in_specs=[pl.BlockSpec((tn, D), lambda i: (i, 0)),
                  pl.BlockSpec((D, 2 * PEER_HEADS * PEER_DH), lambda i: (0, 0)),
                  pl.BlockSpec((PEER_HEADS, 2, N_KEYS, PEER_DH), lambda i: (0, 0, 0, 0))],
        out_specs=[pl.BlockSpec((D, tn), lambda i: (0, i)),
                   pl.BlockSpec((PEER_HEADS, N_KEYS, tn), lambda i: (0, 0, i)),
                   pl.BlockSpec((PEER_HEADS, n_tc, N_KEYS, LANES), lambda i: (0, i, 0, 0)),
                   pl.BlockSpec((PEER_HEADS, 1, tn), lambda i: (0, 0, i))],
        out_shape=[jax.ShapeDtypeStruct((D, n), BF),
                   jax.ShapeDtypeStruct((PEER_HEADS, N_KEYS, n), F32),
                   jax.ShapeDtypeStruct((PEER_HEADS, n // LANES, N_KEYS, LANES), F32),
                   jax.ShapeDtypeStruct((PEER_HEADS, 1, n), F32)],
        scratch_shapes=[pltpu.VMEM((tn, 2 * PEER_HEADS * PEER_DH), BF)],
        compiler_params=pltpu.CompilerParams(
            dimension_semantics=("arbitrary",), vmem_limit_bytes=VMEM_LIMIT),
        name="peer_scores",
    )(x2d, wq_bf, keys_bf)


def _dense_body(xt_ref, x_ref, u0_ref, un_ref, vp_ref, p1_ref, p2_ref, th_ref, g2_ref, b2_ref,
                y_ref, acc, h_even, h_odd, g_even, g_odd, *, tn, eb, nb):
    j = pl.program_id(1)
    kb = eb // N_KEYS
    n_tc = tn // LANES

    def matmul_jobs(jobs):
        assert len(jobs) % 2 == 0
        for p in range(0, len(jobs), 2):
            pair = jobs[p:p + 2]
            base = (p // 2 % 2) * (MXU_ROWS // 4)
            for k in range(D // MXU_TILE):
                ks = slice(k * MXU_TILE, (k + 1) * MXU_TILE)
                for q, (lhs, rhs, r0, c0, _) in enumerate(pair):
                    pltpu.matmul_push_rhs(rhs[ks, c0:c0 + MXU_TILE], staging_register=k % 2, mxu_index=q)
                shared_lhs = pair[0][0] is pair[1][0] and pair[0][2] == pair[1][2]
                for m in range(MXU_ROWS // MXU_STEP):
                    piece = None
                    for q, (lhs, rhs, r0, c0, _) in enumerate(pair):
                        if piece is None or not shared_lhs:
                            piece = lhs[r0 + m * MXU_STEP:r0 + (m + 1) * MXU_STEP, ks]
                        pltpu.matmul_acc_lhs(base + m * (MXU_STEP // 4), piece, q,
                                             load_staged_rhs=(k % 2) if m == 0 else None)
                    yield
            for q, (lhs, rhs, r0, c0, sink) in enumerate(pair):
                sink(r0, c0, pltpu.matmul_pop(base, (MXU_ROWS, MXU_TILE), F32, q))

    def tiles(lhs, rhs, sink):
        return [(lhs, rhs, r0, c0, sink) for r0 in range(0, eb, MXU_ROWS) for c0 in range(0, tn, MXU_TILE)]

    def store_into(dst):
        def sink(r0, c0, val):
            dst[r0:r0 + MXU_ROWS, c0:c0 + MXU_TILE] = val
        return sink

    def add_into_acc(r0, c0, val):
        acc[r0:r0 + MXU_ROWS, c0:c0 + MXU_TILE] += val

    def run(*programs):
        live = list(programs)
        while live:
            for g in list(live):
                if next(g, StopIteration) is StopIteration:
                    live.remove(g)

    @pl.when(j == 0)
    def _():
        acc[...] = jnp.zeros((D, tn), F32)
        g_odd[:, 0:tn] = jnp.zeros((eb, tn), BF)
        run(matmul_jobs(tiles(u0_ref, xt_ref, store_into(h_even))))

    def gate_build(h_cur, g_cur):
        for k0 in range(0, kb, GATE_KEYS):
            for tc in range(n_tc):
                for half in range(N_KEYS // GATE_ROWS):
                    lanes = slice(tc * LANES, (tc + 1) * LANES)
                    sub_rows = slice(half * GATE_ROWS, (half + 1) * GATE_ROWS)
                    w = [None] * GATE_KEYS
                    for h in range(PEER_HEADS):
                        p2 = p2_ref[h, tc, sub_rows, :]
                        th = th_ref[h, :, lanes]
                        for a in range(GATE_KEYS):
                            prod = p1_ref[h, k0 + a:k0 + a + 1, lanes] * p2
                            kept = jnp.where(prod >= th, prod, 0.0)
                            w[a] = kept if h == 0 else w[a] + kept
                            if a % GATE_YIELD == GATE_YIELD - 1:
                                yield
                    for a in range(GATE_KEYS):
                        r0 = (k0 + a) * N_KEYS + half * GATE_ROWS
                        g_cur[r0:r0 + GATE_ROWS, lanes] = (
                            w[a] * _gelu_twice(h_cur[r0:r0 + GATE_ROWS, lanes])).astype(BF)

    def block_step(h_cur, h_next, g_cur, g_prev):
        run(matmul_jobs(tiles(un_ref, xt_ref, store_into(h_next)) + tiles(vp_ref, g_prev, add_into_acc)),
            gate_build(h_cur, g_cur))

    @pl.when((j < nb) & (j % 2 == 0))
    def _():
        block_step(h_even, h_odd, g_even, g_odd)

    @pl.when((j < nb) & (j % 2 == 1))
    def _():
        block_step(h_odd, h_even, g_odd, g_even)

    @pl.when(j == nb)
    def _():
        g_last = g_even if (nb - 1) % 2 == 0 else g_odd

        run(matmul_jobs(tiles(vp_ref, g_last, add_into_acc)))
        y_ref[...] = _layer_norm(ALPHA * x_ref[...] + acc[...].T, g2_ref[...], b2_ref[...])


def _peer_dense(xt_bf, x2d, u_bf, vt_bf, p1, p2, th, g2, b2, tn):
    n = x2d.shape[0]
    eb = D
    nb = N_EXPERTS // eb
    n_tc = tn // LANES
    assert eb // N_KEYS == SUBLANES and (eb // N_KEYS) % GATE_KEYS == 0
    once = pl.Buffered(1)
    return pl.pallas_call(
        functools.partial(_dense_body, tn=tn, eb=eb, nb=nb),
        grid=(n // tn, nb + 1),
        in_specs=[pl.BlockSpec((D, tn), lambda i, j: (0, i), pipeline_mode=once),
                  pl.BlockSpec((tn, D), lambda i, j: (i, 0), pipeline_mode=once),
                  pl.BlockSpec((eb, D), lambda i, j: (0, 0), pipeline_mode=once),
                  pl.BlockSpec((eb, D), lambda i, j: (jnp.minimum(j + 1, nb - 1), 0)),
                  pl.BlockSpec((None, D, eb), lambda i, j: (jnp.maximum(j - 1, 0), 0, 0)),
                  pl.BlockSpec((PEER_HEADS, eb // N_KEYS, tn),
                               lambda i, j: (0, jnp.minimum(j, nb - 1), i)),
                  pl.BlockSpec((PEER_HEADS, n_tc, N_KEYS, LANES), lambda i, j: (0, i, 0, 0),
                               pipeline_mode=once),
                  pl.BlockSpec((PEER_HEADS, 1, tn), lambda i, j: (0, 0, i), pipeline_mode=once),
                  pl.BlockSpec((1, D), lambda i, j: (0, 0), pipeline_mode=once),
                  pl.BlockSpec((1, D), lambda i, j: (0, 0), pipeline_mode=once)],
        out_specs=pl.BlockSpec((tn, D), lambda i, j: (i, 0)),
        out_shape=jax.ShapeDtypeStruct((n, D), F32),
        scratch_shapes=[pltpu.VMEM((D, tn), F32),
                        pltpu.VMEM((eb, tn + LANES), F32), pltpu.VMEM((eb, tn + LANES), F32),
                        pltpu.VMEM((eb, tn + LANES), BF), pltpu.VMEM((eb, tn + LANES), BF)],
        compiler_params=pltpu.CompilerParams(
            dimension_semantics=("arbitrary", "arbitrary"), vmem_limit_bytes=VMEM_LIMIT),
        name="peer_dense",
    )(xt_bf, x2d, u_bf, u_bf, vt_bf, p1, p2, th, g2, b2)


def _trunk_layer(x, past_a, past_b, h0, s0, lb, lw):
    bsz, t_len, _ = x.shape
    n = bsz * t_len
    z = _in_proj(x.reshape(n, D), lw["w_in"], lw["b_in"]).reshape(bsz, t_len, IN_COLS)
    x1, new_a, new_b, h_t, s_t = _mixer(x, z, past_a, past_b, h0, s0, lb, lw)
    x1 = x1.reshape(n, D)
    tn = min(n, 512)
    xt, p1, p2, th = _peer_scores(x1, lw["peer_wq"], lw["peer_keys"], tn)
    x2 = _peer_dense(xt, x1, lw["peer_u"], lw["peer_vt"], p1, p2, th, lw["ln2_g"], lw["ln2_b"],
                     min(n, DENSE_TOKENS))
    return x2.reshape(bsz, t_len, D), new_a, new_b, h_t.reshape(bsz, D), s_t


def kernel(x_prompt, x_sample, state_conv_a, state_conv_b, state_lru, state_hgrn, w_in, b_in, conv_a_w,
           conv_b_w, conv_b_b, lru_wa, lru_ba, lru_wx, lru_bx, lru_lambda, hgrn_lb_logits, hgrn_norm_g,
           w_out_a, w_out_b, w_out_c, w_o, ln1_g, ln1_b, peer_wq, peer_keys, peer_u, peer_v, ln2_g,
           ln2_b):
    p = jax.nn.softmax(hgrn_lb_logits.astype(F32), axis=0)
    lbs = jnp.cumsum(p, axis=0) - p[0]
    bp = x_prompt.shape[0]
    xp, xs = x_prompt, x_sample
    outs_p, outs_s = [], []
    for l in range(DEPTH):
        row = lambda a: a[l].reshape(1, -1)
        lw = dict(
            w_in=w_in[l].astype(BF), b_in=row(b_in),
            conv_a_w=conv_a_w[l], conv_b_w=conv_b_w[l], conv_b_b=row(conv_b_b),
            lru_wa=lru_wa[l].astype(BF), lru_ba=row(lru_ba), lru_wx=lru_wx[l].astype(BF),
            lru_bx=row(lru_bx), lru_lambda=row(lru_lambda), hgrn_norm_g=row(hgrn_norm_g),
            w_out_a=w_out_a[l].astype(BF), w_out_b=w_out_b[l].astype(BF),
            w_out_c=w_out_c[l].astype(BF), w_o=w_o[l].astype(BF),
            ln1_g=row(ln1_g), ln1_b=row(ln1_b),
            peer_wq=peer_wq[l].astype(BF), peer_keys=peer_keys[l].astype(BF),
            peer_u=peer_u[l].astype(BF),
            peer_vt=peer_v[l].reshape(N_EXPERTS // D, D, D).transpose(0, 2, 1).astype(BF),
            ln2_g=row(ln2_g), ln2_b=row(ln2_b))
        lb = lbs[l].reshape(1, D)
        xp, pa, pb, ph, ps = _trunk_layer(
            xp, jnp.zeros((bp, CONV_A_W - 1, D), F32), jnp.zeros((bp, CONV_B_W - 1, D), F32),
            jnp.zeros((bp, 1, D), F32), jnp.zeros((bp, HEADS, HD, HD), F32), lb, lw)
        xs, sa, sb, sh, ss = _trunk_layer(
            xs, state_conv_a[l], state_conv_b[l], state_lru[l][:, None, :], state_hgrn[l], lb, lw)
        outs_p.append((pa, pb, ph, ps))
        outs_s.append((sa, sb, sh, ss))
    stack = lambda outs, k: jnp.stack([o[k] for o in outs])
    return (xp, xs,
            stack(outs_p, 0), stack(outs_p, 1), stack(outs_p, 2), stack(outs_p, 3),
            stack(outs_s, 0), stack(outs_s, 1), stack(outs_s, 2), stack(outs_s, 3))
```

```python
import functools

import jax
import jax.numpy as jnp
from jax import lax
from jax.experimental import pallas as pl
from jax.experimental.pallas import tpu as pltpu

D = 1024
DEPTH = 2
CONV_A_W = 3
CONV_B_W = 4
LRU_BLOCKS = 8
LRU_BW = D // LRU_BLOCKS
LRU_C = 8.0
HEADS = 8
HD = D // HEADS
PEER_HEADS = 8
PEER_DH = 128
N_KEYS = 128
N_EXPERTS = N_KEYS * N_KEYS
TOPK = 16
ALPHA = (2.0 * DEPTH) ** 0.25
LN_EPS = 1e-5
RMS_EPS = 1e-6
IN_COLS = 12 * D
SUB = 16
INTRA_CAP = 60.0
LANES = 128
SUBLANES = 8
VMEM_LIMIT = 56 * 1024 * 1024
GATE_KEYS = 4
GATE_ROWS = 64
SCORE_TOKENS = 256
DENSE_TOKENS = 1024
MXU_TILE = 256
MXU_ROWS = 512
MXU_STEP = 32
GATE_YIELD = 2

OFF_AB, OFF_AC, OFF_AX, OFF_BX, OFF_BG, OFF_CQ, OFF_CF, OFF_CI, OFF_CG, OFF_GA, OFF_GB, OFF_GC = (
    i * D for i in range(12))

BF = jnp.bfloat16
F32 = jnp.float32


def _tanh_half(x):
    return jnp.tanh(0.5 * x)


def _sigmoid(x):
    return 0.5 * _tanh_half(x) + 0.5


def _silu(x):
    h = 0.5 * x
    return h + h * jnp.tanh(h)


def _gelu_twice(x):
    return x * (1.0 + lax.erf(x * (2.0 ** -0.5)))


def _gelu(x):
    return 0.5 * _gelu_twice(x)


def _layer_norm(x, g, b):
    mu = jnp.mean(x, axis=-1, keepdims=True)
    xc = x - mu
    var = jnp.mean(xc * xc, axis=-1, keepdims=True)
    return xc * lax.rsqrt(var + LN_EPS) * g + b


def _nt_dot(a, b):
    return lax.dot_general(a, b, (((1,), (1,)), ((), ())), preferred_element_type=F32)


def _tn_dot(a, b):
    return lax.dot_general(a, b, (((0,), (0,)), ((), ())), preferred_element_type=F32)


def _scan_rows(a, u, carry, rows8):
    out = []
    for g in range(u.shape[0] // SUBLANES):
        rs = slice(g * SUBLANES, (g + 1) * SUBLANES)
        u_g = u[rs, :]
        a_g = None if a is None else a[rs, :]
        d = 1
        while d < SUBLANES:
            keep = rows8 >= d
            if a_g is None:
                u_g = u_g + jnp.where(keep, pltpu.roll(u_g, d, 0), 0.0)
            else:
                u_g = a_g * jnp.where(keep, pltpu.roll(u_g, d, 0), 0.0) + u_g
                a_g = a_g * jnp.where(keep, pltpu.roll(a_g, d, 0), 1.0)
            d *= 2
        h_g = u_g + carry if a_g is None else u_g + a_g * carry
        out.append(h_g)
        carry = h_g[SUBLANES - 1:SUBLANES, :]
    return jnp.concatenate(out, axis=0)


def _inproj_body(x_ref, w_ref, b_ref, o_ref):
    x = x_ref[...].astype(BF)
    o_ref[...] = jnp.dot(x, w_ref[...], preferred_element_type=F32) + b_ref[...]


def _in_proj(x2d, w_bf, b_row, layer):
    n = x2d.shape[0]
    tm = min(n, 1024)
    tn = 2048
    return pl.pallas_call(
        _inproj_body,
        grid=(n // tm, IN_COLS // tn),
        in_specs=[pl.BlockSpec((tm, D), lambda i, j: (i, 0)),
                  pl.BlockSpec((None, D, tn), lambda i, j: (layer, 0, j)),
                  pl.BlockSpec((1, tn), lambda i, j: (0, j))],
        out_specs=pl.BlockSpec((tm, tn), lambda i, j: (i, j)),
        out_shape=jax.ShapeDtypeStruct((n, IN_COLS), F32),
        compiler_params=pltpu.CompilerParams(
            dimension_semantics=("arbitrary", "arbitrary"), vmem_limit_bytes=VMEM_LIMIT),
        name="in_proj",
    )(x2d, w_bf, b_row)


def _mixer_body(x_ref, z_ref, pa_ref, pb_ref, h0_ref, s0_ref, lb_ref, cwa_ref, cwb_ref, cbb_ref,
                wa_ref, ba_ref, wx_ref, bx_ref, lam_ref, ng_ref, woa_ref, wob_ref, woc_ref, wo_ref,
                g1_ref, b1_ref,
                x1_ref, na_ref, nb_ref, nh_ref, ns_ref,
                ua, ub, hc, st, qs, ks, vs, bs, att, oc, kf, bf, qsil, qe, kd, att_all, *, chunk):
    c = pl.program_id(1)
    last = pl.num_programs(1) - 1
    nsub = chunk // SUB

    rows8 = lax.broadcasted_iota(jnp.int32, (SUBLANES, D), 0)

    def causal_conv(slab, w_ref, width, prev):
        out = []
        below = [pltpu.roll(prev, d, 0) for d in range(1, width)]
        for g in range(chunk // SUBLANES):
            x_g = slab(g)
            y = w_ref[width - 1:width, :] * x_g
            for d in range(1, width):
                cur = pltpu.roll(x_g, d, 0)
                y = y + w_ref[width - 1 - d:width - d, :] * jnp.where(rows8 < d, below[d - 1], cur)
                below[d - 1] = cur
            out.append(y)
        return jnp.concatenate(out, axis=0), x_g

    @pl.when(c == 0)
    def _():
        ua[...] = jnp.zeros((SUBLANES, D), F32)
        ub[...] = jnp.zeros((SUBLANES, D), F32)
        ua[SUBLANES - 2:SUBLANES, :] = pa_ref[...]
        ub[SUBLANES - 3:SUBLANES, :] = pb_ref[...]
        hc[0:1, :] = h0_ref[...]
        for h in range(HEADS):
            st[h] = s0_ref[h].T
        att_all[...] = jnp.zeros((HEADS, chunk, chunk), F32)

    def slab_a(g):
        rs = slice(g * SUBLANES, (g + 1) * SUBLANES)
        return z_ref[rs, OFF_AC:OFF_AC + D] * z_ref[rs, OFF_AX:OFF_AX + D]

    conv_a, last_a = causal_conv(slab_a, cwa_ref, CONV_A_W, ua[...])
    na_ref[...] = last_a[SUBLANES - (CONV_A_W - 1):SUBLANES, :]
    ua[...] = last_a
    y_a = jnp.dot((z_ref[:, OFF_AB:OFF_AB + D] * conv_a).astype(BF), woa_ref[...],
                  preferred_element_type=F32)
    merged2 = (1.0 + _tanh_half(z_ref[:, OFF_GA:OFF_GA + D])) * y_a

    conv_b, last_b = causal_conv(
        lambda g: z_ref[g * SUBLANES:(g + 1) * SUBLANES, OFF_BX:OFF_BX + D], cwb_ref, CONV_B_W, ub[...])
    xl = conv_b + cbb_ref[...]
    nb_ref[...] = last_b[SUBLANES - (CONV_B_W - 1):SUBLANES, :]
    ub[...] = last_b

    xl_bf = xl.astype(BF)
    ra = jnp.concatenate(
        [jnp.dot(xl_bf[:, n * LRU_BW:(n + 1) * LRU_BW], wa_ref[n], preferred_element_type=F32)
         for n in range(LRU_BLOCKS)], axis=1)
    rx = jnp.concatenate(
        [jnp.dot(xl_bf[:, n * LRU_BW:(n + 1) * LRU_BW], wx_ref[n], preferred_element_type=F32)
         for n in range(LRU_BLOCKS)], axis=1)
    gi = _sigmoid(rx + bx_ref[...])
    nlam = -lam_ref[...]
    softplus = jnp.maximum(nlam, 0.0) + jnp.log1p(jnp.exp(-jnp.abs(nlam)))
    log_a = ((-0.5 * LRU_C) * softplus) * (1.0 + _tanh_half(ra + ba_ref[...]))
    a = jnp.exp(log_a)
    uu = jnp.sqrt(jnp.tanh(-log_a) * (a * a + 1.0)) * (gi * xl)
    hh = _scan_rows(a, uu, hc[0:1, :], rows8)
    h_last = hh[chunk - 1:chunk, :]
    hc[0:1, :] = h_last
    nh_ref[...] = h_last
    y_b = jnp.dot((_gelu(z_ref[:, OFF_BG:OFF_BG + D]) * hh).astype(BF), wob_ref[...],
                  preferred_element_type=F32)
    merged2 = merged2 + (1.0 + _tanh_half(z_ref[:, OFF_GB:OFF_GB + D])) * y_b

    cols_a = lax.broadcasted_iota(jnp.int32, (SUB, chunk), 1)
    sub_rows = lax.broadcasted_iota(jnp.int32, (SUB, 1), 0)
    f_all = (0.5 * (1.0 + lb_ref[...])) + (0.5 * (1.0 - lb_ref[...])) * _tanh_half(z_ref[:, OFF_CF:OFF_CF + D])
    kf[...] = 1.0 - f_all
    bf[...] = _scan_rows(None, jnp.log(f_all), jnp.zeros((1, D), F32), rows8)
    intra_bound = jnp.max(functools.reduce(jnp.maximum, [
        (bf[i * SUB - 1:i * SUB, :] if i else 0.0) - bf[(i + 1) * SUB - 1:(i + 1) * SUB, :]
        for i in range(nsub)]))

    def head_step(h, carry):
        off = pl.multiple_of(h * HD, HD)
        k = kf[:, pl.ds(off, HD)]
        b = bf[:, pl.ds(off, HD)]
        cq = z_ref[:, pl.ds(OFF_CQ + off, HD)]
        q = _silu(cq)
        v = z_ref[:, pl.ds(OFF_CI + off, HD)]
        qs[...] = q
        ks[...] = k
        vs[...] = v
        bs[...] = b
        st_h = st[h]
        b_last = b[chunk - 1:chunk, :]
        o = _nt_dot((q * jnp.exp(b)).astype(BF), st_h.astype(BF))
        kdec = k * jnp.exp(b_last - b)
        st[h] = st_h * jnp.exp(b_last) + _tn_dot(v.astype(BF), kdec.astype(BF))

        def sub_start(i):
            return bs[i * SUB - 1:i * SUB, :] if i else jnp.zeros((1, HD), F32)

        def intra_direct():
            att[0:SUB, :] = jnp.zeros((SUB, chunk), F32)
            for i in range(1, nsub):
                lo, hi = i * SUB, (i + 1) * SUB
                q_i = qs[lo:hi, :] * jnp.exp(bs[lo:hi, :] - sub_start(i))
                k_i = k * jnp.exp(jnp.minimum(sub_start(i) - b, 0.0))
                a_i = _nt_dot(q_i.astype(BF), k_i.astype(BF))
                att[lo:hi, :] = jnp.where(cols_a < lo, a_i, 0.0)
            o_off = jnp.dot(att[...].astype(BF), v.astype(BF), preferred_element_type=F32)
            o_parts = []
            for i in range(nsub):
                q_i = qs[i * SUB:(i + 1) * SUB, :]
                b_i = bs[i * SUB:(i + 1) * SUB, :]
                o_i = jnp.zeros((SUB, HD), F32)
                for s in range(SUB):
                    r0 = i * SUB + s
                    p = q_i * ks[r0:r0 + 1, :] * jnp.exp(jnp.minimum(b_i - bs[r0:r0 + 1, :], 0.0))
                    a_col = jnp.sum(p, axis=-1, keepdims=True)
                    a_col = jnp.where(sub_rows >= s, a_col, 0.0)
                    o_i = o_i + a_col * vs[r0:r0 + 1, :]
                o_parts.append(o_i)
            return o_off + jnp.concatenate(o_parts, axis=0)

        o = o + intra_direct()
        oc[:, pl.ds(off, HD)] = o * lax.rsqrt(jnp.mean(o * o, axis=-1, keepdims=True) + RMS_EPS) * ng_ref[...]
        return carry

    def hgrn_factored():
        cq = z_ref[:, OFF_CQ:OFF_CQ + D]
        qsil[...] = _silu(cq)
        b_last = bf[chunk - 1:chunk, :]
        qe[...] = (qsil[...] * jnp.exp(bf[...])).astype(BF)
        kd[...] = (kf[...] * jnp.exp(b_last - bf[...])).astype(BF)
        for i in range(nsub):
            lo, hi = i * SUB, (i + 1) * SUB
            start = bf[lo - 1:lo, :] if i else 0.0
            q_i = (qsil[lo:hi, :] * jnp.exp(bf[lo:hi, :] - start)).astype(BF)
            k_i = (kf[0:hi, :] * jnp.exp(jnp.minimum(start - bf[0:hi, :], INTRA_CAP))).astype(BF)
            causal = (lax.broadcasted_iota(jnp.int32, (SUB, hi), 1)
                      <= lax.broadcasted_iota(jnp.int32, (SUB, hi), 0) + lo)
            for h in range(HEADS):
                hs = slice(h * HD, (h + 1) * HD)
                att_all[h, lo:hi, 0:hi] = jnp.where(causal, _nt_dot(q_i[:, hs], k_i[:, hs]), 0.0)
        for h in range(HEADS):
            hs = slice(h * HD, (h + 1) * HD)
            v_h = z_ref[:, OFF_CI + h * HD:OFF_CI + (h + 1) * HD].astype(BF)
            st_h = st[h]
            o = (_nt_dot(qe[:, hs], st_h.astype(BF))
                 + jnp.dot(att_all[h].astype(BF), v_h, preferred_element_type=F32))
            st[h] = st_h * jnp.exp(b_last[:, hs]) + _tn_dot(v_h, kd[:, hs])
            oc[:, hs] = o * lax.rsqrt(jnp.mean(o * o, axis=-1, keepdims=True) + RMS_EPS) * ng_ref[...]

    def hgrn_direct():
        lax.fori_loop(0, HEADS, head_step, 0)

    lax.cond(intra_bound <= INTRA_CAP, hgrn_factored, hgrn_direct)

    @pl.when(c == last)
    def _():
        for h in range(HEADS):
            ns_ref[h] = st[h].T

    cg = z_ref[:, OFF_CG:OFF_CG + D]
    y_c = jnp.dot((oc[...] * _silu(cg)).astype(BF), woc_ref[...], preferred_element_type=F32)
    merged2 = merged2 + (1.0 + _tanh_half(z_ref[:, OFF_GC:OFF_GC + D])) * y_c
    res = ALPHA * x_ref[...] + jnp.dot((0.5 * merged2).astype(BF), wo_ref[...], preferred_element_type=F32)
    x1_ref[...] = _layer_norm(res, g1_ref[...], b1_ref[...])


def _mixer(x, z, past_a, past_b, h0, s0, lb, lw):
    bsz, t_len, _ = x.shape
    chunk = min(t_len, 128)
    assert t_len % chunk == 0 and chunk % SUB == 0
    nc = t_len // chunk
    full2 = lambda shape: pl.BlockSpec(shape, lambda b, c: (0, 0))
    full3 = lambda shape: pl.BlockSpec(shape, lambda b, c: (0, 0, 0))
    in_specs = [
        pl.BlockSpec((None, chunk, D), lambda b, c: (b, c, 0)),
        pl.BlockSpec((None, chunk, IN_COLS), lambda b, c: (b, c, 0)),
        pl.BlockSpec((None, CONV_A_W - 1, D), lambda b, c: (b, 0, 0)),
        pl.BlockSpec((None, CONV_B_W - 1, D), lambda b, c: (b, 0, 0)),
        pl.BlockSpec((None, 1, D), lambda b, c: (b, 0, 0)),
        pl.BlockSpec((None, HEADS, HD, HD), lambda b, c: (b, 0, 0, 0)),
        full2((1, D)),
        full2((CONV_A_W, D)), full2((CONV_B_W, D)), full2((1, D)),
        full3((LRU_BLOCKS, LRU_BW, LRU_BW)), full2((1, D)),
        full3((LRU_BLOCKS, LRU_BW, LRU_BW)), full2((1, D)),
        full2((1, D)), full2((1, HD)),
        full2((D, D)), full2((D, D)), full2((D, D)), full2((D, D)),
        full2((1, D)), full2((1, D)),
    ]
    out_specs = [
        pl.BlockSpec((None, chunk, D), lambda b, c: (b, c, 0)),
        pl.BlockSpec((None, CONV_A_W - 1, D), lambda b, c: (b, 0, 0)),
        pl.BlockSpec((None, CONV_B_W - 1, D), lambda b, c: (b, 0, 0)),
        pl.BlockSpec((None, 1, D), lambda b, c: (b, 0, 0)),
        pl.BlockSpec((None, HEADS, HD, HD), lambda b, c: (b, 0, 0, 0)),
    ]
    out_shape = [
        jax.ShapeDtypeStruct((bsz, t_len, D), F32),
        jax.ShapeDtypeStruct((bsz, CONV_A_W - 1, D), F32),
        jax.ShapeDtypeStruct((bsz, CONV_B_W - 1, D), F32),
        jax.ShapeDtypeStruct((bsz, 1, D), F32),
        jax.ShapeDtypeStruct((bsz, HEADS, HD, HD), F32),
    ]
    scratch = [
        pltpu.VMEM((SUBLANES, D), F32),
        pltpu.VMEM((SUBLANES, D), F32),
        pltpu.VMEM((SUBLANES, D), F32),
        pltpu.VMEM((HEADS, HD, HD), F32),
        pltpu.VMEM((chunk, HD), F32),
        pltpu.VMEM((chunk, HD), F32),
        pltpu.VMEM((chunk, HD), F32),
        pltpu.VMEM((chunk, HD), F32),
        pltpu.VMEM((chunk, chunk), F32),
        pltpu.VMEM((chunk, D), F32),
        pltpu.VMEM((chunk, D), F32),
        pltpu.VMEM((chunk, D), F32),
        pltpu.VMEM((chunk, D), F32),
        pltpu.VMEM((chunk, D), BF),
        pltpu.VMEM((chunk, D), BF),
        pltpu.VMEM((HEADS, chunk, chunk), F32),
    ]
    return pl.pallas_call(
        functools.partial(_mixer_body, chunk=chunk),
        grid=(bsz, nc),
        in_specs=in_specs, out_specs=out_specs, out_shape=out_shape, scratch_shapes=scratch,
        compiler_params=pltpu.CompilerParams(
            dimension_semantics=("arbitrary", "arbitrary"), vmem_limit_bytes=VMEM_LIMIT),
        name="mixer",
    )(x, z, past_a, past_b, h0, s0, lb, lw["conv_a_w"], lw["conv_b_w"], lw["conv_b_b"],
      lw["lru_wa"], lw["lru_ba"], lw["lru_wx"], lw["lru_bx"], lw["lru_lambda"], lw["hgrn_norm_g"],
      lw["w_out_a"], lw["w_out_b"], lw["w_out_c"], lw["w_o"], lw["ln1_g"], lw["ln1_b"])


def _sort_network(n):
    pairs = []

    def merge(lo, m, r):
        step = 2 * r
        if step < m:
            merge(lo, m, step)
            merge(lo + r, m, step)
            pairs.extend((i, i + r) for i in range(lo + r, lo + m - r, step))
        else:
            pairs.append((lo, lo + r))

    def sort(lo, m):
        if m > 1:
            sort(lo, m // 2)
            sort(lo + m // 2, m // 2)
            merge(lo, m, 1)

    sort(0, n)
    return pairs


_SORT16 = _sort_network(TOPK)


def _cmp_exchange(v, i, j):
    if v[j] is None:
        return
    if v[i] is None:
        v[i], v[j] = v[j], None
        return
    v[i], v[j] = jnp.maximum(v[i], v[j]), jnp.minimum(v[i], v[j])


def _sort_desc(v):
    v = list(v) + [None] * (TOPK - len(v))
    for i, j in _SORT16:
        _cmp_exchange(v, i, j)
    return v


def _bitonic_to_desc(v):
    v = list(v)
    d = TOPK // 2
    while d >= 1:
        for i in range(TOPK):
            if i & d == 0:
                _cmp_exchange(v, i, i + d)
        d //= 2
    return v


def _merge_rolled(v, shift):
    w = [None if x is None else pltpu.roll(x, shift, 0) for x in v]
    out = []
    for i in range(TOPK):
        a, b = v[i], w[TOPK - 1 - i]
        out.append(b if a is None else a if b is None else jnp.maximum(a, b))
    return out


def _top16_of_rows(vregs):
    v = _sort_desc(vregs)
    for shift in (4, 2, 1):
        v = _bitonic_to_desc(_merge_rolled(v, shift))
    return v


def _kth16_of_rows(vregs):
    v = _sort_desc(vregs)
    for shift in (4, 2):
        v = _bitonic_to_desc(_merge_rolled(v, shift))
    return functools.reduce(jnp.minimum, _merge_rolled(v, 1))


def _pack_rows(vs, sub):
    out = vs[0]
    for j in range(1, SUBLANES):
        out = jnp.where(sub == j, vs[j], out)
    return out


def _scores_body(x_ref, wq_ref, keys_ref, xt_ref, p1_ref, p2_ref, th_ref, qbuf, *, tn):
    xt_ref[...] = x_ref[...].T.astype(BF)
    qbuf[...] = jnp.dot(x_ref[...].astype(BF), wq_ref[...], preferred_element_type=F32).astype(BF)
    tw = SCORE_TOKENS
    sub = lax.broadcasted_iota(jnp.int32, (SUBLANES, tw), 0)
    n_tc = tn // tw

    def step(idx, carry):
        h = idx // n_tc
        tc = idx % n_tc
        t0 = pl.multiple_of(tc * tw, tw)
        c0 = pl.multiple_of(h * (2 * PEER_DH), 2 * PEER_DH)
        s1 = _nt_dot(keys_ref[h, 0], qbuf[pl.ds(t0, tw), pl.ds(c0, PEER_DH)])
        s2 = _nt_dot(keys_ref[h, 1], qbuf[pl.ds(t0, tw), pl.ds(c0 + PEER_DH, PEER_DH)])
        split = lambda s: [s[r * SUBLANES:(r + 1) * SUBLANES, :] for r in range(N_KEYS // SUBLANES)]
        a = _top16_of_rows(split(s1))
        b = _top16_of_rows(split(s2))
        pa = [jnp.exp(x - a[0]) for x in a]
        pb = [jnp.exp(x - b[0]) for x in b]
        pb_lo, pb_hi = _pack_rows(pb[:SUBLANES], sub), _pack_rows(pb[SUBLANES:], sub)
        pa_hi = _pack_rows(pa[SUBLANES:], sub)
        lhs = [pa[0], pa[0]] + pa[1:SUBLANES] + [pa_hi]
        rhs = [pb_lo, pb_hi] + [pb_lo] * (SUBLANES - 1) + [pb[0]]
        t = [x * y for x, y in zip(lhs, rhs)]
        t16 = _kth16_of_rows(t)
        keep = [x >= t16 for x in t]
        z_sum = jnp.sum(sum(jnp.where(m, x, 0.0) for m, x in zip(keep, t)), axis=0, keepdims=True)
        rz = 0.5 / z_sum
        gates = [(x * rz) * y for x, y in zip(lhs, rhs)]
        th = jnp.min(functools.reduce(jnp.minimum, [jnp.where(m, g, jnp.inf) for m, g in zip(keep, gates)]),
                     axis=0, keepdims=True)
        p1_ref[h, :, pl.ds(t0, tw)] = jnp.exp(s1 - a[0][0:1]) * rz
        p2 = jnp.exp(s2 - b[0][0:1])
        for part in range(tw // LANES):
            p2_ref[h, tc * (tw // LANES) + part] = p2[:, part * LANES:(part + 1) * LANES]
        th_ref[h, :, pl.ds(t0, tw)] = th
        return carry

    lax.fori_loop(0, PEER_HEADS * n_tc, step, 0)


def _peer_scores(x2d, wq_bf, keys_bf, tn, layer):
    n = x2d.shape[0]
    n_tc = tn // LANES
    return pl.pallas_call(
        functools.partial(_scores_body, tn=tn),
        grid=(n // tn,),
        in_specs=[pl.BlockSpec((tn, D), lambda i: (i, 0)),
                  pl.BlockSpec((None, D, 2 * PEER_HEADS * PEER_DH), lambda i: (layer, 0, 0)),
                  pl.BlockSpec((None, PEER_HEADS, 2, N_KEYS, PEER_DH), lambda i: (layer, 0, 0, 0, 0))],
        out_specs=[pl.BlockSpec((D, tn), lambda i: (0, i)),
                   pl.BlockSpec((PEER_HEADS, N_KEYS, tn), lambda i: (0, 0, i)),
                   pl.BlockSpec((PEER_HEADS, n_tc, N_KEYS, LANES), lambda i: (0, i, 0, 0)),
                   pl.BlockSpec((PEER_HEADS, 1, tn), lambda i: (0, 0, i))],
        out_shape=[jax.ShapeDtypeStruct((D, n), BF),
                   jax.ShapeDtypeStruct((PEER_HEADS, N_KEYS, n), F32),
                   jax.ShapeDtypeStruct((PEER_HEADS, n // LANES, N_KEYS, LANES), F32),
                   jax.ShapeDtypeStruct((PEER_HEADS, 1, n), F32)],
        scratch_shapes=[pltpu.VMEM((tn, 2 * PEER_HEADS * PEER_DH), BF)],
        compiler_params=pltpu.CompilerParams(
            dimension_semantics=("arbitrary",), vmem_limit_bytes=VMEM_LIMIT),
        name="peer_scores",
    )(x2d, wq_bf, keys_bf)


def _dense_body(xt_ref, x_ref, u0_ref, un_ref, vp_ref, p1_ref, p2_ref, th_ref, g2_ref, b2_ref,
                y_ref, acc, h_even, h_odd, g_even, g_odd, *, tn, eb, nb):
    j = pl.program_id(1)
    kb = eb // N_KEYS
    n_tc = tn // LANES

    def matmul_jobs(jobs):
        assert len(jobs) % 2 == 0
        for p in range(0, len(jobs), 2):
            pair = jobs[p:p + 2]
            base = (p // 2 % 2) * (MXU_ROWS // 4)
            for k in range(D // MXU_TILE):
                ks = slice(k * MXU_TILE, (k + 1) * MXU_TILE)
                for q, (lhs, rhs, r0, c0, _) in enumerate(pair):
                    pltpu.matmul_push_rhs(rhs[ks, c0:c0 + MXU_TILE], staging_register=k % 2, mxu_index=q)
                shared_lhs = pair[0][0] is pair[1][0] and pair[0][2] == pair[1][2]
                for m in range(MXU_ROWS // MXU_STEP):
                    piece = None
                    for q, (lhs, rhs, r0, c0, _) in enumerate(pair):
                        if piece is None or not shared_lhs:
                            piece = lhs[r0 + m * MXU_STEP:r0 + (m + 1) * MXU_STEP, ks]
                        pltpu.matmul_acc_lhs(base + m * (MXU_STEP // 4), piece, q,
                                             load_staged_rhs=(k % 2) if m == 0 else None)
                    yield
            for q, (lhs, rhs, r0, c0, sink) in enumerate(pair):
                sink(r0, c0, pltpu.matmul_pop(base, (MXU_ROWS, MXU_TILE), F32, q))

    def tiles(lhs, rhs, sink):
        return [(lhs, rhs, r0, c0, sink) for r0 in range(0, eb, MXU_ROWS) for c0 in range(0, tn, MXU_TILE)]

    def store_into(dst):
        def sink(r0, c0, val):
            dst[r0:r0 + MXU_ROWS, c0:c0 + MXU_TILE] = val
        return sink

    def add_into_acc(r0, c0, val):
        acc[r0:r0 + MXU_ROWS, c0:c0 + MXU_TILE] += val

    def run(*programs):
        live = list(programs)
        while live:
            for g in list(live):
                if next(g, StopIteration) is StopIteration:
                    live.remove(g)

    @pl.when(j == 0)
    def _():
        acc[...] = jnp.zeros((D, tn), F32)
        g_odd[:, 0:tn] = jnp.zeros((eb, tn), BF)
        run(matmul_jobs(tiles(u0_ref, xt_ref, store_into(h_even))))

    def gate_build(h_cur, g_cur):
        for k0 in range(0, kb, GATE_KEYS):
            for tc in range(n_tc):
                for half in range(N_KEYS // GATE_ROWS):
                    lanes = slice(tc * LANES, (tc + 1) * LANES)
                    sub_rows = slice(half * GATE_ROWS, (half + 1) * GATE_ROWS)
                    w = [None] * GATE_KEYS
                    for h in range(PEER_HEADS):
                        p2 = p2_ref[h, tc, sub_rows, :]
                        th = th_ref[h, :, lanes]
                        for a in range(GATE_KEYS):
                            prod = p1_ref[h, k0 + a:k0 + a + 1, lanes] * p2
                            kept = jnp.where(prod >= th, prod, 0.0)
                            w[a] = kept if h == 0 else w[a] + kept
                            if a % GATE_YIELD == GATE_YIELD - 1:
                                yield
                    for a in range(GATE_KEYS):
                        r0 = (k0 + a) * N_KEYS + half * GATE_ROWS
                        g_cur[r0:r0 + GATE_ROWS, lanes] = (
                            w[a] * _gelu_twice(h_cur[r0:r0 + GATE_ROWS, lanes])).astype(BF)

    def block_step(h_cur, h_next, g_cur, g_prev):
        run(matmul_jobs(tiles(un_ref, xt_ref, store_into(h_next)) + tiles(vp_ref, g_prev, add_into_acc)),
            gate_build(h_cur, g_cur))

    @pl.when((j < nb) & (j % 2 == 0))
    def _():
        block_step(h_even, h_odd, g_even, g_odd)

    @pl.when((j < nb) & (j % 2 == 1))
    def _():
        block_step(h_odd, h_even, g_odd, g_even)

    @pl.when(j == nb)
    def _():
        g_last = g_even if (nb - 1) % 2 == 0 else g_odd

        run(matmul_jobs(tiles(vp_ref, g_last, add_into_acc)))
        y_ref[...] = _layer_norm(ALPHA * x_ref[...] + acc[...].T, g2_ref[...], b2_ref[...])


def _peer_dense(xt_bf, x2d, u_bf, vt_bf, p1, p2, th, g2, b2, tn, layer):
    n = x2d.shape[0]
    eb = D
    nb = N_EXPERTS // eb
    n_tc = tn // LANES
    assert eb // N_KEYS == SUBLANES and (eb // N_KEYS) % GATE_KEYS == 0
    once = pl.Buffered(1)
    return pl.pallas_call(
        functools.partial(_dense_body, tn=tn, eb=eb, nb=nb),
        grid=(n // tn, nb + 1),
        in_specs=[pl.BlockSpec((D, tn), lambda i, j: (0, i), pipeline_mode=once),
                  pl.BlockSpec((tn, D), lambda i, j: (i, 0), pipeline_mode=once),
                  pl.BlockSpec((None, eb, D), lambda i, j: (layer, 0, 0), pipeline_mode=once),
                  pl.BlockSpec((None, eb, D), lambda i, j: (layer, jnp.minimum(j + 1, nb - 1), 0)),
                  pl.BlockSpec((None, None, D, eb), lambda i, j: (layer, jnp.maximum(j - 1, 0), 0, 0)),
                  pl.BlockSpec((PEER_HEADS, eb // N_KEYS, tn),
                               lambda i, j: (0, jnp.minimum(j, nb - 1), i)),
                  pl.BlockSpec((PEER_HEADS, n_tc, N_KEYS, LANES), lambda i, j: (0, i, 0, 0),
                               pipeline_mode=once),
                  pl.BlockSpec((PEER_HEADS, 1, tn), lambda i, j: (0, 0, i), pipeline_mode=once),
                  pl.BlockSpec((1, D), lambda i, j: (0, 0), pipeline_mode=once),
                  pl.BlockSpec((1, D), lambda i, j: (0, 0), pipeline_mode=once)],
        out_specs=pl.BlockSpec((tn, D), lambda i, j: (i, 0)),
        out_shape=jax.ShapeDtypeStruct((n, D), F32),
        scratch_shapes=[pltpu.VMEM((D, tn), F32),
                        pltpu.VMEM((eb, tn + LANES), F32), pltpu.VMEM((eb, tn + LANES), F32),
                        pltpu.VMEM((eb, tn + LANES), BF), pltpu.VMEM((eb, tn + LANES), BF)],
        compiler_params=pltpu.CompilerParams(
            dimension_semantics=("arbitrary", "arbitrary"), vmem_limit_bytes=VMEM_LIMIT),
        name="peer_dense",
    )(xt_bf, x2d, u_bf, u_bf, vt_bf, p1, p2, th, g2, b2)


def _trunk_layer(x, past_a, past_b, h0, s0, lb, lw, big, layer):
    bsz, t_len, _ = x.shape
    n = bsz * t_len
    z = _in_proj(x.reshape(n, D), big["w_in"], lw["b_in"], layer).reshape(bsz, t_len, IN_COLS)
    x1, new_a, new_b, h_t, s_t = _mixer(x, z, past_a, past_b, h0, s0, lb, lw)
    x1 = x1.reshape(n, D)
    tn = min(n, 512)
    xt, p1, p2, th = _peer_scores(x1, big["peer_wq"], big["peer_keys"], tn, layer)
    x2 = _peer_dense(xt, x1, big["peer_u"], big["peer_vt"], p1, p2, th, lw["ln2_g"], lw["ln2_b"],
                     min(n, DENSE_TOKENS), layer)
    return x2.reshape(bsz, t_len, D), new_a, new_b, h_t.reshape(bsz, D), s_t


def kernel(x_prompt, x_sample, state_conv_a, state_conv_b, state_lru, state_hgrn, w_in, b_in, conv_a_w,
           conv_b_w, conv_b_b, lru_wa, lru_ba, lru_wx, lru_bx, lru_lambda, hgrn_lb_logits, hgrn_norm_g,
           w_out_a, w_out_b, w_out_c, w_o, ln1_g, ln1_b, peer_wq, peer_keys, peer_u, peer_v, ln2_g,
           ln2_b):
    p = jax.nn.softmax(hgrn_lb_logits.astype(F32), axis=0)
    lbs = jnp.cumsum(p, axis=0) - p[0]
    bp = x_prompt.shape[0]
    xp, xs = x_prompt, x_sample
    outs_p, outs_s = [], []
    big = dict(
        w_in=w_in.astype(BF), peer_wq=peer_wq.astype(BF), peer_keys=peer_keys.astype(BF),
        peer_u=peer_u.astype(BF),
        peer_vt=peer_v.reshape(DEPTH, N_EXPERTS // D, D, D).transpose(0, 1, 3, 2).astype(BF))
    for l in range(DEPTH):
        row = lambda a: a[l].reshape(1, -1)
        lw = dict(
            b_in=row(b_in),
            conv_a_w=conv_a_w[l], conv_b_w=conv_b_w[l], conv_b_b=row(conv_b_b),
            lru_wa=lru_wa[l].astype(BF), lru_ba=row(lru_ba), lru_wx=lru_wx[l].astype(BF),
            lru_bx=row(lru_bx), lru_lambda=row(lru_lambda), hgrn_norm_g=row(hgrn_norm_g),
            w_out_a=w_out_a[l].astype(BF), w_out_b=w_out_b[l].astype(BF),
            w_out_c=w_out_c[l].astype(BF), w_o=w_o[l].astype(BF),
            ln1_g=row(ln1_g), ln1_b=row(ln1_b), ln2_g=row(ln2_g), ln2_b=row(ln2_b))
        lb = lbs[l].reshape(1, D)
        xp, pa, pb, ph, ps = _trunk_layer(
            xp, jnp.zeros((bp, CONV_A_W - 1, D), F32), jnp.zeros((bp, CONV_B_W - 1, D), F32),
            jnp.zeros((bp, 1, D), F32), jnp.zeros((bp, HEADS, HD, HD), F32), lb, lw, big, l)
        xs, sa, sb, sh, ss = _trunk_layer(
            xs, state_conv_a[l], state_conv_b[l], state_lru[l][:, None, :], state_hgrn[l], lb, lw, big, l)
        outs_p.append((pa, pb, ph, ps))
        outs_s.append((sa, sb, sh, ss))
    stack = lambda outs, k: jnp.stack([o[k] for o in outs])
    return (xp, xs,
            stack(outs_p, 0), stack(outs_p, 1), stack(outs_p, 2), stack(outs_p, 3),
            stack(outs_s, 0), stack(outs_s, 1), stack(outs_s, 2), stack(outs_s, 3))
```

```python
import functools

import jax
import jax.numpy as jnp
from jax import lax
from jax.experimental import pallas as pl
from jax.experimental.pallas import tpu as pltpu

D = 1024
DEPTH = 2
CONV_A_W = 3
CONV_B_W = 4
LRU_BLOCKS = 8
LRU_BW = D // LRU_BLOCKS
LRU_C = 8.0
HEADS = 8
HD = D // HEADS
PEER_HEADS = 8
PEER_DH = 128
N_KEYS = 128
N_EXPERTS = N_KEYS * N_KEYS
TOPK = 16
ALPHA = (2.0 * DEPTH) ** 0.25
LN_EPS = 1e-5
RMS_EPS = 1e-6
IN_COLS = 12 * D
SUB = 16
INTRA_CAP = 60.0
LANES = 128
SUBLANES = 8
VMEM_LIMIT = 56 * 1024 * 1024
GATE_KEYS = 4
GATE_ROWS = 64
SCORE_TOKENS = 256
DENSE_TOKENS = 1024
MXU_TILE = 256
MXU_ROWS = 512
MXU_STEP = 32
GATE_YIELD = 2

OFF_AB, OFF_AC, OFF_AX, OFF_BX, OFF_BG, OFF_CQ, OFF_CF, OFF_CI, OFF_CG, OFF_GA, OFF_GB, OFF_GC = (
    i * D for i in range(12))

BF = jnp.bfloat16
F32 = jnp.float32


def _tanh_half(x):
    return jnp.tanh(0.5 * x)


def _sigmoid(x):
    return 0.5 * _tanh_half(x) + 0.5


def _silu(x):
    h = 0.5 * x
    return h + h * jnp.tanh(h)


def _gelu_twice(x):
    return x * (1.0 + lax.erf(x * (2.0 ** -0.5)))


def _gelu(x):
    return 0.5 * _gelu_twice(x)


def _layer_norm(x, g, b):
    mu = jnp.mean(x, axis=-1, keepdims=True)
    xc = x - mu
    var = jnp.mean(xc * xc, axis=-1, keepdims=True)
    return xc * lax.rsqrt(var + LN_EPS) * g + b


def _nt_dot(a, b):
    return lax.dot_general(a, b, (((1,), (1,)), ((), ())), preferred_element_type=F32)


def _tn_dot(a, b):
    return lax.dot_general(a, b, (((0,), (0,)), ((), ())), preferred_element_type=F32)


def _scan_rows(a, u, carry, rows8):
    out = []
    for g in range(u.shape[0] // SUBLANES):
        rs = slice(g * SUBLANES, (g + 1) * SUBLANES)
        u_g = u[rs, :]
        a_g = None if a is None else a[rs, :]
        d = 1
        while d < SUBLANES:
            keep = rows8 >= d
            if a_g is None:
                u_g = u_g + jnp.where(keep, pltpu.roll(u_g, d, 0), 0.0)
            else:
                u_g = a_g * jnp.where(keep, pltpu.roll(u_g, d, 0), 0.0) + u_g
                a_g = a_g * jnp.where(keep, pltpu.roll(a_g, d, 0), 1.0)
            d *= 2
        h_g = u_g + carry if a_g is None else u_g + a_g * carry
        out.append(h_g)
        carry = h_g[SUBLANES - 1:SUBLANES, :]
    return jnp.concatenate(out, axis=0)


def _inproj_body(x_ref, w_ref, b_ref, o_ref):
    x = x_ref[...].astype(BF)
    o_ref[...] = jnp.dot(x, w_ref[...], preferred_element_type=F32) + b_ref[...]


def _in_proj(x2d, w_bf, b_row, layer):
    n = x2d.shape[0]
    tm = min(n, 1024)
    tn = 2048
    return pl.pallas_call(
        _inproj_body,
        grid=(n // tm, IN_COLS // tn),
        in_specs=[pl.BlockSpec((tm, D), lambda i, j: (i, 0)),
                  pl.BlockSpec((None, D, tn), lambda i, j: (layer, 0, j)),
                  pl.BlockSpec((1, tn), lambda i, j: (0, j))],
        out_specs=pl.BlockSpec((tm, tn), lambda i, j: (i, j)),
        out_shape=jax.ShapeDtypeStruct((n, IN_COLS), F32),
        compiler_params=pltpu.CompilerParams(
            dimension_semantics=("arbitrary", "arbitrary"), vmem_limit_bytes=VMEM_LIMIT),
        name="in_proj",
    )(x2d, w_bf, b_row)


def _mixer_body(x_ref, z_ref, pa_ref, pb_ref, h0_ref, s0_ref, lb_ref, cwa_ref, cwb_ref, cbb_ref,
                wa_ref, ba_ref, wx_ref, bx_ref, lam_ref, ng_ref, woa_ref, wob_ref, woc_ref, wo_ref,
                g1_ref, b1_ref,
                x1_ref, na_ref, nb_ref, nh_ref, ns_ref,
                ua, ub, hc, st, qs, ks, vs, bs, att, oc, kf, bf, qsil, qe, kd, att_all, *, chunk):
    c = pl.program_id(1)
    last = pl.num_programs(1) - 1
    nsub = chunk // SUB

    rows8 = lax.broadcasted_iota(jnp.int32, (SUBLANES, D), 0)

    def causal_conv(slab, w_ref, width, prev):
        out = []
        below = [pltpu.roll(prev, d, 0) for d in range(1, width)]
        for g in range(chunk // SUBLANES):
            x_g = slab(g)
            y = w_ref[width - 1:width, :] * x_g
            for d in range(1, width):
                cur = pltpu.roll(x_g, d, 0)
                y = y + w_ref[width - 1 - d:width - d, :] * jnp.where(rows8 < d, below[d - 1], cur)
                below[d - 1] = cur
            out.append(y)
        return jnp.concatenate(out, axis=0), x_g

    @pl.when(c == 0)
    def _():
        ua[...] = jnp.zeros((SUBLANES, D), F32)
        ub[...] = jnp.zeros((SUBLANES, D), F32)
        ua[SUBLANES - 2:SUBLANES, :] = pa_ref[...]
        ub[SUBLANES - 3:SUBLANES, :] = pb_ref[...]
        hc[0:1, :] = h0_ref[...]
        for h in range(HEADS):
            st[h] = s0_ref[h].T
        att_all[...] = jnp.zeros((HEADS, chunk, chunk), F32)

    def slab_a(g):
        rs = slice(g * SUBLANES, (g + 1) * SUBLANES)
        return z_ref[rs, OFF_AC:OFF_AC + D] * z_ref[rs, OFF_AX:OFF_AX + D]

    conv_a, last_a = causal_conv(slab_a, cwa_ref, CONV_A_W, ua[...])
    na_ref[...] = last_a[SUBLANES - (CONV_A_W - 1):SUBLANES, :]
    ua[...] = last_a
    y_a = jnp.dot((z_ref[:, OFF_AB:OFF_AB + D] * conv_a).astype(BF), woa_ref[...],
                  preferred_element_type=F32)
    merged2 = (1.0 + _tanh_half(z_ref[:, OFF_GA:OFF_GA + D])) * y_a

    conv_b, last_b = causal_conv(
        lambda g: z_ref[g * SUBLANES:(g + 1) * SUBLANES, OFF_BX:OFF_BX + D], cwb_ref, CONV_B_W, ub[...])
    xl = conv_b + cbb_ref[...]
    nb_ref[...] = last_b[SUBLANES - (CONV_B_W - 1):SUBLANES, :]
    ub[...] = last_b

    xl_bf = xl.astype(BF)
    ra = jnp.concatenate(
        [jnp.dot(xl_bf[:, n * LRU_BW:(n + 1) * LRU_BW], wa_ref[n], preferred_element_type=F32)
         for n in range(LRU_BLOCKS)], axis=1)
    rx = jnp.concatenate(
        [jnp.dot(xl_bf[:, n * LRU_BW:(n + 1) * LRU_BW], wx_ref[n], preferred_element_type=F32)
         for n in range(LRU_BLOCKS)], axis=1)
    gi = _sigmoid(rx + bx_ref[...])
    nlam = -lam_ref[...]
    softplus = jnp.maximum(nlam, 0.0) + jnp.log1p(jnp.exp(-jnp.abs(nlam)))
    log_a = ((-0.5 * LRU_C) * softplus) * (1.0 + _tanh_half(ra + ba_ref[...]))
    a = jnp.exp(log_a)
    uu = jnp.sqrt(jnp.tanh(-log_a) * (a * a + 1.0)) * (gi * xl)
    hh = _scan_rows(a, uu, hc[0:1, :], rows8)
    h_last = hh[chunk - 1:chunk, :]
    hc[0:1, :] = h_last
    nh_ref[...] = h_last
    y_b = jnp.dot((_gelu(z_ref[:, OFF_BG:OFF_BG + D]) * hh).astype(BF), wob_ref[...],
                  preferred_element_type=F32)
    merged2 = merged2 + (1.0 + _tanh_half(z_ref[:, OFF_GB:OFF_GB + D])) * y_b

    cols_a = lax.broadcasted_iota(jnp.int32, (SUB, chunk), 1)
    sub_rows = lax.broadcasted_iota(jnp.int32, (SUB, 1), 0)
    f_all = lb_ref[...] + (1.0 - lb_ref[...]) / (1.0 + jnp.exp(-z_ref[:, OFF_CF:OFF_CF + D]))
    kf[...] = 1.0 - f_all
    bf[...] = _scan_rows(None, jnp.log(f_all), jnp.zeros((1, D), F32), rows8)
    intra_bound = jnp.max(functools.reduce(jnp.maximum, [
        (bf[i * SUB - 1:i * SUB, :] if i else 0.0) - bf[(i + 1) * SUB - 1:(i + 1) * SUB, :]
        for i in range(nsub)]))

    def head_step(h, carry):
        off = pl.multiple_of(h * HD, HD)
        k = kf[:, pl.ds(off, HD)]
        b = bf[:, pl.ds(off, HD)]
        cq = z_ref[:, pl.ds(OFF_CQ + off, HD)]
        q = _silu(cq)
        v = z_ref[:, pl.ds(OFF_CI + off, HD)]
        qs[...] = q
        ks[...] = k
        vs[...] = v
        bs[...] = b
        st_h = st[h]
        b_last = b[chunk - 1:chunk, :]
        o = _nt_dot((q * jnp.exp(b)).astype(BF), st_h.astype(BF))
        kdec = k * jnp.exp(b_last - b)
        st[h] = st_h * jnp.exp(b_last) + _tn_dot(v.astype(BF), kdec.astype(BF))

        def sub_start(i):
            return bs[i * SUB - 1:i * SUB, :] if i else jnp.zeros((1, HD), F32)

        def intra_direct():
            att[0:SUB, :] = jnp.zeros((SUB, chunk), F32)
            for i in range(1, nsub):
                lo, hi = i * SUB, (i + 1) * SUB
                q_i = qs[lo:hi, :] * jnp.exp(bs[lo:hi, :] - sub_start(i))
                k_i = k * jnp.exp(jnp.minimum(sub_start(i) - b, 0.0))
                a_i = _nt_dot(q_i.astype(BF), k_i.astype(BF))
                att[lo:hi, :] = jnp.where(cols_a < lo, a_i, 0.0)
            o_off = jnp.dot(att[...].astype(BF), v.astype(BF), preferred_element_type=F32)
            o_parts = []
            for i in range(nsub):
                q_i = qs[i * SUB:(i + 1) * SUB, :]
                b_i = bs[i * SUB:(i + 1) * SUB, :]
                o_i = jnp.zeros((SUB, HD), F32)
                for s in range(SUB):
                    r0 = i * SUB + s
                    p = q_i * ks[r0:r0 + 1, :] * jnp.exp(jnp.minimum(b_i - bs[r0:r0 + 1, :], 0.0))
                    a_col = jnp.sum(p, axis=-1, keepdims=True)
                    a_col = jnp.where(sub_rows >= s, a_col, 0.0)
                    o_i = o_i + a_col * vs[r0:r0 + 1, :]
                o_parts.append(o_i)
            return o_off + jnp.concatenate(o_parts, axis=0)

        o = o + intra_direct()
        oc[:, pl.ds(off, HD)] = o * lax.rsqrt(jnp.mean(o * o, axis=-1, keepdims=True) + RMS_EPS) * ng_ref[...]
        return carry

    def hgrn_factored():
        cq = z_ref[:, OFF_CQ:OFF_CQ + D]
        qsil[...] = _silu(cq)
        b_last = bf[chunk - 1:chunk, :]
        qe[...] = (qsil[...] * jnp.exp(bf[...])).astype(BF)
        kd[...] = (kf[...] * jnp.exp(b_last - bf[...])).astype(BF)
        for i in range(nsub):
            lo, hi = i * SUB, (i + 1) * SUB
            start = bf[lo - 1:lo, :] if i else 0.0
            q_i = (qsil[lo:hi, :] * jnp.exp(bf[lo:hi, :] - start)).astype(BF)
            k_i = (kf[0:hi, :] * jnp.exp(jnp.minimum(start - bf[0:hi, :], INTRA_CAP))).astype(BF)
            causal = (lax.broadcasted_iota(jnp.int32, (SUB, hi), 1)
                      <= lax.broadcasted_iota(jnp.int32, (SUB, hi), 0) + lo)
            for h in range(HEADS):
                hs = slice(h * HD, (h + 1) * HD)
                att_all[h, lo:hi, 0:hi] = jnp.where(causal, _nt_dot(q_i[:, hs], k_i[:, hs]), 0.0)
        for h in range(HEADS):
            hs = slice(h * HD, (h + 1) * HD)
            v_h = z_ref[:, OFF_CI + h * HD:OFF_CI + (h + 1) * HD].astype(BF)
            st_h = st[h]
            o = (_nt_dot(qe[:, hs], st_h.astype(BF))
                 + jnp.dot(att_all[h].astype(BF), v_h, preferred_element_type=F32))
            st[h] = st_h * jnp.exp(b_last[:, hs]) + _tn_dot(v_h, kd[:, hs])
            oc[:, hs] = o * lax.rsqrt(jnp.mean(o * o, axis=-1, keepdims=True) + RMS_EPS) * ng_ref[...]

    def hgrn_direct():
        lax.fori_loop(0, HEADS, head_step, 0)

    lax.cond(intra_bound <= INTRA_CAP, hgrn_factored, hgrn_direct)

    @pl.when(c == last)
    def _():
        for h in range(HEADS):
            ns_ref[h] = st[h].T

    cg = z_ref[:, OFF_CG:OFF_CG + D]
    y_c = jnp.dot((oc[...] * _silu(cg)).astype(BF), woc_ref[...], preferred_element_type=F32)
    merged2 = merged2 + (1.0 + _tanh_half(z_ref[:, OFF_GC:OFF_GC + D])) * y_c
    res = ALPHA * x_ref[...] + jnp.dot((0.5 * merged2).astype(BF), wo_ref[...], preferred_element_type=F32)
    x1_ref[...] = _layer_norm(res, g1_ref[...], b1_ref[...])


def _mixer(x, z, past_a, past_b, h0, s0, lb, lw):
    bsz, t_len, _ = x.shape
    chunk = min(t_len, 128)
    assert t_len % chunk == 0 and chunk % SUB == 0
    nc = t_len // chunk
    full2 = lambda shape: pl.BlockSpec(shape, lambda b, c: (0, 0))
    full3 = lambda shape: pl.BlockSpec(shape, lambda b, c: (0, 0, 0))
    in_specs = [
        pl.BlockSpec((None, chunk, D), lambda b, c: (b, c, 0)),
        pl.BlockSpec((None, chunk, IN_COLS), lambda b, c: (b, c, 0)),
        pl.BlockSpec((None, CONV_A_W - 1, D), lambda b, c: (b, 0, 0)),
        pl.BlockSpec((None, CONV_B_W - 1, D), lambda b, c: (b, 0, 0)),
        pl.BlockSpec((None, 1, D), lambda b, c: (b, 0, 0)),
        pl.BlockSpec((None, HEADS, HD, HD), lambda b, c: (b, 0, 0, 0)),
        full2((1, D)),
        full2((CONV_A_W, D)), full2((CONV_B_W, D)), full2((1, D)),
        full3((LRU_BLOCKS, LRU_BW, LRU_BW)), full2((1, D)),
        full3((LRU_BLOCKS, LRU_BW, LRU_BW)), full2((1, D)),
        full2((1, D)), full2((1, HD)),
        full2((D, D)), full2((D, D)), full2((D, D)), full2((D, D)),
        full2((1, D)), full2((1, D)),
    ]
    out_specs = [
        pl.BlockSpec((None, chunk, D), lambda b, c: (b, c, 0)),
        pl.BlockSpec((None, CONV_A_W - 1, D), lambda b, c: (b, 0, 0)),
        pl.BlockSpec((None, CONV_B_W - 1, D), lambda b, c: (b, 0, 0)),
        pl.BlockSpec((None, 1, D), lambda b, c: (b, 0, 0)),
        pl.BlockSpec((None, HEADS, HD, HD), lambda b, c: (b, 0, 0, 0)),
    ]
    out_shape = [
        jax.ShapeDtypeStruct((bsz, t_len, D), F32),
        jax.ShapeDtypeStruct((bsz, CONV_A_W - 1, D), F32),
        jax.ShapeDtypeStruct((bsz, CONV_B_W - 1, D), F32),
        jax.ShapeDtypeStruct((bsz, 1, D), F32),
        jax.ShapeDtypeStruct((bsz, HEADS, HD, HD), F32),
    ]
    scratch = [
        pltpu.VMEM((SUBLANES, D), F32),
        pltpu.VMEM((SUBLANES, D), F32),
        pltpu.VMEM((SUBLANES, D), F32),
        pltpu.VMEM((HEADS, HD, HD), F32),
        pltpu.VMEM((chunk, HD), F32),
        pltpu.VMEM((chunk, HD), F32),
        pltpu.VMEM((chunk, HD), F32),
        pltpu.VMEM((chunk, HD), F32),
        pltpu.VMEM((chunk, chunk), F32),
        pltpu.VMEM((chunk, D), F32),
        pltpu.VMEM((chunk, D), F32),
        pltpu.VMEM((chunk, D), F32),
        pltpu.VMEM((chunk, D), F32),
        pltpu.VMEM((chunk, D), BF),
        pltpu.VMEM((chunk, D), BF),
        pltpu.VMEM((HEADS, chunk, chunk), F32),
    ]
    return pl.pallas_call(
        functools.partial(_mixer_body, chunk=chunk),
        grid=(bsz, nc),
        in_specs=in_specs, out_specs=out_specs, out_shape=out_shape, scratch_shapes=scratch,
        compiler_params=pltpu.CompilerParams(
            dimension_semantics=("arbitrary", "arbitrary"), vmem_limit_bytes=VMEM_LIMIT),
        name="mixer",
    )(x, z, past_a, past_b, h0, s0, lb, lw["conv_a_w"], lw["conv_b_w"], lw["conv_b_b"],
      lw["lru_wa"], lw["lru_ba"], lw["lru_wx"], lw["lru_bx"], lw["lru_lambda"], lw["hgrn_norm_g"],
      lw["w_out_a"], lw["w_out_b"], lw["w_out_c"], lw["w_o"], lw["ln1_g"], lw["ln1_b"])


def _sort_network(n):
    pairs = []

    def merge(lo, m, r):
        step = 2 * r
        if step < m:
            merge(lo, m, step)
            merge(lo + r, m, step)
            pairs.extend((i, i + r) for i in range(lo + r, lo + m - r, step))
        else:
            pairs.append((lo, lo + r))

    def sort(lo, m):
        if m > 1:
            sort(lo, m // 2)
            sort(lo + m // 2, m // 2)
            merge(lo, m, 1)

    sort(0, n)
    return pairs


_SORT16 = _sort_network(TOPK)


def _cmp_exchange(v, i, j):
    if v[j] is None:
        return
    if v[i] is None:
        v[i], v[j] = v[j], None
        return
    v[i], v[j] = jnp.maximum(v[i], v[j]), jnp.minimum(v[i], v[j])


def _sort_desc(v):
    v = list(v) + [None] * (TOPK - len(v))
    for i, j in _SORT16:
        _cmp_exchange(v, i, j)
    return v


def _bitonic_to_desc(v):
    v = list(v)
    d = TOPK // 2
    while d >= 1:
        for i in range(TOPK):
            if i & d == 0:
                _cmp_exchange(v, i, i + d)
        d //= 2
    return v


def _merge_rolled(v, shift):
    w = [None if x is None else pltpu.roll(x, shift, 0) for x in v]
    out = []
    for i in range(TOPK):
        a, b = v[i], w[TOPK - 1 - i]
        out.append(b if a is None else a if b is None else jnp.maximum(a, b))
    return out


def _top16_of_rows(vregs):
    v = _sort_desc(vregs)
    for shift in (4, 2, 1):
        v = _bitonic_to_desc(_merge_rolled(v, shift))
    return v


def _kth16_of_rows(vregs):
    v = _sort_desc(vregs)
    for shift in (4, 2):
        v = _bitonic_to_desc(_merge_rolled(v, shift))
    return functools.reduce(jnp.minimum, _merge_rolled(v, 1))


def _pack_rows(vs, sub):
    out = vs[0]
    for j in range(1, SUBLANES):
        out = jnp.where(sub == j, vs[j], out)
    return out


def _scores_body(x_ref, wq_ref, keys_ref, xt_ref, p1_ref, p2_ref, th_ref, qbuf, *, tn):
    xt_ref[...] = x_ref[...].T.astype(BF)
    qbuf[...] = jnp.dot(x_ref[...].astype(BF), wq_ref[...], preferred_element_type=F32).astype(BF)
    tw = SCORE_TOKENS
    sub = lax.broadcasted_iota(jnp.int32, (SUBLANES, tw), 0)
    n_tc = tn // tw

    def step(idx, carry):
        h = idx // n_tc
        tc = idx % n_tc
        t0 = pl.multiple_of(tc * tw, tw)
        c0 = pl.multiple_of(h * (2 * PEER_DH), 2 * PEER_DH)
        s1 = _nt_dot(keys_ref[h, 0], qbuf[pl.ds(t0, tw), pl.ds(c0, PEER_DH)])
        s2 = _nt_dot(keys_ref[h, 1], qbuf[pl.ds(t0, tw), pl.ds(c0 + PEER_DH, PEER_DH)])
        split = lambda s: [s[r * SUBLANES:(r + 1) * SUBLANES, :] for r in range(N_KEYS // SUBLANES)]
        a = _top16_of_rows(split(s1))
        b = _top16_of_rows(split(s2))
        pa = [jnp.exp(x - a[0]) for x in a]
        pb = [jnp.exp(x - b[0]) for x in b]
        pb_lo, pb_hi = _pack_rows(pb[:SUBLANES], sub), _pack_rows(pb[SUBLANES:], sub)
        pa_hi = _pack_rows(pa[SUBLANES:], sub)
        lhs = [pa[0], pa[0]] + pa[1:SUBLANES] + [pa_hi]
        rhs = [pb_lo, pb_hi] + [pb_lo] * (SUBLANES - 1) + [pb[0]]
        t = [x * y for x, y in zip(lhs, rhs)]
        t16 = _kth16_of_rows(t)
        keep = [x >= t16 for x in t]
        z_sum = jnp.sum(sum(jnp.where(m, x, 0.0) for m, x in zip(keep, t)), axis=0, keepdims=True)
        rz = 0.5 / z_sum
        gates = [(x * rz) * y for x, y in zip(lhs, rhs)]
        th = jnp.min(functools.reduce(jnp.minimum, [jnp.where(m, g, jnp.inf) for m, g in zip(keep, gates)]),
                     axis=0, keepdims=True)
        p1_ref[h, :, pl.ds(t0, tw)] = jnp.exp(s1 - a[0][0:1]) * rz
        p2 = jnp.exp(s2 - b[0][0:1])
        for part in range(tw // LANES):
            p2_ref[h, tc * (tw // LANES) + part] = p2[:, part * LANES:(part + 1) * LANES]
        th_ref[h, :, pl.ds(t0, tw)] = th
        return carry

    lax.fori_loop(0, PEER_HEADS * n_tc, step, 0)


def _peer_scores(x2d, wq_bf, keys_bf, tn, layer):
    n = x2d.shape[0]
    n_tc = tn // LANES
    return pl.pallas_call(
        functools.partial(_scores_body, tn=tn),
        grid=(n // tn,),
        in_specs=[pl.BlockSpec((tn, D), lambda i: (i, 0)),
                  pl.BlockSpec((None, D, 2 * PEER_HEADS * PEER_DH), lambda i: (layer, 0, 0)),
                  pl.BlockSpec((None, PEER_HEADS, 2, N_KEYS, PEER_DH), lambda i: (layer, 0, 0, 0, 0))],
        out_specs=[pl.BlockSpec((D, tn), lambda i: (0, i)),
                   pl.BlockSpec((PEER_HEADS, N_KEYS, tn), lambda i: (0, 0, i)),
                   pl.BlockSpec((PEER_HEADS, n_tc, N_KEYS, LANES), lambda i: (0, i, 0, 0)),
                   pl.BlockSpec((PEER_HEADS, 1, tn), lambda i: (0, 0, i))],
        out_shape=[jax.ShapeDtypeStruct((D, n), BF),
                   jax.ShapeDtypeStruct((PEER_HEADS, N_KEYS, n), F32),
                   jax.ShapeDtypeStruct((PEER_HEADS, n // LANES, N_KEYS, LANES), F32),
                   jax.ShapeDtypeStruct((PEER_HEADS, 1, n), F32)],
        scratch_shapes=[pltpu.VMEM((tn, 2 * PEER_HEADS * PEER_DH), BF)],
        compiler_params=pltpu.CompilerParams(
            dimension_semantics=("arbitrary",), vmem_limit_bytes=VMEM_LIMIT),
        name="peer_scores",
    )(x2d, wq_bf, keys_bf)


def _dense_body(xt_ref, x_ref, u0_ref, un_ref, vp_ref, p1_ref, p2_ref, th_ref, g2_ref, b2_ref,
                y_ref, acc, h_even, h_odd, g_even, g_odd, *, tn, eb, nb):
    j = pl.program_id(1)
    kb = eb // N_KEYS
    n_tc = tn // LANES

    def matmul_jobs(jobs):
        assert len(jobs) % 2 == 0
        for p in range(0, len(jobs), 2):
            pair = jobs[p:p + 2]
            base = (p // 2 % 2) * (MXU_ROWS // 4)
            for k in range(D // MXU_TILE):
                ks = slice(k * MXU_TILE, (k + 1) * MXU_TILE)
                for q, (lhs, rhs, r0, c0, _) in enumerate(pair):
                    pltpu.matmul_push_rhs(rhs[ks, c0:c0 + MXU_TILE], staging_register=k % 2, mxu_index=q)
                shared_lhs = pair[0][0] is pair[1][0] and pair[0][2] == pair[1][2]
                for m in range(MXU_ROWS // MXU_STEP):
                    piece = None
                    for q, (lhs, rhs, r0, c0, _) in enumerate(pair):
                        if piece is None or not shared_lhs:
                            piece = lhs[r0 + m * MXU_STEP:r0 + (m + 1) * MXU_STEP, ks]
                        pltpu.matmul_acc_lhs(base + m * (MXU_STEP // 4), piece, q,
                                             load_staged_rhs=(k % 2) if m == 0 else None)
                    yield
            for q, (lhs, rhs, r0, c0, sink) in enumerate(pair):
                sink(r0, c0, pltpu.matmul_pop(base, (MXU_ROWS, MXU_TILE), F32, q))

    def tiles(lhs, rhs, sink):
        return [(lhs, rhs, r0, c0, sink) for r0 in range(0, eb, MXU_ROWS) for c0 in range(0, tn, MXU_TILE)]

    def store_into(dst):
        def sink(r0, c0, val):
            dst[r0:r0 + MXU_ROWS, c0:c0 + MXU_TILE] = val
        return sink

    def add_into_acc(r0, c0, val):
        acc[r0:r0 + MXU_ROWS, c0:c0 + MXU_TILE] += val

    def run(*programs):
        live = list(programs)
        while live:
            for g in list(live):
                if next(g, StopIteration) is StopIteration:
                    live.remove(g)

    @pl.when(j == 0)
    def _():
        acc[...] = jnp.zeros((D, tn), F32)
        g_odd[:, 0:tn] = jnp.zeros((eb, tn), BF)
        run(matmul_jobs(tiles(u0_ref, xt_ref, store_into(h_even))))

    def gate_build(h_cur, g_cur):
        for k0 in range(0, kb, GATE_KEYS):
            for tc in range(n_tc):
                for half in range(N_KEYS // GATE_ROWS):
                    lanes = slice(tc * LANES, (tc + 1) * LANES)
                    sub_rows = slice(half * GATE_ROWS, (half + 1) * GATE_ROWS)
                    w = [None] * GATE_KEYS
                    for h in range(PEER_HEADS):
                        p2 = p2_ref[h, tc, sub_rows, :]
                        th = th_ref[h, :, lanes]
                        for a in range(GATE_KEYS):
                            prod = p1_ref[h, k0 + a:k0 + a + 1, lanes] * p2
                            kept = jnp.where(prod >= th, prod, 0.0)
                            w[a] = kept if h == 0 else w[a] + kept
                            if a % GATE_YIELD == GATE_YIELD - 1:
                                yield
                    for a in range(GATE_KEYS):
                        r0 = (k0 + a) * N_KEYS + half * GATE_ROWS
                        g_cur[r0:r0 + GATE_ROWS, lanes] = (
                            w[a] * _gelu_twice(h_cur[r0:r0 + GATE_ROWS, lanes])).astype(BF)

    def block_step(h_cur, h_next, g_cur, g_prev):
        run(matmul_jobs(tiles(un_ref, xt_ref, store_into(h_next)) + tiles(vp_ref, g_prev, add_into_acc)),
            gate_build(h_cur, g_cur))

    @pl.when((j < nb) & (j % 2 == 0))
    def _():
        block_step(h_even, h_odd, g_even, g_odd)

    @pl.when((j < nb) & (j % 2 == 1))
    def _():
        block_step(h_odd, h_even, g_odd, g_even)

    @pl.when(j == nb)
    def _():
        g_last = g_even if (nb - 1) % 2 == 0 else g_odd

        run(matmul_jobs(tiles(vp_ref, g_last, add_into_acc)))
        y_ref[...] = _layer_norm(ALPHA * x_ref[...] + acc[...].T, g2_ref[...], b2_ref[...])


def _peer_dense(xt_bf, x2d, u_bf, vt_bf, p1, p2, th, g2, b2, tn, layer):
    n = x2d.shape[0]
    eb = D
    nb = N_EXPERTS // eb
    n_tc = tn // LANES
    assert eb // N_KEYS == SUBLANES and (eb // N_KEYS) % GATE_KEYS == 0
    once = pl.Buffered(1)
    return pl.pallas_call(
        functools.partial(_dense_body, tn=tn, eb=eb, nb=nb),
        grid=(n // tn, nb + 1),
        in_specs=[pl.BlockSpec((D, tn), lambda i, j: (0, i), pipeline_mode=once),
                  pl.BlockSpec((tn, D), lambda i, j: (i, 0), pipeline_mode=once),
                  pl.BlockSpec((None, eb, D), lambda i, j: (layer, 0, 0), pipeline_mode=once),
                  pl.BlockSpec((None, eb, D), lambda i, j: (layer, jnp.minimum(j + 1, nb - 1), 0)),
                  pl.BlockSpec((None, None, D, eb), lambda i, j: (layer, jnp.maximum(j - 1, 0), 0, 0)),
                  pl.BlockSpec((PEER_HEADS, eb // N_KEYS, tn),
                               lambda i, j: (0, jnp.minimum(j, nb - 1), i)),
                  pl.BlockSpec((PEER_HEADS, n_tc, N_KEYS, LANES), lambda i, j: (0, i, 0, 0),
                               pipeline_mode=once),
                  pl.BlockSpec((PEER_HEADS, 1, tn), lambda i, j: (0, 0, i), pipeline_mode=once),
                  pl.BlockSpec((1, D), lambda i, j: (0, 0), pipeline_mode=once),
                  pl.BlockSpec((1, D), lambda i, j: (0, 0), pipeline_mode=once)],
        out_specs=pl.BlockSpec((tn, D), lambda i, j: (i, 0)),
        out_shape=jax.ShapeDtypeStruct((n, D), F32),
        scratch_shapes=[pltpu.VMEM((D, tn), F32),
                        pltpu.VMEM((eb, tn + LANES), F32), pltpu.VMEM((eb, tn + LANES), F32),
                        pltpu.VMEM((eb, tn + LANES), BF), pltpu.VMEM((eb, tn + LANES), BF)],
        compiler_params=pltpu.CompilerParams(
            dimension_semantics=("arbitrary", "arbitrary"), vmem_limit_bytes=VMEM_LIMIT),
        name="peer_dense",
    )(xt_bf, x2d, u_bf, u_bf, vt_bf, p1, p2, th, g2, b2)


def _trunk_layer(x, past_a, past_b, h0, s0, lb, lw, big, layer):
    bsz, t_len, _ = x.shape
    n = bsz * t_len
    z = _in_proj(x.reshape(n, D), big["w_in"], lw["b_in"], layer).reshape(bsz, t_len, IN_COLS)
    x1, new_a, new_b, h_t, s_t = _mixer(x, z, past_a, past_b, h0, s0, lb, lw)
    x1 = x1.reshape(n, D)
    tn = min(n, 512)
    xt, p1, p2, th = _peer_scores(x1, big["peer_wq"], big["peer_keys"], tn, layer)
    x2 = _peer_dense(xt, x1, big["peer_u"], big["peer_vt"], p1, p2, th, lw["ln2_g"], lw["ln2_b"],
                     min(n, DENSE_TOKENS), layer)
    return x2.reshape(bsz, t_len, D), new_a, new_b, h_t.reshape(bsz, D), s_t


def kernel(x_prompt, x_sample, state_conv_a, state_conv_b, state_lru, state_hgrn, w_in, b_in, conv_a_w,
           conv_b_w, conv_b_b, lru_wa, lru_ba, lru_wx, lru_bx, lru_lambda, hgrn_lb_logits, hgrn_norm_g,
           w_out_a, w_out_b, w_out_c, w_o, ln1_g, ln1_b, peer_wq, peer_keys, peer_u, peer_v, ln2_g,
           ln2_b):
    p = jax.nn.softmax(hgrn_lb_logits.astype(F32), axis=0)
    lbs = jnp.cumsum(p, axis=0) - p[0]
    bp = x_prompt.shape[0]
    xp, xs = x_prompt, x_sample
    outs_p, outs_s = [], []
    big = dict(
        w_in=w_in.astype(BF), peer_wq=peer_wq.astype(BF), peer_keys=peer_keys.astype(BF),
        peer_u=peer_u.astype(BF),
        peer_vt=peer_v.reshape(DEPTH, N_EXPERTS // D, D, D).transpose(0, 1, 3, 2).astype(BF))
    for l in range(DEPTH):
        row = lambda a: a[l].reshape(1, -1)
        lw = dict(
            b_in=row(b_in),
            conv_a_w=conv_a_w[l], conv_b_w=conv_b_w[l], conv_b_b=row(conv_b_b),
            lru_wa=lru_wa[l].astype(BF), lru_ba=row(lru_ba), lru_wx=lru_wx[l].astype(BF),
            lru_bx=row(lru_bx), lru_lambda=row(lru_lambda), hgrn_norm_g=row(hgrn_norm_g),
            w_out_a=w_out_a[l].astype(BF), w_out_b=w_out_b[l].astype(BF),
            w_out_c=w_out_c[l].astype(BF), w_o=w_o[l].astype(BF),
            ln1_g=row(ln1_g), ln1_b=row(ln1_b), ln2_g=row(ln2_g), ln2_b=row(ln2_b))
        lb = lbs[l].reshape(1, D)
        xp, pa, pb, ph, ps = _trunk_layer(
            xp, jnp.zeros((bp, CONV_A_W - 1, D), F32), jnp.zeros((bp, CONV_B_W - 1, D), F32),
            jnp.zeros((bp, 1, D), F32), jnp.zeros((bp, HEADS, HD, HD), F32), lb, lw, big, l)
        xs, sa, sb, sh, ss = _trunk_layer(
            xs, state_conv_a[l], state_conv_b[l], state_lru[l][:, None, :], state_hgrn[l], lb, lw, big, l)
        outs_p.append((pa, pb, ph, ps))
        outs_s.append((sa, sb, sh, ss))
    stack = lambda outs, k: jnp.stack([o[k] for o in outs])
    return (xp, xs,
            stack(outs_p, 0), stack(outs_p, 1), stack(outs_p, 2), stack(outs_p, 3),
            stack(outs_s, 0), stack(outs_s, 1), stack(outs_s, 2), stack(outs_s, 3))
```

```python
import functools

import jax
import jax.numpy as jnp
from jax import lax
from jax.experimental import pallas as pl
from jax.experimental.pallas import tpu as pltpu

D = 1024
DEPTH = 2
CONV_A_W = 3
CONV_B_W = 4
LRU_BLOCKS = 8
LRU_BW = D // LRU_BLOCKS
LRU_C = 8.0
HEADS = 8
HD = D // HEADS
PEER_HEADS = 8
PEER_DH = 128
N_KEYS = 128
N_EXPERTS = N_KEYS * N_KEYS
TOPK = 16
ALPHA = (2.0 * DEPTH) ** 0.25
LN_EPS = 1e-5
RMS_EPS = 1e-6
IN_COLS = 12 * D
SUB = 16
INTRA_CAP = 60.0
LANES = 128
SUBLANES = 8
VMEM_LIMIT = 56 * 1024 * 1024
GATE_KEYS = 4
GATE_ROWS = 64
SCORE_TOKENS = 256
DENSE_TOKENS = 1024
MXU_TILE = 256
MXU_ROWS = 512
MXU_STEP = 32
GATE_YIELD = 2

OFF_AB, OFF_AC, OFF_AX, OFF_BX, OFF_BG, OFF_CQ, OFF_CF, OFF_CI, OFF_CG, OFF_GA, OFF_GB, OFF_GC = (
    i * D for i in range(12))

BF = jnp.bfloat16
F32 = jnp.float32


def _tanh_half(x):
    return jnp.tanh(0.5 * x)


def _sigmoid(x):
    return 0.5 * _tanh_half(x) + 0.5


def _silu(x):
    h = 0.5 * x
    return h + h * jnp.tanh(h)


def _gelu_twice(x):
    return x * (1.0 + lax.erf(x * (2.0 ** -0.5)))


def _gelu(x):
    return 0.5 * _gelu_twice(x)


def _layer_norm(x, g, b):
    mu = jnp.mean(x, axis=-1, keepdims=True)
    xc = x - mu
    var = jnp.mean(xc * xc, axis=-1, keepdims=True)
    return xc * lax.rsqrt(var + LN_EPS) * g + b


def _nt_dot(a, b):
    return lax.dot_general(a, b, (((1,), (1,)), ((), ())), preferred_element_type=F32)


def _tn_dot(a, b):
    return lax.dot_general(a, b, (((0,), (0,)), ((), ())), preferred_element_type=F32)


def _scan_rows(a, u, carry, rows8):
    out = []
    for g in range(u.shape[0] // SUBLANES):
        rs = slice(g * SUBLANES, (g + 1) * SUBLANES)
        u_g = u[rs, :]
        a_g = None if a is None else a[rs, :]
        d = 1
        while d < SUBLANES:
            keep = rows8 >= d
            if a_g is None:
                u_g = u_g + jnp.where(keep, pltpu.roll(u_g, d, 0), 0.0)
            else:
                u_g = a_g * jnp.where(keep, pltpu.roll(u_g, d, 0), 0.0) + u_g
                a_g = a_g * jnp.where(keep, pltpu.roll(a_g, d, 0), 1.0)
            d *= 2
        h_g = u_g + carry if a_g is None else u_g + a_g * carry
        out.append(h_g)
        carry = h_g[SUBLANES - 1:SUBLANES, :]
    return jnp.concatenate(out, axis=0)


def _inproj_body(x_ref, w_ref, b_ref, o_ref):
    x = x_ref[...].astype(BF)
    o_ref[...] = jnp.dot(x, w_ref[...], preferred_element_type=F32) + b_ref[...]


def _in_proj(x2d, w_bf, b_row, layer):
    n = x2d.shape[0]
    tm = min(n, 1024)
    tn = 2048
    return pl.pallas_call(
        _inproj_body,
        grid=(n // tm, IN_COLS // tn),
        in_specs=[pl.BlockSpec((tm, D), lambda i, j: (i, 0)),
                  pl.BlockSpec((None, D, tn), lambda i, j: (layer, 0, j)),
                  pl.BlockSpec((1, tn), lambda i, j: (0, j))],
        out_specs=pl.BlockSpec((tm, tn), lambda i, j: (i, j)),
        out_shape=jax.ShapeDtypeStruct((n, IN_COLS), F32),
        compiler_params=pltpu.CompilerParams(
            dimension_semantics=("arbitrary", "arbitrary"), vmem_limit_bytes=VMEM_LIMIT),
        name="in_proj",
    )(x2d, w_bf, b_row)


def _mixer_body(x_ref, z_ref, pa_ref, pb_ref, h0_ref, s0_ref, lb_ref, cwa_ref, cwb_ref, cbb_ref,
                wa_ref, ba_ref, wx_ref, bx_ref, lam_ref, ng_ref, woa_ref, wob_ref, woc_ref, wo_ref,
                g1_ref, b1_ref,
                x1_ref, na_ref, nb_ref, nh_ref, ns_ref,
                ua, ub, hc, st, qs, ks, vs, bs, att, oc, kf, bf, qsil, qe, kd, att_all, *, chunk):
    c = pl.program_id(1)
    last = pl.num_programs(1) - 1
    nsub = chunk // SUB

    rows8 = lax.broadcasted_iota(jnp.int32, (SUBLANES, D), 0)

    def causal_conv(slab, w_ref, width, prev):
        out = []
        below = [pltpu.roll(prev, d, 0) for d in range(1, width)]
        for g in range(chunk // SUBLANES):
            x_g = slab(g)
            y = w_ref[width - 1:width, :] * x_g
            for d in range(1, width):
                cur = pltpu.roll(x_g, d, 0)
                y = y + w_ref[width - 1 - d:width - d, :] * jnp.where(rows8 < d, below[d - 1], cur)
                below[d - 1] = cur
            out.append(y)
        return jnp.concatenate(out, axis=0), x_g

    @pl.when(c == 0)
    def _():
        ua[...] = jnp.zeros((SUBLANES, D), F32)
        ub[...] = jnp.zeros((SUBLANES, D), F32)
        ua[SUBLANES - 2:SUBLANES, :] = pa_ref[...]
        ub[SUBLANES - 3:SUBLANES, :] = pb_ref[...]
        hc[0:1, :] = h0_ref[...]
        for h in range(HEADS):
            st[h] = s0_ref[h].T
        att_all[...] = jnp.zeros((HEADS, chunk, chunk), F32)

    def slab_a(g):
        rs = slice(g * SUBLANES, (g + 1) * SUBLANES)
        return z_ref[rs, OFF_AC:OFF_AC + D] * z_ref[rs, OFF_AX:OFF_AX + D]

    conv_a, last_a = causal_conv(slab_a, cwa_ref, CONV_A_W, ua[...])
    na_ref[...] = last_a[SUBLANES - (CONV_A_W - 1):SUBLANES, :]
    ua[...] = last_a
    y_a = jnp.dot((z_ref[:, OFF_AB:OFF_AB + D] * conv_a).astype(BF), woa_ref[...],
                  preferred_element_type=F32)
    merged2 = (1.0 + _tanh_half(z_ref[:, OFF_GA:OFF_GA + D])) * y_a

    conv_b, last_b = causal_conv(
        lambda g: z_ref[g * SUBLANES:(g + 1) * SUBLANES, OFF_BX:OFF_BX + D], cwb_ref, CONV_B_W, ub[...])
    xl = conv_b + cbb_ref[...]
    nb_ref[...] = last_b[SUBLANES - (CONV_B_W - 1):SUBLANES, :]
    ub[...] = last_b

    xl_bf = xl.astype(BF)
    ra = jnp.concatenate(
        [jnp.dot(xl_bf[:, n * LRU_BW:(n + 1) * LRU_BW], wa_ref[n], preferred_element_type=F32)
         for n in range(LRU_BLOCKS)], axis=1)
    rx = jnp.concatenate(
        [jnp.dot(xl_bf[:, n * LRU_BW:(n + 1) * LRU_BW], wx_ref[n], preferred_element_type=F32)
         for n in range(LRU_BLOCKS)], axis=1)
    gi = _sigmoid(rx + bx_ref[...])
    nlam = -lam_ref[...]
    softplus = jnp.maximum(nlam, 0.0) + jnp.log1p(jnp.exp(-jnp.abs(nlam)))
    log_a = ((-0.5 * LRU_C) * softplus) * (1.0 + _tanh_half(ra + ba_ref[...]))
    a = jnp.exp(log_a)
    uu = jnp.sqrt(jnp.tanh(-log_a) * (a * a + 1.0)) * (gi * xl)
    hh = _scan_rows(a, uu, hc[0:1, :], rows8)
    h_last = hh[chunk - 1:chunk, :]
    hc[0:1, :] = h_last
    nh_ref[...] = h_last
    y_b = jnp.dot((_gelu(z_ref[:, OFF_BG:OFF_BG + D]) * hh).astype(BF), wob_ref[...],
                  preferred_element_type=F32)
    merged2 = merged2 + (1.0 + _tanh_half(z_ref[:, OFF_GB:OFF_GB + D])) * y_b

    cols_a = lax.broadcasted_iota(jnp.int32, (SUB, chunk), 1)
    sub_rows = lax.broadcasted_iota(jnp.int32, (SUB, 1), 0)
    f_all = lb_ref[...] + (1.0 - lb_ref[...]) / (1.0 + jnp.exp(-z_ref[:, OFF_CF:OFF_CF + D]))
    kf[...] = 1.0 - f_all
    bf[...] = _scan_rows(None, jnp.log(f_all), jnp.zeros((1, D), F32), rows8)
    intra_bound = jnp.max(functools.reduce(jnp.maximum, [
        (bf[i * SUB - 1:i * SUB, :] if i else 0.0) - bf[(i + 1) * SUB - 1:(i + 1) * SUB, :]
        for i in range(nsub)]))

    def head_step(h, carry):
        off = pl.multiple_of(h * HD, HD)
        k = kf[:, pl.ds(off, HD)]
        b = bf[:, pl.ds(off, HD)]
        cq = z_ref[:, pl.ds(OFF_CQ + off, HD)]
        q = _silu(cq)
        v = z_ref[:, pl.ds(OFF_CI + off, HD)]
        qs[...] = q
        ks[...] = k
        vs[...] = v
        bs[...] = b
        st_h = st[h]
        b_last = b[chunk - 1:chunk, :]
        o = _nt_dot((q * jnp.exp(b)).astype(BF), st_h.astype(BF))
        kdec = k * jnp.exp(b_last - b)
        st[h] = st_h * jnp.exp(b_last) + _tn_dot(v.astype(BF), kdec.astype(BF))

        def sub_start(i):
            return bs[i * SUB - 1:i * SUB, :] if i else jnp.zeros((1, HD), F32)

        def intra_direct():
            att[0:SUB, :] = jnp.zeros((SUB, chunk), F32)
            for i in range(1, nsub):
                lo, hi = i * SUB, (i + 1) * SUB
                q_i = qs[lo:hi, :] * jnp.exp(bs[lo:hi, :] - sub_start(i))
                k_i = k * jnp.exp(jnp.minimum(sub_start(i) - b, 0.0))
                a_i = _nt_dot(q_i.astype(BF), k_i.astype(BF))
                att[lo:hi, :] = jnp.where(cols_a < lo, a_i, 0.0)
            o_off = jnp.dot(att[...].astype(BF), v.astype(BF), preferred_element_type=F32)
            o_parts = []
            for i in range(nsub):
                q_i = qs[i * SUB:(i + 1) * SUB, :]
                b_i = bs[i * SUB:(i + 1) * SUB, :]
                o_i = jnp.zeros((SUB, HD), F32)
                for s in range(SUB):
                    r0 = i * SUB + s
                    p = q_i * ks[r0:r0 + 1, :] * jnp.exp(jnp.minimum(b_i - bs[r0:r0 + 1, :], 0.0))
                    a_col = jnp.sum(p, axis=-1, keepdims=True)
                    a_col = jnp.where(sub_rows >= s, a_col, 0.0)
                    o_i = o_i + a_col * vs[r0:r0 + 1, :]
                o_parts.append(o_i)
            return o_off + jnp.concatenate(o_parts, axis=0)

        o = o + intra_direct()
        oc[:, pl.ds(off, HD)] = o * lax.rsqrt(jnp.mean(o * o, axis=-1, keepdims=True) + RMS_EPS) * ng_ref[...]
        return carry

    def hgrn_factored():
        cq = z_ref[:, OFF_CQ:OFF_CQ + D]
        qsil[...] = _silu(cq)
        b_last = bf[chunk - 1:chunk, :]
        qe[...] = (qsil[...] * jnp.exp(bf[...])).astype(BF)
        kd[...] = (kf[...] * jnp.exp(b_last - bf[...])).astype(BF)
        for i in range(nsub):
            lo, hi = i * SUB, (i + 1) * SUB
            start = bf[lo - 1:lo, :] if i else 0.0
            q_i = (qsil[lo:hi, :] * jnp.exp(bf[lo:hi, :] - start)).astype(BF)
            k_i = (kf[0:hi, :] * jnp.exp(jnp.minimum(start - bf[0:hi, :], INTRA_CAP))).astype(BF)
            causal = (lax.broadcasted_iota(jnp.int32, (SUB, hi), 1)
                      <= lax.broadcasted_iota(jnp.int32, (SUB, hi), 0) + lo)
            for h in range(HEADS):
                hs = slice(h * HD, (h + 1) * HD)
                att_all[h, lo:hi, 0:hi] = jnp.where(causal, _nt_dot(q_i[:, hs], k_i[:, hs]), 0.0)
        for h in range(HEADS):
            hs = slice(h * HD, (h + 1) * HD)
            v_h = z_ref[:, OFF_CI + h * HD:OFF_CI + (h + 1) * HD].astype(BF)
            st_h = st[h]
            o = (_nt_dot(qe[:, hs], st_h.astype(BF))
                 + jnp.dot(att_all[h].astype(BF), v_h, preferred_element_type=F32))
            st[h] = st_h * jnp.exp(b_last[:, hs]) + _tn_dot(v_h, kd[:, hs])
            oc[:, hs] = o * lax.rsqrt(jnp.mean(o * o, axis=-1, keepdims=True) + RMS_EPS) * ng_ref[...]

    def hgrn_direct():
        lax.fori_loop(0, HEADS, head_step, 0)

    lax.cond(intra_bound <= INTRA_CAP, hgrn_factored, hgrn_direct)

    @pl.when(c == last)
    def _():
        for h in range(HEADS):
            ns_ref[h] = st[h].T

    cg = z_ref[:, OFF_CG:OFF_CG + D]
    y_c = jnp.dot((oc[...] * _silu(cg)).astype(BF), woc_ref[...], preferred_element_type=F32)
    merged2 = merged2 + (1.0 + _tanh_half(z_ref[:, OFF_GC:OFF_GC + D])) * y_c
    res = ALPHA * x_ref[...] + jnp.dot((0.5 * merged2).astype(BF), wo_ref[...], preferred_element_type=F32)
    x1_ref[...] = _layer_norm(res, g1_ref[...], b1_ref[...])


def _mixer(x, z, past_a, past_b, h0, s0, lb, lw):
    bsz, t_len, _ = x.shape
    chunk = min(t_len, 128)
    assert t_len % chunk == 0 and chunk % SUB == 0
    nc = t_len // chunk
    full2 = lambda shape: pl.BlockSpec(shape, lambda b, c: (0, 0))
    full3 = lambda shape: pl.BlockSpec(shape, lambda b, c: (0, 0, 0))
    in_specs = [
        pl.BlockSpec((None, chunk, D), lambda b, c: (b, c, 0)),
        pl.BlockSpec((None, chunk, IN_COLS), lambda b, c: (b, c, 0)),
        pl.BlockSpec((None, CONV_A_W - 1, D), lambda b, c: (b, 0, 0)),
        pl.BlockSpec((None, CONV_B_W - 1, D), lambda b, c: (b, 0, 0)),
        pl.BlockSpec((None, 1, D), lambda b, c: (b, 0, 0)),
        pl.BlockSpec((None, HEADS, HD, HD), lambda b, c: (b, 0, 0, 0)),
        full2((1, D)),
        full2((CONV_A_W, D)), full2((CONV_B_W, D)), full2((1, D)),
        full3((LRU_BLOCKS, LRU_BW, LRU_BW)), full2((1, D)),
        full3((LRU_BLOCKS, LRU_BW, LRU_BW)), full2((1, D)),
        full2((1, D)), full2((1, HD)),
        full2((D, D)), full2((D, D)), full2((D, D)), full2((D, D)),
        full2((1, D)), full2((1, D)),
    ]
    out_specs = [
        pl.BlockSpec((None, chunk, D), lambda b, c: (b, c, 0)),
        pl.BlockSpec((None, CONV_A_W - 1, D), lambda b, c: (b, 0, 0)),
        pl.BlockSpec((None, CONV_B_W - 1, D), lambda b, c: (b, 0, 0)),
        pl.BlockSpec((None, 1, D), lambda b, c: (b, 0, 0)),
        pl.BlockSpec((None, HEADS, HD, HD), lambda b, c: (b, 0, 0, 0)),
    ]
    out_shape = [
        jax.ShapeDtypeStruct((bsz, t_len, D), F32),
        jax.ShapeDtypeStruct((bsz, CONV_A_W - 1, D), F32),
        jax.ShapeDtypeStruct((bsz, CONV_B_W - 1, D), F32),
        jax.ShapeDtypeStruct((bsz, 1, D), F32),
        jax.ShapeDtypeStruct((bsz, HEADS, HD, HD), F32),
    ]
    scratch = [
        pltpu.VMEM((SUBLANES, D), F32),
        pltpu.VMEM((SUBLANES, D), F32),
        pltpu.VMEM((SUBLANES, D), F32),
        pltpu.VMEM((HEADS, HD, HD), F32),
        pltpu.VMEM((chunk, HD), F32),
        pltpu.VMEM((chunk, HD), F32),
        pltpu.VMEM((chunk, HD), F32),
        pltpu.VMEM((chunk, HD), F32),
        pltpu.VMEM((chunk, chunk), F32),
        pltpu.VMEM((chunk, D), F32),
        pltpu.VMEM((chunk, D), F32),
        pltpu.VMEM((chunk, D), F32),
        pltpu.VMEM((chunk, D), F32),
        pltpu.VMEM((chunk, D), BF),
        pltpu.VMEM((chunk, D), BF),
        pltpu.VMEM((HEADS, chunk, chunk), F32),
    ]
    return pl.pallas_call(
        functools.partial(_mixer_body, chunk=chunk),
        grid=(bsz, nc),
        in_specs=in_specs, out_specs=out_specs, out_shape=out_shape, scratch_shapes=scratch,
        compiler_params=pltpu.CompilerParams(
            dimension_semantics=("arbitrary", "arbitrary"), vmem_limit_bytes=VMEM_LIMIT),
        name="mixer",
    )(x, z, past_a, past_b, h0, s0, lb, lw["conv_a_w"], lw["conv_b_w"], lw["conv_b_b"],
      lw["lru_wa"], lw["lru_ba"], lw["lru_wx"], lw["lru_bx"], lw["lru_lambda"], lw["hgrn_norm_g"],
      lw["w_out_a"], lw["w_out_b"], lw["w_out_c"], lw["w_o"], lw["ln1_g"], lw["ln1_b"])


def _sort_network(n):
    pairs = []

    def merge(lo, m, r):
        step = 2 * r
        if step < m:
            merge(lo, m, step)
            merge(lo + r, m, step)
            pairs.extend((i, i + r) for i in range(lo + r, lo + m - r, step))
        else:
            pairs.append((lo, lo + r))

    def sort(lo, m):
        if m > 1:
            sort(lo, m // 2)
            sort(lo + m // 2, m // 2)
            merge(lo, m, 1)

    sort(0, n)
    return pairs


_SORT16 = _sort_network(TOPK)


def _cmp_exchange(v, i, j):
    if v[j] is None:
        return
    if v[i] is None:
        v[i], v[j] = v[j], None
        return
    v[i], v[j] = jnp.maximum(v[i], v[j]), jnp.minimum(v[i], v[j])


def _sort_desc(v):
    v = list(v) + [None] * (TOPK - len(v))
    for i, j in _SORT16:
        _cmp_exchange(v, i, j)
    return v


def _bitonic_to_desc(v):
    v = list(v)
    d = TOPK // 2
    while d >= 1:
        for i in range(TOPK):
            if i & d == 0:
                _cmp_exchange(v, i, i + d)
        d //= 2
    return v


def _merge_rolled(v, shift):
    w = [None if x is None else pltpu.roll(x, shift, 0) for x in v]
    out = []
    for i in range(TOPK):
        a, b = v[i], w[TOPK - 1 - i]
        out.append(b if a is None else a if b is None else jnp.maximum(a, b))
    return out


def _top16_of_rows(vregs):
    v = _sort_desc(vregs)
    for shift in (4, 2, 1):
        v = _bitonic_to_desc(_merge_rolled(v, shift))
    return v


def _kth16_of_rows(vregs):
    v = _sort_desc(vregs)
    for shift in (4, 2):
        v = _bitonic_to_desc(_merge_rolled(v, shift))
    return functools.reduce(jnp.minimum, _merge_rolled(v, 1))


def _pack_rows(vs, sub):
    out = vs[0]
    for j in range(1, SUBLANES):
        out = jnp.where(sub == j, vs[j], out)
    return out


def _scores_body(x_ref, wq_ref, keys_ref, xt_ref, p1_ref, p2_ref, th_ref, qbuf, *, tn):
    xt_ref[...] = x_ref[...].T.astype(BF)
    qbuf[...] = jnp.dot(x_ref[...].astype(BF), wq_ref[...], preferred_element_type=F32).astype(BF)
    tw = SCORE_TOKENS
    sub = lax.broadcasted_iota(jnp.int32, (SUBLANES, tw), 0)
    n_tc = tn // tw

    def step(idx, carry):
        h = idx // n_tc
        tc = idx % n_tc
        t0 = pl.multiple_of(tc * tw, tw)
        c0 = pl.multiple_of(h * (2 * PEER_DH), 2 * PEER_DH)
        s1 = _nt_dot(keys_ref[h, 0], qbuf[pl.ds(t0, tw), pl.ds(c0, PEER_DH)])
        s2 = _nt_dot(keys_ref[h, 1], qbuf[pl.ds(t0, tw), pl.ds(c0 + PEER_DH, PEER_DH)])
        split = lambda s: [s[r * SUBLANES:(r + 1) * SUBLANES, :] for r in range(N_KEYS // SUBLANES)]
        a = _top16_of_rows(split(s1))
        b = _top16_of_rows(split(s2))
        pa = [jnp.exp(x - a[0]) for x in a]
        pb = [jnp.exp(x - b[0]) for x in b]
        pb_lo, pb_hi = _pack_rows(pb[:SUBLANES], sub), _pack_rows(pb[SUBLANES:], sub)
        pa_hi = _pack_rows(pa[SUBLANES:], sub)
        lhs = [pa[0], pa[0]] + pa[1:SUBLANES] + [pa_hi]
        rhs = [pb_lo, pb_hi] + [pb_lo] * (SUBLANES - 1) + [pb[0]]
        t = [x * y for x, y in zip(lhs, rhs)]
        t16 = _kth16_of_rows(t)
        keep = [x >= t16 for x in t]
        z_sum = jnp.sum(sum(jnp.where(m, x, 0.0) for m, x in zip(keep, t)), axis=0, keepdims=True)
        rz = 0.5 / z_sum
        gates = [(x * rz) * y for x, y in zip(lhs, rhs)]
        th = jnp.min(functools.reduce(jnp.minimum, [jnp.where(m, g, jnp.inf) for m, g in zip(keep, gates)]),
                     axis=0, keepdims=True)
        p1_ref[h, :, pl.ds(t0, tw)] = jnp.exp(s1 - a[0][0:1]) * rz
        p2 = jnp.exp(s2 - b[0][0:1])
        for part in range(tw // LANES):
            p2_ref[h, tc * (tw // LANES) + part] = p2[:, part * LANES:(part + 1) * LANES]
        th_ref[h, :, pl.ds(t0, tw)] = th
        return carry

    lax.fori_loop(0, PEER_HEADS * n_tc, step, 0)


def _peer_scores(x2d, wq_bf, keys_bf, tn, layer):
    n = x2d.shape[0]
    n_tc = tn // LANES
    return pl.pallas_call(
        functools.partial(_scores_body, tn=tn),
        grid=(n // tn,),
        in_specs=[pl.BlockSpec((tn, D), lambda i: (i, 0)),
                  pl.BlockSpec((None, D, 2 * PEER_HEADS * PEER_DH), lambda i: (layer, 0, 0)),
                  pl.BlockSpec((None, PEER_HEADS, 2, N_KEYS, PEER_DH), lambda i: (layer, 0, 0, 0, 0))],
        out_specs=[pl.BlockSpec((D, tn), lambda i: (0, i)),
                   pl.BlockSpec((PEER_HEADS, N_KEYS, tn), lambda i: (0, 0, i)),
                   pl.BlockSpec((PEER_HEADS, n_tc, N_KEYS, LANES), lambda i: (0, i, 0, 0)),
                   pl.BlockSpec((PEER_HEADS, 1, tn), lambda i: (0, 0, i))],
        out_shape=[jax.ShapeDtypeStruct((D, n), BF),
                   jax.ShapeDtypeStruct((PEER_HEADS, N_KEYS, n), F32),
                   jax.ShapeDtypeStruct((PEER_HEADS, n // LANES, N_KEYS, LANES), F32),
                   jax.ShapeDtypeStruct((PEER_HEADS, 1, n), F32)],
        scratch_shapes=[pltpu.VMEM((tn, 2 * PEER_HEADS * PEER_DH), BF)],
        compiler_params=pltpu.CompilerParams(
            dimension_semantics=("arbitrary",), vmem_limit_bytes=VMEM_LIMIT),
        name="peer_scores",
    )(x2d, wq_bf, keys_bf)


def _dense_body(xt_ref, x_ref, u0_ref, un_ref, vp_ref, p1_ref, p2_ref, th_ref, g2_ref, b2_ref,
                y_ref, acc, h_even, h_odd, g_even, g_odd, *, tn, eb, nb):
    j = pl.program_id(1)
    kb = eb // N_KEYS
    n_tc = tn // LANES

    def matmul_jobs(jobs):
        assert len(jobs) % 2 == 0
        for p in range(0, len(jobs), 2):
            pair = jobs[p:p + 2]
            base = (p // 2 % 2) * (MXU_ROWS // 4)
            for k in range(D // MXU_TILE):
                ks = slice(k * MXU_TILE, (k + 1) * MXU_TILE)
                for q, (lhs, rhs, r0, c0, _) in enumerate(pair):
                    pltpu.matmul_push_rhs(rhs[ks, c0:c0 + MXU_TILE], staging_register=k % 2, mxu_index=q)
                shared_lhs = pair[0][0] is pair[1][0] and pair[0][2] == pair[1][2]
                for m in range(MXU_ROWS // MXU_STEP):
                    piece = None
                    for q, (lhs, rhs, r0, c0, _) in enumerate(pair):
                        if piece is None or not shared_lhs:
                            piece = lhs[r0 + m * MXU_STEP:r0 + (m + 1) * MXU_STEP, ks]
                        pltpu.matmul_acc_lhs(base + m * (MXU_STEP // 4), piece, q,
                                             load_staged_rhs=(k % 2) if m == 0 else None)
                    yield
            for q, (lhs, rhs, r0, c0, sink) in enumerate(pair):
                sink(r0, c0, pltpu.matmul_pop(base, (MXU_ROWS, MXU_TILE), F32, q))

    def tiles(lhs, rhs, sink):
        return [(lhs, rhs, r0, c0, sink) for r0 in range(0, eb, MXU_ROWS) for c0 in range(0, tn, MXU_TILE)]

    def store_into(dst):
        def sink(r0, c0, val):
            dst[r0:r0 + MXU_ROWS, c0:c0 + MXU_TILE] = val
        return sink

    def add_into_acc(r0, c0, val):
        acc[r0:r0 + MXU_ROWS, c0:c0 + MXU_TILE] += val

    def run(*programs):
        live = list(programs)
        while live:
            for g in list(live):
                if next(g, StopIteration) is StopIteration:
                    live.remove(g)

    @pl.when(j == 0)
    def _():
        acc[...] = jnp.zeros((D, tn), F32)
        g_odd[:, 0:tn] = jnp.zeros((eb, tn), BF)
        run(matmul_jobs(tiles(u0_ref, xt_ref, store_into(h_even))))

    def gate_build(h_cur, g_cur):
        for k0 in range(0, kb, GATE_KEYS):
            for tc in range(n_tc):
                for half in range(N_KEYS // GATE_ROWS):
                    lanes = slice(tc * LANES, (tc + 1) * LANES)
                    sub_rows = slice(half * GATE_ROWS, (half + 1) * GATE_ROWS)
                    w = [None] * GATE_KEYS
                    for h in range(PEER_HEADS):
                        p2 = p2_ref[h, tc, sub_rows, :]
                        th = th_ref[h, :, lanes]
                        for a in range(GATE_KEYS):
                            prod = p1_ref[h, k0 + a:k0 + a + 1, lanes] * p2
                            kept = jnp.where(prod >= th, prod, 0.0)
                            w[a] = kept if h == 0 else w[a] + kept
                            if a % GATE_YIELD == GATE_YIELD - 1:
                                yield
                    for a in range(GATE_KEYS):
                        r0 = (k0 + a) * N_KEYS + half * GATE_ROWS
                        g_cur[r0:r0 + GATE_ROWS, lanes] = (
                            w[a] * _gelu_twice(h_cur[r0:r0 + GATE_ROWS, lanes])).astype(BF)

    def block_step(h_cur, h_next, g_cur, g_prev):
        run(matmul_jobs(tiles(un_ref, xt_ref, store_into(h_next)) + tiles(vp_ref, g_prev, add_into_acc)),
            gate_build(h_cur, g_cur))

    @pl.when((j < nb) & (j % 2 == 0))
    def _():
        block_step(h_even, h_odd, g_even, g_odd)

    @pl.when((j < nb) & (j % 2 == 1))
    def _():
        block_step(h_odd, h_even, g_odd, g_even)

    @pl.when(j == nb)
    def _():
        g_last = g_even if (nb - 1) % 2 == 0 else g_odd

        finished = []

        def sink(r0, c0, val):
            add_into_acc(r0, c0, val)
            if r0 + MXU_ROWS == eb:
                finished.append(c0)

        def norm_program():
            emitted = 0
            while emitted < tn // MXU_TILE:
                if len(finished) > emitted:
                    c0 = finished[emitted]
                    emitted += 1
                    for t0 in range(c0, c0 + MXU_TILE, LANES):
                        y_ref[t0:t0 + LANES, :] = _layer_norm(
                            ALPHA * x_ref[t0:t0 + LANES, :] + acc[:, t0:t0 + LANES].T, g2_ref[...], b2_ref[...])
                        yield
                else:
                    yield

        jobs = [(vp_ref, g_last, r0, c0, sink)
                for c0 in range(0, tn, MXU_TILE) for r0 in range(0, eb, MXU_ROWS)]
        run(matmul_jobs(jobs), norm_program())


def _peer_dense(xt_bf, x2d, u_bf, vt_bf, p1, p2, th, g2, b2, tn, layer):
    n = x2d.shape[0]
    eb = D
    nb = N_EXPERTS // eb
    n_tc = tn // LANES
    assert eb // N_KEYS == SUBLANES and (eb // N_KEYS) % GATE_KEYS == 0
    once = pl.Buffered(1)
    return pl.pallas_call(
        functools.partial(_dense_body, tn=tn, eb=eb, nb=nb),
        grid=(n // tn, nb + 1),
        in_specs=[pl.BlockSpec((D, tn), lambda i, j: (0, i), pipeline_mode=once),
                  pl.BlockSpec((tn, D), lambda i, j: (i, 0), pipeline_mode=once),
                  pl.BlockSpec((None, eb, D), lambda i, j: (layer, 0, 0), pipeline_mode=once),
                  pl.BlockSpec((None, eb, D), lambda i, j: (layer, jnp.minimum(j + 1, nb - 1), 0)),
                  pl.BlockSpec((None, None, D, eb), lambda i, j: (layer, jnp.maximum(j - 1, 0), 0, 0)),
                  pl.BlockSpec((PEER_HEADS, eb // N_KEYS, tn),
                               lambda i, j: (0, jnp.minimum(j, nb - 1), i)),
                  pl.BlockSpec((PEER_HEADS, n_tc, N_KEYS, LANES), lambda i, j: (0, i, 0, 0),
                               pipeline_mode=once),
                  pl.BlockSpec((PEER_HEADS, 1, tn), lambda i, j: (0, 0, i), pipeline_mode=once),
                  pl.BlockSpec((1, D), lambda i, j: (0, 0), pipeline_mode=once),
                  pl.BlockSpec((1, D), lambda i, j: (0, 0), pipeline_mode=once)],
        out_specs=pl.BlockSpec((tn, D), lambda i, j: (i, 0)),
        out_shape=jax.ShapeDtypeStruct((n, D), F32),
        scratch_shapes=[pltpu.VMEM((D, tn), F32),
                        pltpu.VMEM((eb, tn + LANES), F32), pltpu.VMEM((eb, tn + LANES), F32),
                        pltpu.VMEM((eb, tn + LANES), BF), pltpu.VMEM((eb, tn + LANES), BF)],
        compiler_params=pltpu.CompilerParams(
            dimension_semantics=("arbitrary", "arbitrary"), vmem_limit_bytes=VMEM_LIMIT),
        name="peer_dense",
    )(xt_bf, x2d, u_bf, u_bf, vt_bf, p1, p2, th, g2, b2)


def _trunk_layer(x, past_a, past_b, h0, s0, lb, lw, big, layer):
    bsz, t_len, _ = x.shape
    n = bsz * t_len
    z = _in_proj(x.reshape(n, D), big["w_in"], lw["b_in"], layer).reshape(bsz, t_len, IN_COLS)
    x1, new_a, new_b, h_t, s_t = _mixer(x, z, past_a, past_b, h0, s0, lb, lw)
    x1 = x1.reshape(n, D)
    tn = min(n, 512)
    xt, p1, p2, th = _peer_scores(x1, big["peer_wq"], big["peer_keys"], tn, layer)
    x2 = _peer_dense(xt, x1, big["peer_u"], big["peer_vt"], p1, p2, th, lw["ln2_g"], lw["ln2_b"],
                     min(n, DENSE_TOKENS), layer)
    return x2.reshape(bsz, t_len, D), new_a, new_b, h_t.reshape(bsz, D), s_t


def kernel(x_prompt, x_sample, state_conv_a, state_conv_b, state_lru, state_hgrn, w_in, b_in, conv_a_w,
           conv_b_w, conv_b_b, lru_wa, lru_ba, lru_wx, lru_bx, lru_lambda, hgrn_lb_logits, hgrn_norm_g,
           w_out_a, w_out_b, w_out_c, w_o, ln1_g, ln1_b, peer_wq, peer_keys, peer_u, peer_v, ln2_g,
           ln2_b):
    p = jax.nn.softmax(hgrn_lb_logits.astype(F32), axis=0)
    lbs = jnp.cumsum(p, axis=0) - p[0]
    bp = x_prompt.shape[0]
    xp, xs = x_prompt, x_sample
    outs_p, outs_s = [], []
    big = dict(
        w_in=w_in.astype(BF), peer_wq=peer_wq.astype(BF), peer_keys=peer_keys.astype(BF),
        peer_u=peer_u.astype(BF),
        peer_vt=peer_v.reshape(DEPTH, N_EXPERTS // D, D, D).transpose(0, 1, 3, 2).astype(BF))
    for l in range(DEPTH):
        row = lambda a: a[l].reshape(1, -1)
        lw = dict(
            b_in=row(b_in),
            conv_a_w=conv_a_w[l], conv_b_w=conv_b_w[l], conv_b_b=row(conv_b_b),
            lru_wa=lru_wa[l].astype(BF), lru_ba=row(lru_ba), lru_wx=lru_wx[l].astype(BF),
            lru_bx=row(lru_bx), lru_lambda=row(lru_lambda), hgrn_norm_g=row(hgrn_norm_g),
            w_out_a=w_out_a[l].astype(BF), w_out_b=w_out_b[l].astype(BF),
            w_out_c=w_out_c[l].astype(BF), w_o=w_o[l].astype(BF),
            ln1_g=row(ln1_g), ln1_b=row(ln1_b), ln2_g=row(ln2_g), ln2_b=row(ln2_b))
        lb = lbs[l].reshape(1, D)
        xp, pa, pb, ph, ps = _trunk_layer(
            xp, jnp.zeros((bp, CONV_A_W - 1, D), F32), jnp.zeros((bp, CONV_B_W - 1, D), F32),
            jnp.zeros((bp, 1, D), F32), jnp.zeros((bp, HEADS, HD, HD), F32), lb, lw, big, l)
        xs, sa, sb, sh, ss = _trunk_layer(
            xs, state_conv_a[l], state_conv_b[l], state_lru[l][:, None, :], state_hgrn[l], lb, lw, big, l)
        outs_p.append((pa, pb, ph, ps))
        outs_s.append((sa, sb, sh, ss))
    stack = lambda outs, k: jnp.stack([o[k] for o in outs])
    return (xp, xs,
            stack(outs_p, 0), stack(outs_p, 1), stack(outs_p, 2), stack(outs_p, 3),
            stack(outs_s, 0), stack(outs_s, 1), stack(outs_s, 2), stack(outs_s, 3))
```
